```python
import math
import jax, jax.numpy as jnp
from jax import lax
import numpy as np


D_MODEL = 2048
BATCH = 4
SEQ = 4096
DEPTH = 2

GRID_W = 64
CTX_LEN = 256
NORM_EPS = 1e-6
N_BRANCH = 4
BRANCH_WIDTH = D_MODEL // 2

SSD_HEADDIM = 64
SSD_D_INNER = BRANCH_WIDTH
SSD_HEADS = SSD_D_INNER // SSD_HEADDIM
SSD_GROUPS = 4
SSD_STATE = 128
SSD_CONV_DIM = SSD_D_INNER + 2 * SSD_GROUPS * SSD_STATE
SSD_CONV_K = 5
SSD_CHUNK = 128

NA_HEAD_DIM = 64
NA_HEADS = BRANCH_WIDTH // NA_HEAD_DIM
NA_WIN_R = 8
NA_WIN_C = 16

GLA_HEADS = 4
GLA_V_DIM = BRANCH_WIDTH
GLA_K_DIM = BRANCH_WIDTH // 2
GLA_HEAD_K = GLA_K_DIM // GLA_HEADS
GLA_HEAD_V = GLA_V_DIM // GLA_HEADS
GLA_GATE_RANK = 16
GLA_TAU = 16.0
GLA_CHUNK = 64

SWA_HEAD_DIM = 64
SWA_HEADS = BRANCH_WIDTH // SWA_HEAD_DIM
SWA_KV_HEADS = 4
SWA_WINDOW = 128
SWA_BLOCK = 128
ROPE_BASE = 10000.0
ROPE_AXIS_DIM = SWA_HEAD_DIM // 2

PEER_HEADS = 8
PEER_NKEYS = 128
PEER_EXPERTS = PEER_NKEYS * PEER_NKEYS
PEER_QDIM = 256
PEER_TOPK = 16
PEER_BLOCK = 128

IN_SPLITS = (SSD_D_INNER, SSD_CONV_DIM, 2 * SSD_HEADS,
             3 * BRANCH_WIDTH,
             GLA_K_DIM, GLA_K_DIM, GLA_V_DIM, GLA_V_DIM, 2 * GLA_GATE_RANK,
             SWA_HEADS * SWA_HEAD_DIM, SWA_KV_HEADS * SWA_HEAD_DIM, SWA_KV_HEADS * SWA_HEAD_DIM,
             N_BRANCH * D_MODEL)
IN_WIDTH = sum(IN_SPLITS)

F32 = jnp.float32

kernel_name = 'hybrid_ssd_na_gla_swa_peer_block'


def rms_norm(x, g):
    xf = x.astype(F32)
    y = xf * lax.rsqrt(jnp.mean(xf * xf, axis=-1, keepdims=True) + NORM_EPS)
    return (y * g.astype(F32)).astype(x.dtype)


def modulate(h, shift, scale):
    return h * (1 + scale) + shift


def split_cols(p):
    idx = np.cumsum(np.array(IN_SPLITS))[:-1].tolist()
    return jnp.split(p, idx, axis=-1)


def flip(t):
    return jnp.flip(t, axis=1)


def dwconv_centered(x, w, b):
    k = w.shape[0]
    y = lax.conv_general_dilated(x, w[:, None, :].astype(x.dtype), window_strides=(1,),
                                 padding=[(k // 2, k // 2)],
                                 dimension_numbers=('NWC', 'WIO', 'NWC'),
                                 feature_group_count=x.shape[-1])
    return y + b.astype(x.dtype)


def axial_rope_tables(s):
    t = jnp.arange(s)
    row = (t // GRID_W).astype(F32)
    col = (t % GRID_W).astype(F32)
    nf = ROPE_AXIS_DIM // 2
    inv = ROPE_BASE ** (-jnp.arange(nf, dtype=F32) / nf)
    ar = row[:, None] * inv
    ac = col[:, None] * inv
    return (jnp.cos(ar), jnp.sin(ar), jnp.cos(ac), jnp.sin(ac))


def apply_axial_rope(x, tabs):
    cr, sr, cc, sc = tabs
    xf = x.astype(F32)

    def rot(xa, cos, sin):
        x1, x2 = jnp.split(xa, 2, axis=-1)
        cos = cos[:, None, :]
        sin = sin[:, None, :]
        return jnp.concatenate([x1 * cos - x2 * sin, x1 * sin + x2 * cos], axis=-1)

    return jnp.concatenate([rot(xf[..., :ROPE_AXIS_DIM], cr, sr),
                            rot(xf[..., ROPE_AXIS_DIM:], cc, sc)], axis=-1).astype(x.dtype)


def ctx_attend(q, k, v, sink):
    bsz, n, hq, dh = q.shape
    hk = k.shape[2]
    grp = hq // hk
    qg = q.reshape(bsz, n, hk, grp, dh)
    logits = jnp.einsum('bqhgd,bkhd->bhgqk', qg, k).astype(F32) * dh ** -0.5
    if sink is not None:
        s_sink = jnp.broadcast_to(sink.astype(F32).reshape(hk, grp, 1, 1), logits.shape[:-1] + (1,))
        logits = jnp.concatenate([logits, s_sink], axis=-1)
    p = jax.nn.softmax(logits, axis=-1).astype(v.dtype)
    o = jnp.einsum('bhgqk,bkhd->bqhgd', p[..., :k.shape[1]], v)
    return o.reshape(bsz, n, hq * dh)


def segsum(a):
    n = a.shape[-1]
    cs = jnp.cumsum(a, axis=-1)
    diff = cs[..., :, None] - cs[..., None, :]
    mask = jnp.tril(jnp.ones((n, n), dtype=bool))
    return jnp.where(mask, diff, -jnp.inf)


def ssd_scan(xs, dt, a, bm, cm, h0):
    bsz, s, h, p = xs.shape
    n = bm.shape[-1]
    L = SSD_CHUNK
    nc = s // L
    xdt = (xs * dt[..., None]).reshape(bsz, nc, L, h, p)
    ad = (dt * a).reshape(bsz, nc, L, h).transpose(0, 3, 1, 2)
    bc = bm.reshape(bsz, nc, L, h, n)
    cc = cm.reshape(bsz, nc, L, h, n)
    a_cs = jnp.cumsum(ad, axis=-1)
    cb = jnp.einsum('bclhn,bcshn->bhcls', cc, bc) * jnp.exp(segsum(ad))
    y_diag = jnp.einsum('bhcls,bcshp->bclhp', cb, xdt)
    decay_states = jnp.exp(a_cs[..., -1:] - a_cs)
    states = jnp.einsum('bclhn,bhcl,bclhp->bchpn', bc, decay_states, xdt)
    if h0 is None:
        h0 = jnp.zeros_like(states[:, 0])
    states = jnp.concatenate([h0[:, None].astype(states.dtype), states], axis=1)
    decay_chunk = jnp.exp(segsum(jnp.pad(a_cs[..., -1], ((0, 0), (0, 0), (1, 0)))))
    new_states = jnp.einsum('bhzc,bchpn->bzhpn', decay_chunk, states)
    y_off = jnp.einsum('bclhn,bchpn,bhcl->bclhp', cc, new_states[:, :-1], jnp.exp(a_cs))
    return (y_diag + y_off).reshape(bsz, s, h, p), new_states[:, -1]


def ssd_bidir(xs, bm, cm, dt, a, h0f, h0b):
    yf, hf = ssd_scan(xs, dt[:, :, 0], a[0], bm, cm, h0f)
    yb, hb = ssd_scan(flip(xs), flip(dt[:, :, 1]), a[1], flip(bm), flip(cm), h0b)
    return yf + flip(yb), hf, hb


def ssd_branch(px, pc, conv_w, conv_b, a_log, dt_bias, d_skip, norm_g, need_ctx):
    a = -jnp.exp(a_log.astype(F32))

    def prep(xbc, dt_raw):
        bsz, s, _ = xbc.shape
        xbc = jax.nn.silu(dwconv_centered(xbc, conv_w, conv_b))
        xs, bm, cm = jnp.split(xbc, [SSD_D_INNER, SSD_D_INNER + SSD_GROUPS * SSD_STATE], axis=-1)
        rep = SSD_HEADS // SSD_GROUPS
        xs = xs.reshape(bsz, s, SSD_HEADS, SSD_HEADDIM)
        bm = jnp.repeat(bm.reshape(bsz, s, SSD_GROUPS, SSD_STATE), rep, axis=2)
        cm = jnp.repeat(cm.reshape(bsz, s, SSD_GROUPS, SSD_STATE), rep, axis=2)
        dt = jax.nn.softplus(dt_raw.reshape(bsz, s, 2, SSD_HEADS).astype(F32) + dt_bias.astype(F32))
        return xs, bm, cm, dt

    def finish(y, xs, z):
        bsz, s, _ = z.shape
        y = (y + d_skip.astype(F32)[:, None] * xs.astype(F32)).reshape(bsz, s, SSD_D_INNER).astype(z.dtype)
        return rms_norm(y * jax.nn.silu(z), norm_g)

    z, xbc, dt_raw = px
    zc, xbcc, dtc_raw = pc
    cx, cbm, ccm, cdt = prep(xbcc, dtc_raw)
    yc, hcf, hcb = ssd_bidir(cx, cbm, ccm, cdt, a, None, None)
    xs, bm, cm, dt = prep(xbc, dt_raw)
    y, _, _ = ssd_bidir(xs, bm, cm, dt, a, hcf, hcb)
    out_c = finish(yc, cx, zc) if need_ctx else None
    return finish(y, xs, z), out_c


def na_attend(q, k, v, kc, vc, rpb):
    bsz, s, h, dh = q.shape
    rows = s // GRID_W
    wr = min(NA_WIN_R, rows)
    n_nb = wr * NA_WIN_C
    scale = dh ** -0.5
    qg = q.reshape(bsz, rows, GRID_W, h, dh)
    kg = k.reshape(bsz, rows, GRID_W, h, dh)
    vg = v.reshape(bsz, rows, GRID_W, h, dh)
    cols = jnp.arange(GRID_W)
    c_start = jnp.clip(cols - NA_WIN_C // 2, 0, GRID_W - NA_WIN_C)
    col_idx = c_start[:, None] + jnp.arange(NA_WIN_C)[None, :]
    col_rel = col_idx - cols[:, None] + (NA_WIN_C - 1)

    def row_block(r):
        r0 = jnp.clip(r - wr // 2, 0, rows - wr)
        q_r = lax.dynamic_index_in_dim(qg, r, axis=1, keepdims=False)
        k_nb = lax.dynamic_slice_in_dim(kg, r0, wr, axis=1)[:, :, col_idx]
        v_nb = lax.dynamic_slice_in_dim(vg, r0, wr, axis=1)[:, :, col_idx]
        row_rel = r0 + jnp.arange(wr) - r + (NA_WIN_R - 1)
        bias = rpb[:, row_rel[:, None, None], col_rel[None, :, :]]
        bias = bias.transpose(0, 2, 1, 3).astype(F32)
        s_nb = jnp.einsum('bqhd,brqjhd->bhqrj', q_r, k_nb).astype(F32) * scale + bias
        s_cx = jnp.einsum('bqhd,bchd->bhqc', q_r, kc).astype(F32) * scale
        logits = jnp.concatenate([s_nb.reshape(bsz, h, GRID_W, n_nb), s_cx], axis=-1)
        p = jax.nn.softmax(logits, axis=-1).astype(v.dtype)
        p_nb = p[..., :n_nb].reshape(bsz, h, GRID_W, wr, NA_WIN_C)
        return (jnp.einsum('bhqrj,brqjhd->bqhd', p_nb, v_nb)
                + jnp.einsum('bhqc,bchd->bqhd', p[..., n_nb:], vc))

    o = lax.map(row_block, jnp.arange(rows))
    return o.transpose(1, 0, 2, 3, 4).reshape(bsz, s, h * dh)


def na_branch(pqkv, pqkvc, q_norm, k_norm, rpb, need_ctx):
    def heads(t):
        bsz, s, _ = t.shape
        t = t.reshape(bsz, s, 3, NA_HEADS, NA_HEAD_DIM)
        return rms_norm(t[:, :, 0], q_norm), rms_norm(t[:, :, 1], k_norm), t[:, :, 2]

    q, k, v = heads(pqkv)
    qc, kc, vc = heads(pqkvc)
    out = na_attend(q, k, v, kc, vc, rpb)
    out_c = ctx_attend(qc, kc, vc, None) if need_ctx else None
    return out, out_c


def gla_chunk_scan(q, k, v, g, s0):
    bsz, s, h, dk = q.shape
    dv = v.shape[-1]
    L = GLA_CHUNK
    nc = s // L
    q = q.reshape(bsz, nc, L, h, dk)
    k = k.reshape(bsz, nc, L, h, dk)
    v = v.reshape(bsz, nc, L, h, dv)
    gc = jnp.cumsum(g.reshape(bsz, nc, L, h, dk), axis=2)
    q_in = q * jnp.exp(gc)
    k_in = k * jnp.exp(-gc)
    k_out = k * jnp.exp(gc[:, :, -1:] - gc)
    att = jnp.einsum('bclhk,bcshk->bchls', q_in, k_in)
    att = jnp.where(jnp.tril(jnp.ones((L, L), dtype=bool)), att, 0.0)
    o_intra = jnp.einsum('bchls,bcshv->bclhv', att, v)
    s_chunk = jnp.einsum('bclhk,bclhv->bchkv', k_out, v)
    decay = jnp.exp(gc[:, :, -1])
    if s0 is None:
        s0 = jnp.zeros_like(s_chunk[:, 0])

    def step(state, inp):
        dec, sc = inp
        return state * dec[..., None] + sc, state

    s_final, s_before = lax.scan(step, s0.astype(s_chunk.dtype),
                                 (decay.transpose(1, 0, 2, 3), s_chunk.transpose(1, 0, 2, 3, 4)))
    o_inter = jnp.einsum('bclhk,cbhkv->bclhv', q_in, s_before)
    return (o_intra + o_inter).reshape(bsz, s, h, dv), s_final


def gla_bidir(q, k, v, g, s0f, s0b):
    of, sf = gla_chunk_scan(q, k, v, g[:, :, 0], s0f)
    ob, sb = gla_chunk_scan(flip(q), flip(k), flip(v), flip(g[:, :, 1]), s0b)
    return of + flip(ob), sf, sb


def gla_branch(px, pc, w_gate, b_gate, norm_g, need_ctx):
    def prep(q, k, v, glr):
        bsz, s, _ = q.shape
        q = q.reshape(bsz, s, GLA_HEADS, GLA_HEAD_K) * (GLA_HEAD_K ** -0.5)
        k = k.reshape(bsz, s, GLA_HEADS, GLA_HEAD_K)
        v = v.reshape(bsz, s, GLA_HEADS, GLA_HEAD_V)
        logit = jnp.einsum('bsdr,dre->bsde', glr.reshape(bsz, s, 2, GLA_GATE_RANK), w_gate) + b_gate
        g = (jax.nn.log_sigmoid(logit.astype(F32)) / GLA_TAU).reshape(bsz, s, 2, GLA_HEADS, GLA_HEAD_K)
        return q, k, v, g

    def finish(o, r):
        bsz, s, _ = r.shape
        o = rms_norm(o.astype(r.dtype), norm_g).reshape(bsz, s, GLA_V_DIM)
        return o * jax.nn.silu(r)

    q, k, v, r, glr = px
    qc, kc, vc, rc, glrc = pc
    cq, ck, cv, cg = prep(qc, kc, vc, glrc)
    oc, scf, scb = gla_bidir(cq, ck, cv, cg, None, None)
    lq, lk, lv, lg = prep(q, k, v, glr)
    o, _, _ = gla_bidir(lq, lk, lv, lg, scf, scb)
    out_c = finish(oc, rc) if need_ctx else None
    return finish(o, r), out_c


def swa_attend(q, k, v, kc, vc, sink):
    bsz, s, hq, dh = q.shape
    hk = k.shape[2]
    grp = hq // hk
    blk = SWA_BLOCK
    nb = s // blk
    scale = dh ** -0.5
    qb = q.reshape(bsz, nb, blk, hk, grp, dh)

    def band(t):
        tp = jnp.pad(t, ((0, 0), (blk, blk), (0, 0), (0, 0))).reshape(bsz, nb + 2, blk, hk, dh)
        return jnp.concatenate([tp[:, :-2], tp[:, 1:-1], tp[:, 2:]], axis=2)

    kb = band(k)
    vb = band(v)
    qpos = jnp.arange(s).reshape(nb, blk)
    kpos = jnp.arange(-blk, s + blk).reshape(nb + 2, blk)
    kpos = jnp.concatenate([kpos[:-2], kpos[1:-1], kpos[2:]], axis=1)
    valid = ((jnp.abs(qpos[:, :, None] - kpos[:, None, :]) <= SWA_WINDOW)
             & (kpos >= 0)[:, None, :] & (kpos < s)[:, None, :])
    s_loc = jnp.einsum('bnqhgd,bnkhd->bhgnqk', qb, kb).astype(F32) * scale
    s_loc = jnp.where(valid, s_loc, -jnp.inf)
    s_ctx = jnp.einsum('bnqhgd,bchd->bhgnqc', qb, kc).astype(F32) * scale
    s_sink = jnp.broadcast_to(sink.astype(F32).reshape(hk, grp, 1, 1, 1), s_loc.shape[:-1] + (1,))
    p = jax.nn.softmax(jnp.concatenate([s_loc, s_ctx, s_sink], axis=-1), axis=-1).astype(v.dtype)
    nk = 3 * blk
    nc = kc.shape[1]
    o = (jnp.einsum('bhgnqk,bnkhd->bnqhgd', p[..., :nk], vb)
         + jnp.einsum('bhgnqc,bchd->bnqhgd', p[..., nk:nk + nc], vc))
    return o.reshape(bsz, s, hq * dh)


def swa_branch(px, pc, q_norm, k_norm, sink, rope, need_ctx):
    def heads(q, k, v):
        bsz, s, _ = q.shape
        q = rms_norm(q.reshape(bsz, s, SWA_HEADS, SWA_HEAD_DIM), q_norm)
        k = rms_norm(k.reshape(bsz, s, SWA_KV_HEADS, SWA_HEAD_DIM), k_norm)
        return q, k, v.reshape(bsz, s, SWA_KV_HEADS, SWA_HEAD_DIM)

    q, k, v = heads(*px)
    qc, kc, vc = heads(*pc)
    q = apply_axial_rope(q, rope)
    k = apply_axial_rope(k, rope)
    out = swa_attend(q, k, v, kc, vc, sink)
    out_c = ctx_attend(qc, kc, vc, sink) if need_ctx else None
    return out, out_c


def merge_branches(gate_logits, outs, w_branch, w_out):
    g = jax.nn.sigmoid(gate_logits.reshape(gate_logits.shape[:-1] + (N_BRANCH, D_MODEL)))
    m = g[..., 0, :] * (outs[0] @ w_branch[0])
    for i in range(1, N_BRANCH):
        m = m + g[..., i, :] * (outs[i] @ w_branch[i])
    return m @ w_out


def mixer_sublayer(hx, hc, w_in, ssd_conv_w, ssd_conv_b, ssd_a_log, ssd_dt_bias, ssd_d, ssd_norm_g,
                   na_q_norm, na_k_norm, na_rpb, gla_w_gate, gla_b_gate, gla_norm_g,
                   swa_q_norm, swa_k_norm, swa_sink, w_branch, w_out, rope, need_ctx):
    px = split_cols(hx @ w_in)
    pc = split_cols(hc @ w_in)
    a_x, a_c = ssd_branch(px[0:3], pc[0:3], ssd_conv_w, ssd_conv_b, ssd_a_log, ssd_dt_bias, ssd_d,
                          ssd_norm_g, need_ctx)
    b_x, b_c = na_branch(px[3], pc[3], na_q_norm, na_k_norm, na_rpb, need_ctx)
    c_x, c_c = gla_branch(px[4:9], pc[4:9], gla_w_gate, gla_b_gate, gla_norm_g, need_ctx)
    d_x, d_c = swa_branch(px[9:12], pc[9:12], swa_q_norm, swa_k_norm, swa_sink, rope, need_ctx)
    out_x = merge_branches(px[12], (a_x, b_x, c_x, d_x), w_branch, w_out)
    out_c = merge_branches(pc[12], (a_c, b_c, c_c, d_c), w_branch, w_out) if need_ctx else None
    return out_x, out_c


def peer_ffn(h, wq, k1, k2, u_tab, v_tab):
    bsz, n, dm = h.shape
    t = h.reshape(bsz * n, dm)
    nt = t.shape[0]
    q = (t @ wq).reshape(nt, PEER_HEADS, 2, PEER_QDIM // 2)
    s1 = jnp.einsum('thd,kd->thk', q[:, :, 0], k1).astype(F32)
    s2 = jnp.einsum('thd,kd->thk', q[:, :, 1], k2).astype(F32)
    v1, i1 = lax.top_k(s1, PEER_TOPK)
    v2, i2 = lax.top_k(s2, PEER_TOPK)
    n_cand = PEER_TOPK * PEER_TOPK
    cand = (v1[..., :, None] + v2[..., None, :]).reshape(nt, PEER_HEADS, n_cand)
    cand_idx = (i1[..., :, None] * PEER_NKEYS + i2[..., None, :]).reshape(nt, PEER_HEADS, n_cand)
    top_s, pos = lax.top_k(cand, PEER_TOPK)
    idx = jnp.take_along_axis(cand_idx, pos, axis=-1)
    gate = jax.nn.softmax(top_s, axis=-1).astype(h.dtype)
    nsel = PEER_HEADS * PEER_TOPK
    nblk = nt // PEER_BLOCK

    def block(args):
        tb, ib, gb = args
        act = jax.nn.gelu(jnp.einsum('td,tkd->tk', tb, u_tab[ib]), approximate=False) * gb
        return jnp.einsum('tk,tkd->td', act, v_tab[ib])

    out = lax.map(block, (t.reshape(nblk, PEER_BLOCK, dm), idx.reshape(nblk, PEER_BLOCK, nsel),
                          gate.reshape(nblk, PEER_BLOCK, nsel)))
    return out.reshape(bsz, n, dm)


def setup_inputs(seed: int = 0) -> dict:
    key = jax.random.key(seed)
    ks = jax.random.split(key, 40)
    L = DEPTH
    D = D_MODEL

    def nrm(k, shape, std):
        return jax.random.normal(k, shape, F32) * std

    dt0 = jnp.exp(jax.random.uniform(ks[12], (L, 2, SSD_HEADS), F32, math.log(1e-3), math.log(1e-1)))
    return {
        'x': nrm(ks[0], (BATCH, SEQ, D), 1.0),
        'c': nrm(ks[1], (BATCH, D), 1.0),
        'ctx': nrm(ks[2], (BATCH, CTX_LEN, D), 1.0),
        'c_ctx': nrm(ks[3], (D,), 1.0),
        'w_ada': nrm(ks[4], (L, D, 6 * D), 0.5 * D ** -0.5),
        'b_ada': nrm(ks[5], (L, 6 * D), 0.02),
        'g_norm1': 1.0 + nrm(ks[6], (L, D), 0.02),
        'g_norm2': 1.0 + nrm(ks[7], (L, D), 0.02),
        'w_in': nrm(ks[8], (L, D, IN_WIDTH), D ** -0.5),
        'ssd_conv_w': nrm(ks[9], (L, SSD_CONV_K, SSD_CONV_DIM), SSD_CONV_K ** -0.5),
        'ssd_conv_b': nrm(ks[10], (L, SSD_CONV_DIM), 0.02),
        'ssd_a_log': jnp.log(jax.random.uniform(ks[11], (L, 2, SSD_HEADS), F32, 1.0, 16.0)),
        'ssd_dt_bias': dt0 + jnp.log(-jnp.expm1(-dt0)),
        'ssd_d': 1.0 + nrm(ks[13], (L, SSD_HEADS), 0.1),
        'ssd_norm_g': 1.0 + nrm(ks[14], (L, SSD_D_INNER), 0.02),
        'na_q_norm': 1.0 + nrm(ks[15], (L, NA_HEAD_DIM), 0.02),
        'na_k_norm': 1.0 + nrm(ks[16], (L, NA_HEAD_DIM), 0.02),
        'na_rpb': nrm(ks[17], (L, NA_HEADS, 2 * NA_WIN_R - 1, 2 * NA_WIN_C - 1), 0.1),
        'gla_w_gate': nrm(ks[18], (L, 2, GLA_GATE_RANK, GLA_K_DIM), GLA_GATE_RANK ** -0.5),
        'gla_b_gate': nrm(ks[19], (L, 2, GLA_K_DIM), 0.1),
        'gla_norm_g': 1.0 + nrm(ks[20], (L, GLA_HEAD_V), 0.02),
        'swa_q_norm': 1.0 + nrm(ks[21], (L, SWA_HEAD_DIM), 0.02),
        'swa_k_norm': 1.0 + nrm(ks[22], (L, SWA_HEAD_DIM), 0.02),
        'swa_sink': nrm(ks[23], (L, SWA_HEADS), 0.5),
        'w_branch': nrm(ks[24], (L, N_BRANCH, BRANCH_WIDTH, D), BRANCH_WIDTH ** -0.5),
        'w_out': nrm(ks[25], (L, D, D), D ** -0.5),
        'peer_wq': nrm(ks[26], (L, D, PEER_HEADS * PEER_QDIM), D ** -0.5),
        'peer_k1': nrm(ks[27], (L, PEER_NKEYS, PEER_QDIM // 2), (PEER_QDIM // 2) ** -0.5),
        'peer_k2': nrm(ks[28], (L, PEER_NKEYS, PEER_QDIM // 2), (PEER_QDIM // 2) ** -0.5),
        'peer_u': nrm(ks[29], (L, PEER_EXPERTS, D), D ** -0.5),
        'peer_v': nrm(ks[30], (L, PEER_EXPERTS, D), 0.5),
    }


def reference(x, c, ctx, c_ctx, w_ada, b_ada, g_norm1, g_norm2, w_in, ssd_conv_w, ssd_conv_b,
              ssd_a_log, ssd_dt_bias, ssd_d, ssd_norm_g, na_q_norm, na_k_norm, na_rpb,
              gla_w_gate, gla_b_gate, gla_norm_g, swa_q_norm, swa_k_norm, swa_sink,
              w_branch, w_out, peer_wq, peer_k1, peer_k2, peer_u, peer_v):
    rope = axial_rope_tables(x.shape[1])
    for l in range(DEPTH):
        need_ctx = l < DEPTH - 1
        mod_x = (jax.nn.silu(c) @ w_ada[l] + b_ada[l])[:, None, :]
        mod_c = (jax.nn.silu(c_ctx) @ w_ada[l] + b_ada[l])[None, None, :]
        sh1, sc1, gt1, sh2, sc2, gt2 = jnp.split(mod_x, 6, axis=-1)
        csh1, csc1, cgt1, csh2, csc2, cgt2 = jnp.split(mod_c, 6, axis=-1)
        hx = modulate(rms_norm(x, g_norm1[l]), sh1, sc1)
        hc = modulate(rms_norm(ctx, g_norm1[l]), csh1, csc1)
        mx, mc = mixer_sublayer(hx, hc, w_in[l], ssd_conv_w[l], ssd_conv_b[l], ssd_a_log[l],
                                ssd_dt_bias[l], ssd_d[l], ssd_norm_g[l], na_q_norm[l], na_k_norm[l],
                                na_rpb[l], gla_w_gate[l], gla_b_gate[l], gla_norm_g[l],
                                swa_q_norm[l], swa_k_norm[l], swa_sink[l], w_branch[l], w_out[l],
                                rope, need_ctx)
        x = x + gt1 * mx
        hx2 = modulate(rms_norm(x, g_norm2[l]), sh2, sc2)
        x = x + gt2 * peer_ffn(hx2, peer_wq[l], peer_k1[l], peer_k2[l], peer_u[l], peer_v[l])
        if need_ctx:
            ctx = ctx + cgt1 * mc
            hc2 = modulate(rms_norm(ctx, g_norm2[l]), csh2, csc2)
            ctx = ctx + cgt2 * peer_ffn(hc2, peer_wq[l], peer_k1[l], peer_k2[l], peer_u[l], peer_v[l])
    return x
```

```python
import functools
import math

import numpy as np
import jax
import jax.numpy as jnp
from jax import lax
from jax.experimental import pallas as pl
from jax.experimental.pallas import tpu as pltpu

D_MODEL = 2048
DEPTH = 2
GRID_W = 64
NORM_EPS = 1e-6
N_BRANCH = 4
BRANCH_WIDTH = D_MODEL // 2

SSD_HEADDIM = 64
SSD_D_INNER = BRANCH_WIDTH
SSD_HEADS = SSD_D_INNER // SSD_HEADDIM
SSD_GROUPS = 4
SSD_STATE = 128
SSD_CONV_DIM = SSD_D_INNER + 2 * SSD_GROUPS * SSD_STATE
SSD_CONV_K = 5
SSD_CHUNK = 128

NA_HEAD_DIM = 64
NA_HEADS = BRANCH_WIDTH // NA_HEAD_DIM
NA_WIN_R = 8
NA_WIN_C = 16

GLA_HEADS = 4
GLA_V_DIM = BRANCH_WIDTH
GLA_K_DIM = BRANCH_WIDTH // 2
GLA_HEAD_K = GLA_K_DIM // GLA_HEADS
GLA_HEAD_V = GLA_V_DIM // GLA_HEADS
GLA_GATE_RANK = 16
GLA_TAU = 16.0
GLA_CHUNK = 64

SWA_HEAD_DIM = 64
SWA_HEADS = BRANCH_WIDTH // SWA_HEAD_DIM
SWA_KV_HEADS = 4
SWA_GROUP = SWA_HEADS // SWA_KV_HEADS
SWA_WINDOW = 128
SWA_BLOCK = 128
ROPE_BASE = 10000.0
ROPE_AXIS_DIM = SWA_HEAD_DIM // 2

PEER_HEADS = 8
PEER_NKEYS = 128
PEER_QDIM = 256
PEER_TOPK = 16
PEER_NSEL = PEER_HEADS * PEER_TOPK

IN_SPLITS = (SSD_D_INNER, SSD_CONV_DIM, 2 * SSD_HEADS,
             3 * BRANCH_WIDTH,
             GLA_K_DIM, GLA_K_DIM, GLA_V_DIM, GLA_V_DIM, 2 * GLA_GATE_RANK,
             SWA_HEADS * SWA_HEAD_DIM, SWA_KV_HEADS * SWA_HEAD_DIM, SWA_KV_HEADS * SWA_HEAD_DIM,
             N_BRANCH * D_MODEL)
IN_OFFS = tuple(int(v) for v in np.cumsum((0,) + IN_SPLITS))

V7X_LANES = 128
V7X_VMEM_BYTES = 64 * 1024 * 1024
VMEM_LIMIT = 48 * 1024 * 1024

F32 = jnp.float32
BF16 = jnp.bfloat16
HI = lax.Precision.HIGHEST
NEG = -1e30


def _cparams(sem):
    return pltpu.CompilerParams(dimension_semantics=sem, vmem_limit_bytes=VMEM_LIMIT)


def _dot(a, b, precision=None):
    return jnp.dot(a, b, preferred_element_type=F32, precision=precision)


def _dot_nt(a, b, precision=None):
    return lax.dot_general(a, b, (((1,), (1,)), ((), ())), preferred_element_type=F32,
                           precision=precision)


def _silu(x):
    return x / (1.0 + jnp.exp(-x))


def _softplus(x):
    return jnp.maximum(x, 0.0) + jnp.log1p(jnp.exp(-jnp.abs(x)))


def _log_sigmoid(x):
    return jnp.minimum(x, 0.0) - jnp.log1p(jnp.exp(-jnp.abs(x)))


def _mm_kernel(a_ref, b_ref, o_ref):
    o_ref[...] = _dot(a_ref[...], b_ref[...]).astype(o_ref.dtype)


def matmul(a, b, out_dtype, tm=1024, tn=512):
    m, k = a.shape
    n = b.shape[1]
    tm = min(tm, m)
    tn = min(tn, n)
    assert m % tm == 0 and n % tn == 0, (m, n, tm, tn)
    return pl.pallas_call(
        _mm_kernel,
        grid=(m // tm, n // tn),
        in_specs=[pl.BlockSpec((tm, k), lambda i, j: (i, 0)),
                  pl.BlockSpec((k, tn), lambda i, j: (0, j))],
        out_specs=pl.BlockSpec((tm, tn), lambda i, j: (i, j)),
        out_shape=jax.ShapeDtypeStruct((m, n), out_dtype),
        compiler_params=_cparams(("parallel", "parallel")),
        name="matmul",
    )(a, b)


def _mod_kernel(c_ref, w_ref, b_ref, o_ref):
    a = _silu(c_ref[...]).astype(BF16)
    o_ref[...] = _dot(a, w_ref[...].astype(BF16)) + b_ref[...]


def ada_mod(cc, w, b, tn=1024):
    m, k = cc.shape
    n = w.shape[1]
    return pl.pallas_call(
        _mod_kernel,
        grid=(n // tn,),
        in_specs=[pl.BlockSpec((m, k), lambda j: (0, 0)),
                  pl.BlockSpec((k, tn), lambda j: (0, j)),
                  pl.BlockSpec((1, tn), lambda j: (0, j))],
        out_specs=pl.BlockSpec((m, tn), lambda j: (0, j)),
        out_shape=jax.ShapeDtypeStruct((m, n), F32),
        compiler_params=_cparams(("parallel",)),
        name="ada_mod",
    )(cc, w, b.reshape(1, n))


def _normmod_kernel(x_ref, g_ref, sh_ref, sc_ref, *o_refs):
    x = x_ref[0]
    var = jnp.mean(x * x, axis=-1, keepdims=True)
    y = x * lax.rsqrt(var + NORM_EPS) * g_ref[...]
    y = y * (1.0 + sc_ref[0]) + sh_ref[0]
    for o_ref in o_refs:
        o_ref[0] = y.astype(o_ref.dtype)


def norm_modulate(x, g, shift, scale, out_dtypes, ts=512):
    bsz, s, d = x.shape
    ts = min(ts, s)
    per_batch = shift.shape[0] == bsz and bsz > 1
    mod_map = (lambda b, i: (b, 0, 0)) if per_batch else (lambda b, i: (0, 0, 0))
    outs = pl.pallas_call(
        _normmod_kernel,
        grid=(bsz, s // ts),
        in_specs=[pl.BlockSpec((1, ts, d), lambda b, i: (b, i, 0)),
                  pl.BlockSpec((1, d), lambda b, i: (0, 0)),
                  pl.BlockSpec((1, 1, d), mod_map),
                  pl.BlockSpec((1, 1, d), mod_map)],
        out_specs=[pl.BlockSpec((1, ts, d), lambda b, i: (b, i, 0)) for _ in out_dtypes],
        out_shape=[jax.ShapeDtypeStruct((bsz, s, d), dt) for dt in out_dtypes],
        compiler_params=_cparams(("parallel", "parallel")),
        name="norm_modulate",
    )(x, g.reshape(1, d), shift, scale)
    return outs


def _mm_resid_kernel(a_ref, w_ref, x_ref, gt_ref, o_ref):
    y = _dot(a_ref[...], w_ref[...])
    o_ref[0] = x_ref[0] + gt_ref[0] * y


def matmul_gated_residual(a, w, x, gate, tm=1024, tn=512):
    bsz, s, n = x.shape
    k = a.shape[1]
    tm = min(tm, s)
    nt = s // tm
    per_batch = gate.shape[0] == bsz and bsz > 1
    g_map = (lambda b, i, j: (b, 0, j)) if per_batch else (lambda b, i, j: (0, 0, j))
    return pl.pallas_call(
        _mm_resid_kernel,
        grid=(bsz, nt, n // tn),
        in_specs=[pl.BlockSpec((tm, k), lambda b, i, j: (b * nt + i, 0)),
                  pl.BlockSpec((k, tn), lambda b, i, j: (0, j)),
                  pl.BlockSpec((1, tm, tn), lambda b, i, j: (b, i, j)),
                  pl.BlockSpec((1, 1, tn), g_map)],
        out_specs=pl.BlockSpec((1, tm, tn), lambda b, i, j: (b, i, j)),
        out_shape=jax.ShapeDtypeStruct((bsz, s, n), F32),
        compiler_params=_cparams(("parallel", "parallel", "parallel")),
        name="out_proj_residual",
    )(a, w, x, gate)


def _merge_kernel(a_ref, b_ref, c_ref, d_ref, g0_ref, g1_ref, g2_ref, g3_ref, w_ref, o_ref):
    acc = None
    for i, (br, gr) in enumerate(((a_ref, g0_ref), (b_ref, g1_ref), (c_ref, g2_ref), (d_ref, g3_ref))):
        y = _dot(br[...], w_ref[i])
        gate = 1.0 / (1.0 + jnp.exp(-gr[...].astype(F32)))
        acc = gate * y if acc is None else acc + gate * y
    o_ref[...] = acc.astype(o_ref.dtype)


def merge_branches(outs, gate_logits, w_branch, tm=512, tn=512):
    t, kb = outs[0].shape
    d = w_branch.shape[-1]
    tm = min(tm, t)
    nj = d // tn
    in_specs = [pl.BlockSpec((tm, kb), lambda i, j: (i, 0)) for _ in range(N_BRANCH)]
    in_specs += [pl.BlockSpec((tm, tn), functools.partial(lambda i, j, br: (i, br * nj + j), br=br))
                 for br in range(N_BRANCH)]
    in_specs += [pl.BlockSpec((N_BRANCH, kb, tn), lambda i, j: (0, 0, j))]
    return pl.pallas_call(
        _merge_kernel,
        grid=(t // tm, nj),
        in_specs=in_specs,
        out_specs=pl.BlockSpec((tm, tn), lambda i, j: (i, j)),
        out_shape=jax.ShapeDtypeStruct((t, d), BF16),
        compiler_params=_cparams(("parallel", "parallel")),
        name="merge_branches",
    )(*outs, gate_logits, gate_logits, gate_logits, gate_logits, w_branch)


def _headprep_kernel(x_ref, g_ref, cos_ref, sin_ref, rot_ref, o_ref, *, rope, scale):
    x = x_ref[0].astype(F32)
    var = jnp.mean(x * x, axis=-1, keepdims=True)
    y = x * lax.rsqrt(var + NORM_EPS) * g_ref[...]
    if rope:
        y = y * cos_ref[...] + _dot(y, rot_ref[...], HI) * sin_ref[...]
    o_ref[0] = (y * scale).astype(o_ref.dtype)


def head_prep(x, g, rope_tabs, scale, ts=1024):
    n, s, dh = x.shape
    ts = min(ts, s)
    rope = rope_tabs is not None
    if rope:
        cos, sin, rot = rope_tabs
    else:
        cos = sin = jnp.zeros((s, dh), F32)
        rot = jnp.zeros((dh, dh), F32)
    return pl.pallas_call(
        functools.partial(_headprep_kernel, rope=rope, scale=scale),
        grid=(n, s // ts),
        in_specs=[pl.BlockSpec((1, ts, dh), lambda h, i: (h, i, 0)),
                  pl.BlockSpec((1, dh), lambda h, i: (0, 0)),
                  pl.BlockSpec((ts, dh), lambda h, i: (i, 0)),
                  pl.BlockSpec((ts, dh), lambda h, i: (i, 0)),
                  pl.BlockSpec((dh, dh), lambda h, i: (0, 0))],
        out_specs=pl.BlockSpec((1, ts, dh), lambda h, i: (h, i, 0)),
        out_shape=jax.ShapeDtypeStruct((n, s, dh), BF16),
        compiler_params=_cparams(("parallel", "parallel")),
        name="head_prep",
    )(x, g.reshape(1, dh), cos, sin, rot)


def rope_tables(s):
    t = np.arange(s)
    row = (t // GRID_W).astype(np.float32)
    col = (t % GRID_W).astype(np.float32)
    nf = ROPE_AXIS_DIM // 2
    inv = jnp.asarray(ROPE_BASE, F32) ** (-jnp.arange(nf, dtype=F32) / nf)
    ar = jnp.asarray(row)[:, None] * inv
    ac = jnp.asarray(col)[:, None] * inv
    cos = jnp.concatenate([jnp.cos(ar), jnp.cos(ar), jnp.cos(ac), jnp.cos(ac)], axis=-1)
    sin = jnp.concatenate([jnp.sin(ar), jnp.sin(ar), jnp.sin(ac), jnp.sin(ac)], axis=-1)
    rot = np.zeros((SWA_HEAD_DIM, SWA_HEAD_DIM), np.float32)
    for d in range(SWA_HEAD_DIM):
        if d % ROPE_AXIS_DIM < nf:
            rot[d + nf, d] = -1.0
        else:
            rot[d - nf, d] = 1.0
    return cos, sin, jnp.asarray(rot)


NA_ROWS_PER_STEP = 8


def _na_kernel(q_ref, k_ref, v_ref, kc_ref, vc_ref, bias_ref, o_ref, *, n_rows):
    i = pl.program_id(2)
    kc = kc_ref[0, 0]
    vc = vc_ref[0, 0]
    n_nb = NA_WIN_R * GRID_W

    def body(rr, carry):
        r = i * NA_ROWS_PER_STEP + rr
        r0 = jnp.clip(r - NA_WIN_R // 2, 0, n_rows - NA_WIN_R)
        qoff = pl.multiple_of(rr * GRID_W, GRID_W)
        koff = pl.multiple_of(r0 * GRID_W, GRID_W)
        q = q_ref[0, 0, pl.ds(qoff, GRID_W), :]
        kn = k_ref[0, 0, pl.ds(koff, n_nb), :]
        vn = v_ref[0, 0, pl.ds(koff, n_nb), :]
        s_nb = _dot_nt(q, kn) + bias_ref[r - r0, 0]
        s_cx = _dot_nt(q, kc)
        m = jnp.maximum(jnp.max(s_nb, axis=-1, keepdims=True), jnp.max(s_cx, axis=-1, keepdims=True))
        p_nb = jnp.exp(s_nb - m)
        p_cx = jnp.exp(s_cx - m)
        den = jnp.sum(p_nb, axis=-1, keepdims=True) + jnp.sum(p_cx, axis=-1, keepdims=True)
        o = _dot(p_nb.astype(BF16), vn) + _dot(p_cx.astype(BF16), vc)
        o_ref[0, 0, pl.ds(qoff, GRID_W), :] = (o / den).astype(o_ref.dtype)
        return carry

    lax.fori_loop(0, NA_ROWS_PER_STEP, body, 0)


def na_bias_table(rpb):
    cols = np.arange(GRID_W)
    c_start = np.clip(cols - NA_WIN_C // 2, 0, GRID_W - NA_WIN_C)
    valid = (cols[None, :] >= c_start[:, None]) & (cols[None, :] < c_start[:, None] + NA_WIN_C)
    cidx = np.clip(cols[None, :] - cols[:, None] + NA_WIN_C - 1, 0, 2 * NA_WIN_C - 2)
    ridx = np.arange(NA_WIN_R)[None, :] - np.arange(NA_WIN_R)[:, None] + NA_WIN_R - 1
    tab = rpb.astype(F32)[:, ridx[:, :, None, None], cidx[None, None, :, :]]
    tab = jnp.where(jnp.asarray(valid)[None, None, None], tab, NEG)
    tab = tab.transpose(1, 0, 3, 2, 4)
    return tab.reshape(NA_WIN_R, rpb.shape[0], GRID_W, NA_WIN_R * GRID_W)


def na_attention(q, k, v, kc, vc, bias):
    bsz, h, s, dh = q.shape
    lc = kc.shape[2]
    n_rows = s // GRID_W
    tq = NA_ROWS_PER_STEP * GRID_W
    n_nb = NA_WIN_R * GRID_W
    return pl.pallas_call(
        functools.partial(_na_kernel, n_rows=n_rows),
        grid=(bsz, h, s // tq),
        in_specs=[pl.BlockSpec((1, 1, tq, dh), lambda b, hh, i: (b, hh, i, 0)),
                  pl.BlockSpec((1, 1, s, dh), lambda b, hh, i: (b, hh, 0, 0)),
                  pl.BlockSpec((1, 1, s, dh), lambda b, hh, i: (b, hh, 0, 0)),
                  pl.BlockSpec((1, 1, lc, dh), lambda b, hh, i: (b, hh, 0, 0)),
                  pl.BlockSpec((1, 1, lc, dh), lambda b, hh, i: (b, hh, 0, 0)),
                  pl.BlockSpec((NA_WIN_R, 1, GRID_W, n_nb), lambda b, hh, i: (0, hh, 0, 0))],
        out_specs=pl.BlockSpec((1, 1, tq, dh), lambda b, hh, i: (b, hh, i, 0)),
        out_shape=jax.ShapeDtypeStruct((bsz, h, s, dh), BF16),
        compiler_params=_cparams(("parallel", "parallel", "arbitrary")),
        name="na_attention",
    )(q, k, v, kc, vc, bias)


def _swa_kernel(sink_ref, q_ref, k_ref, v_ref, kc_ref, vc_ref, o_ref, *, seq):
    kh = pl.program_id(1)
    n = pl.program_id(2)
    span = 3 * SWA_BLOCK
    start = pl.multiple_of(jnp.clip((n - 1) * SWA_BLOCK, 0, seq - span), SWA_BLOCK)
    kw = k_ref[0, 0, pl.ds(start, span), :]
    vw = v_ref[0, 0, pl.ds(start, span), :]
    kc = kc_ref[0, 0]
    vc = vc_ref[0, 0]
    qpos = n * SWA_BLOCK + lax.broadcasted_iota(jnp.int32, (SWA_BLOCK, span), 0)
    kpos = start + lax.broadcasted_iota(jnp.int32, (SWA_BLOCK, span), 1)
    valid = jnp.abs(qpos - kpos) <= SWA_WINDOW
    for g in range(SWA_GROUP):
        q = q_ref[0, 0, g]
        s_loc = jnp.where(valid, _dot_nt(q, kw), NEG)
        s_ctx = _dot_nt(q, kc)
        sink = sink_ref[kh * SWA_GROUP + g]
        m = jnp.maximum(jnp.max(s_loc, axis=-1, keepdims=True), jnp.max(s_ctx, axis=-1, keepdims=True))
        m = jnp.maximum(m, sink)
        p_loc = jnp.exp(s_loc - m)
        p_ctx = jnp.exp(s_ctx - m)
        den = (jnp.sum(p_loc, axis=-1, keepdims=True) + jnp.sum(p_ctx, axis=-1, keepdims=True)
               + jnp.exp(sink - m))
        o = _dot(p_loc.astype(BF16), vw) + _dot(p_ctx.astype(BF16), vc)
        o_ref[0, 0, g] = (o / den).astype(o_ref.dtype)


def swa_attention(q, k, v, kc, vc, sink):
    bsz, hk, grp, s, dh = q.shape
    lc = kc.shape[2]
    assert s >= 3 * SWA_BLOCK
    return pl.pallas_call(
        functools.partial(_swa_kernel, seq=s),
        grid=(bsz, hk, s // SWA_BLOCK),
        in_specs=[pl.BlockSpec(memory_space=pltpu.SMEM),
                  pl.BlockSpec((1, 1, grp, SWA_BLOCK, dh), lambda b, h, n: (b, h, 0, n, 0)),
                  pl.BlockSpec((1, 1, s, dh), lambda b, h, n: (b, h, 0, 0)),
                  pl.BlockSpec((1, 1, s, dh), lambda b, h, n: (b, h, 0, 0)),
                  pl.BlockSpec((1, 1, lc, dh), lambda b, h, n: (b, h, 0, 0)),
                  pl.BlockSpec((1, 1, lc, dh), lambda b, h, n: (b, h, 0, 0))],
        out_specs=pl.BlockSpec((1, 1, grp, SWA_BLOCK, dh), lambda b, h, n: (b, h, 0, n, 0)),
        out_shape=jax.ShapeDtypeStruct((bsz, hk, grp, s, dh), BF16),
        compiler_params=_cparams(("parallel", "parallel", "arbitrary")),
        name="swa_attention",
    )(sink.astype(F32), q, k, v, kc, vc)


def _ctx_attn_kernel(sink_ref, q_ref, k_ref, v_ref, o_ref, *, grp):
    kh = pl.program_id(1)
    k = k_ref[0, 0]
    v = v_ref[0, 0]
    for g in range(grp):
        q = q_ref[0, 0, g]
        s = _dot_nt(q, k)
        sink = sink_ref[kh * grp + g]
        m = jnp.maximum(jnp.max(s, axis=-1, keepdims=True), sink)
        p = jnp.exp(s - m)
        den = jnp.sum(p, axis=-1, keepdims=True) + jnp.exp(sink - m)
        o_ref[0, 0, g] = (_dot(p.astype(BF16), v) / den).astype(o_ref.dtype)


def ctx_attention(q, k, v, sink):
    bsz, hk, grp, n, dh = q.shape
    return pl.pallas_call(
        functools.partial(_ctx_attn_kernel, grp=grp),
        grid=(bsz, hk),
        in_specs=[pl.BlockSpec(memory_space=pltpu.SMEM),
                  pl.BlockSpec((1, 1, grp, n, dh), lambda b, h: (b, h, 0, 0, 0)),
                  pl.BlockSpec((1, 1, n, dh), lambda b, h: (b, h, 0, 0)),
                  pl.BlockSpec((1, 1, n, dh), lambda b, h: (b, h, 0, 0))],
        out_specs=pl.BlockSpec((1, 1, grp, n, dh), lambda b, h: (b, h, 0, 0, 0)),
        out_shape=jax.ShapeDtypeStruct((bsz, hk, grp, n, dh), BF16),
        compiler_params=_cparams(("parallel", "parallel")),
        name="ctx_attention",
    )(sink.astype(F32), q, k, v)


CONV_PAD = 8
CONV_ROWS = 256
CONV_COLS = 256


def _conv_kernel(x_ref, w_ref, b_ref, o_ref, *, seq):
    rows = min(CONV_ROWS, seq)

    def body(i, carry):
        base = pl.multiple_of(i * rows, rows)
        acc = jnp.zeros((rows, CONV_COLS), F32) + b_ref[...]
        halo = x_ref[0, pl.ds(base, rows + 2 * CONV_PAD), :]
        for k in range(SSD_CONV_K):
            off = CONV_PAD + k - SSD_CONV_K // 2
            acc = acc + w_ref[k:k + 1, :] * halo[off:off + rows, :]
        o_ref[0, pl.ds(base, rows), :] = _silu(acc).astype(o_ref.dtype)
        return carry

    lax.fori_loop(0, seq // rows, body, 0)


def conv_silu(xbc, w, b):
    bsz, s, c = xbc.shape
    xp = jnp.pad(xbc, ((0, 0), (CONV_PAD, CONV_PAD), (0, 0)))
    return pl.pallas_call(
        functools.partial(_conv_kernel, seq=s),
        grid=(bsz, c // CONV_COLS),
        in_specs=[pl.BlockSpec((1, s + 2 * CONV_PAD, CONV_COLS), lambda bb, j: (bb, 0, j)),
                  pl.BlockSpec((SSD_CONV_K, CONV_COLS), lambda bb, j: (0, j)),
                  pl.BlockSpec((1, CONV_COLS), lambda bb, j: (0, j))],
        out_specs=pl.BlockSpec((1, s, CONV_COLS), lambda bb, j: (bb, 0, j)),
        out_shape=jax.ShapeDtypeStruct((bsz, s, c), BF16),
        compiler_params=_cparams(("parallel", "parallel")),
        name="conv_silu",
    )(xp, w.astype(F32), b.reshape(1, c).astype(F32))


def _scan_masks(length, d):
    row = lax.broadcasted_iota(jnp.int32, (length, length), 0)
    col = lax.broadcasted_iota(jnp.int32, (length, length), 1)
    return (row - col) * (1 - 2 * d) >= 0


def _ssd_kernel(xs_ref, bm_ref, cm_ref, bmt_ref, dt_ref, dtt_ref, alog_ref, alogt_ref, dtb_ref, dtbt_ref,
                h0_ref, y_ref, hout_ref, state_ref, *, n_chunks):
    d = pl.program_id(0)
    c = pl.program_id(2)

    @pl.when(c == 0)
    def _():
        state_ref[...] = h0_ref[0, 0]

    length = SSD_CHUNK
    incl = _scan_masks(length, d)
    tri = incl.astype(F32)
    dt = _softplus(dt_ref[0, 0] + dtb_ref[0])
    dtt = _softplus(dtt_ref[0, 0] + dtbt_ref[0])
    ad = dt * (-jnp.exp(alog_ref[0]))
    adt = dtt * (-jnp.exp(alogt_ref[0]))
    acs = _dot(tri, ad, HI)
    acst = _dot_nt(adt, tri, HI)
    tot = jnp.sum(ad, axis=0, keepdims=True)
    tott = jnp.sum(adt, axis=1, keepdims=True)
    rep = SSD_HEADS // SSD_GROUPS
    for gi in range(SSD_GROUPS):
        gsl = slice(gi * SSD_STATE, (gi + 1) * SSD_STATE)
        cg = cm_ref[0, :, gsl]
        bg = bm_ref[0, :, gsl]
        bgt = bmt_ref[0, gsl, :].astype(F32)
        cb = _dot_nt(cg, bg)
        for hh in range(rep):
            h = gi * rep + hh
            psl = slice(h * SSD_HEADDIM, (h + 1) * SSD_HEADDIM)
            a_col = acs[:, h:h + 1]
            a_row = acst[h:h + 1, :]
            lmat = jnp.exp(jnp.where(incl, a_col - a_row, NEG))
            xdt = (xs_ref[0, :, psl].astype(F32) * dt[:, h:h + 1]).astype(BF16)
            st = state_ref[h]
            y = _dot((cb * lmat).astype(BF16), xdt)
            y = y + jnp.exp(a_col) * _dot(cg, st.astype(BF16))
            dec = jnp.exp(tott[h:h + 1, :] - a_row)
            state_ref[h] = jnp.exp(tot[:, h:h + 1]) * st + _dot((bgt * dec).astype(BF16), xdt)
            y_ref[0, 0, :, psl] = y.astype(y_ref.dtype)

    @pl.when(c == n_chunks - 1)
    def _():
        hout_ref[0, 0] = state_ref[...]


def _chunk_index(d, c, n_chunks):
    return c + d * (n_chunks - 1 - 2 * c)


def ssd_scan(xbc_act, dt_raw, a_log, dt_bias, h0):
    bsz, s, _ = xbc_act.shape
    nc = s // SSD_CHUNK
    hh = SSD_HEADS
    ng = SSD_GROUPS * SSD_STATE
    bmt = jnp.swapaxes(xbc_act[:, :, SSD_D_INNER:SSD_D_INNER + ng], 1, 2)
    dt2 = dt_raw.reshape(bsz, s, 2, hh).transpose(2, 0, 1, 3)
    dtt = dt2.transpose(0, 1, 3, 2)
    a_log = a_log.astype(F32)
    dt_bias = dt_bias.astype(F32)
    cmap = functools.partial(_chunk_index, n_chunks=nc)
    nb_x = SSD_D_INNER // ng
    return pl.pallas_call(
        functools.partial(_ssd_kernel, n_chunks=nc),
        grid=(2, bsz, nc),
        in_specs=[pl.BlockSpec((1, SSD_CHUNK, SSD_D_INNER), lambda d, b, c: (b, cmap(d, c), 0)),
                  pl.BlockSpec((1, SSD_CHUNK, ng), lambda d, b, c: (b, cmap(d, c), nb_x)),
                  pl.BlockSpec((1, SSD_CHUNK, ng), lambda d, b, c: (b, cmap(d, c), nb_x + 1)),
                  pl.BlockSpec((1, ng, SSD_CHUNK), lambda d, b, c: (b, 0, cmap(d, c))),
                  pl.BlockSpec((1, 1, SSD_CHUNK, hh), lambda d, b, c: (d, b, cmap(d, c), 0)),
                  pl.BlockSpec((1, 1, hh, SSD_CHUNK), lambda d, b, c: (d, b, 0, cmap(d, c))),
                  pl.BlockSpec((1, 1, hh), lambda d, b, c: (d, 0, 0)),
                  pl.BlockSpec((1, hh, 1), lambda d, b, c: (d, 0, 0)),
                  pl.BlockSpec((1, 1, hh), lambda d, b, c: (d, 0, 0)),
                  pl.BlockSpec((1, hh, 1), lambda d, b, c: (d, 0, 0)),
                  pl.BlockSpec((1, 1, hh, SSD_STATE, SSD_HEADDIM), lambda d, b, c: (d, b, 0, 0, 0))],
        out_specs=[pl.BlockSpec((1, 1, SSD_CHUNK, SSD_D_INNER), lambda d, b, c: (d, b, cmap(d, c), 0)),
                   pl.BlockSpec((1, 1, hh, SSD_STATE, SSD_HEADDIM), lambda d, b, c: (d, b, 0, 0, 0))],
        out_shape=[jax.ShapeDtypeStruct((2, bsz, s, SSD_D_INNER), BF16),
                   jax.ShapeDtypeStruct((2, bsz, hh, SSD_STATE, SSD_HEADDIM), F32)],
        scratch_shapes=[pltpu.VMEM((hh, SSD_STATE, SSD_HEADDIM), F32)],
        compiler_params=_cparams(("parallel", "parallel", "arbitrary")),
        name="ssd_scan",
    )(xbc_act, xbc_act, xbc_act, bmt, dt2, dtt,
      a_log.reshape(2, 1, hh), a_log.reshape(2, hh, 1), dt_bias.reshape(2, 1, hh), dt_bias.reshape(2, hh, 1), h0)


def _ssd_finish_kernel(y_ref, xs_ref, z_ref, d_ref, g_ref, o_ref):
    y = y_ref[0, 0].astype(F32) + y_ref[1, 0].astype(F32) + d_ref[...] * xs_ref[0].astype(F32)
    z = z_ref[0].astype(F32)
    u = y * _silu(z)
    var = jnp.mean(u * u, axis=-1, keepdims=True)
    o_ref[0] = (u * lax.rsqrt(var + NORM_EPS) * g_ref[...]).astype(o_ref.dtype)


def ssd_finish(y, xbc_act, z, d_skip, norm_g, ts=512):
    _, bsz, s, di = y.shape
    ts = min(ts, s)
    dvec = jnp.repeat(d_skip.astype(F32), SSD_HEADDIM).reshape(1, di)
    return pl.pallas_call(
        _ssd_finish_kernel,
        grid=(bsz, s // ts),
        in_specs=[pl.BlockSpec((2, 1, ts, di), lambda b, i: (0, b, i, 0)),
                  pl.BlockSpec((1, ts, di), lambda b, i: (b, i, 0)),
                  pl.BlockSpec((1, ts, di), lambda b, i: (b, i, 0)),
                  pl.BlockSpec((1, di), lambda b, i: (0, 0)),
                  pl.BlockSpec((1, di), lambda b, i: (0, 0))],
        out_specs=pl.BlockSpec((1, ts, di), lambda b, i: (b, i, 0)),
        out_shape=jax.ShapeDtypeStruct((bsz, s, di), BF16),
        compiler_params=_cparams(("parallel", "parallel")),
        name="ssd_finish",
    )(y, xbc_act, z, dvec, norm_g.reshape(1, di).astype(F32))


GLA_STEP = 128


def _gla_kernel(q_ref, k_ref, kt_ref, v_ref, glr_ref, glrt_ref, wg_ref, wgt_ref, bg_ref, bgt_ref,
                s0_ref, o_ref, sout_ref, state_ref, *, n_steps, reverse):
    c = pl.program_id(1)

    @pl.when(c == 0)
    def _():
        state_ref[...] = s0_ref[0]

    length = GLA_CHUNK
    incl = _scan_masks(length, int(reverse))
    tri = incl.astype(F32)
    subs = range(GLA_STEP // length)
    for sub in (reversed(subs) if reverse else subs):
        tsl = slice(sub * length, (sub + 1) * length)
        g = _log_sigmoid(_dot(glr_ref[0, tsl, :], wg_ref[...], HI) + bg_ref[...]) / GLA_TAU
        gt = _log_sigmoid(_dot(wgt_ref[...], glrt_ref[0, :, tsl], HI) + bgt_ref[...]) / GLA_TAU
        gc = _dot(tri, g, HI)
        gct = _dot_nt(gt, tri, HI)
        tott = jnp.sum(gt, axis=1, keepdims=True)
        q_in = (q_ref[0, tsl, :].astype(F32) * (GLA_HEAD_K ** -0.5) * jnp.exp(gc)).astype(BF16)
        k_in = (k_ref[0, tsl, :].astype(F32) * jnp.exp(-gc)).astype(BF16)
        k_out_t = (kt_ref[0, :, tsl].astype(F32) * jnp.exp(tott - gct)).astype(BF16)
        dec_t = jnp.exp(tott)
        for h in range(GLA_HEADS):
            ksl = slice(h * GLA_HEAD_K, (h + 1) * GLA_HEAD_K)
            vsl = slice(h * GLA_HEAD_V, (h + 1) * GLA_HEAD_V)
            qh = q_in[:, ksl]
            vh = v_ref[0, tsl, vsl]
            att = jnp.where(incl, _dot_nt(qh, k_in[:, ksl]), 0.0)
            st = state_ref[h]
            o = _dot(att.astype(BF16), vh) + _dot(qh, st.astype(BF16))
            state_ref[h] = st * dec_t[ksl, :] + _dot(k_out_t[ksl, :], vh)
            o_ref[0, tsl, vsl] = o.astype(o_ref.dtype)

    @pl.when(c == n_steps - 1)
    def _():
        sout_ref[0] = state_ref[...]


def gla_scan(q, k, v, glr, w_gate, b_gate, s0):
    bsz, s, kd = q.shape
    vd = v.shape[-1]
    ns = s // GLA_STEP
    r = GLA_GATE_RANK
    kt = jnp.swapaxes(k, 1, 2)
    w_gate = w_gate.astype(F32)
    b_gate = b_gate.astype(F32)
    outs, states = [], []
    for d in range(2):
        cmap = (lambda c: ns - 1 - c) if d else (lambda c: c)
        glr_d = glr[:, :, d * r:(d + 1) * r]
        o, st = pl.pallas_call(
            functools.partial(_gla_kernel, n_steps=ns, reverse=bool(d)),
            grid=(bsz, ns),
            in_specs=[pl.BlockSpec((1, GLA_STEP, kd), lambda b, c, cmap=cmap: (b, cmap(c), 0)),
                      pl.BlockSpec((1, GLA_STEP, kd), lambda b, c, cmap=cmap: (b, cmap(c), 0)),
                      pl.BlockSpec((1, kd, GLA_STEP), lambda b, c, cmap=cmap: (b, 0, cmap(c))),
                      pl.BlockSpec((1, GLA_STEP, vd), lambda b, c, cmap=cmap: (b, cmap(c), 0)),
                      pl.BlockSpec((1, GLA_STEP, r), lambda b, c, cmap=cmap: (b, cmap(c), 0)),
                      pl.BlockSpec((1, r, GLA_STEP), lambda b, c, cmap=cmap: (b, 0, cmap(c))),
                      pl.BlockSpec((r, kd), lambda b, c: (0, 0)),
                      pl.BlockSpec((kd, r), lambda b, c: (0, 0)),
                      pl.BlockSpec((1, kd), lambda b, c: (0, 0)),
                      pl.BlockSpec((kd, 1), lambda b, c: (0, 0)),
                      pl.BlockSpec((1, GLA_HEADS, GLA_HEAD_K, GLA_HEAD_V), lambda b, c: (b, 0, 0, 0))],
            out_specs=[pl.BlockSpec((1, GLA_STEP, vd), lambda b, c, cmap=cmap: (b, cmap(c), 0)),
                       pl.BlockSpec((1, GLA_HEADS, GLA_HEAD_K, GLA_HEAD_V), lambda b, c: (b, 0, 0, 0))],
            out_shape=[jax.ShapeDtypeStruct((bsz, s, vd), BF16),
                       jax.ShapeDtypeStruct((bsz, GLA_HEADS, GLA_HEAD_K, GLA_HEAD_V), F32)],
            scratch_shapes=[pltpu.VMEM((GLA_HEADS, GLA_HEAD_K, GLA_HEAD_V), F32)],
            compiler_params=_cparams(("parallel", "arbitrary")),
            name="gla_scan_bwd" if d else "gla_scan_fwd",
        )(q, k, kt, v, glr_d, jnp.swapaxes(glr_d, 1, 2), w_gate[d], w_gate[d].T,
          b_gate[d].reshape(1, kd), b_gate[d].reshape(kd, 1), s0[d])
        outs.append(o)
        states.append(st)
    return outs, states


def _gla_finish_kernel(of_ref, ob_ref, r_ref, g_ref, out_ref):
    o = of_ref[0].astype(F32) + ob_ref[0].astype(F32)
    r = r_ref[0].astype(F32)
    for h in range(GLA_HEADS):
        vsl = slice(h * GLA_HEAD_V, (h + 1) * GLA_HEAD_V)
        oh = o[:, vsl]
        var = jnp.mean(oh * oh, axis=-1, keepdims=True)
        y = oh * lax.rsqrt(var + NORM_EPS) * g_ref[...]
        out_ref[0, :, vsl] = (y * _silu(r[:, vsl])).astype(out_ref.dtype)


def gla_finish(o, r, norm_g, ts=512):
    bsz, s, vd = o[0].shape
    ts = min(ts, s)
    return pl.pallas_call(
        _gla_finish_kernel,
        grid=(bsz, s // ts),
        in_specs=[pl.BlockSpec((1, ts, vd), lambda b, i: (b, i, 0)),
                  pl.BlockSpec((1, ts, vd), lambda b, i: (b, i, 0)),
                  pl.BlockSpec((1, ts, vd), lambda b, i: (b, i, 0)),
                  pl.BlockSpec((1, GLA_HEAD_V), lambda b, i: (0, 0))],
        out_specs=pl.BlockSpec((1, ts, vd), lambda b, i: (b, i, 0)),
        out_shape=jax.ShapeDtypeStruct((bsz, s, vd), BF16),
        compiler_params=_cparams(("parallel", "parallel")),
        name="gla_finish",
    )(o[0], o[1], r, norm_g.reshape(1, GLA_HEAD_V).astype(F32))


PEER_SCORE_TOKENS = 128
PEER_TOKENS = 128
PEER_SLOTS = 3
PEER_ROWS = 8


def _extract_topk(s, rows, vals_ref, pos_ref):
    iota = lax.broadcasted_iota(jnp.int32, s.shape, 0).astype(F32)

    def body(j, cur):
        m = jnp.max(cur, axis=0, keepdims=True)
        pos = jnp.min(jnp.where(cur == m, iota, float(rows)), axis=0, keepdims=True)
        vals_ref[pl.ds(j, 1), :] = m
        pos_ref[pl.ds(j, 1), :] = pos
        return jnp.where(iota == pos, -jnp.inf, cur)

    lax.fori_loop(0, PEER_TOPK, body, s)


def _peer_score_kernel(q_ref, k1_ref, k2_ref, idx_ref, gate_ref, v1_ref, p1_ref, v2_ref, p2_ref, vt_ref, pt_ref):
    half = PEER_QDIM // 2
    q = q_ref[...]
    _extract_topk(_dot_nt(k1_ref[...], q[:, :half], HI), PEER_NKEYS, v1_ref, p1_ref)
    _extract_topk(_dot_nt(k2_ref[...], q[:, half:], HI), PEER_NKEYS, v2_ref, p2_ref)
    v1 = v1_ref[...]
    v2 = v2_ref[...]
    p1 = p1_ref[...]
    p2 = p2_ref[...]
    cand = jnp.concatenate([v1[a:a + 1, :] + v2 for a in range(PEER_TOPK)], axis=0)
    cidx = jnp.concatenate([p1[a:a + 1, :] * float(PEER_NKEYS) + p2 for a in range(PEER_TOPK)], axis=0)
    _extract_topk(cand, PEER_TOPK * PEER_TOPK, vt_ref, pt_ref)
    pos = pt_ref[...]
    iota = lax.broadcasted_iota(jnp.int32, cand.shape, 0).astype(F32)
    ids = [jnp.max(jnp.where(iota == pos[j:j + 1, :], cidx, -1.0), axis=0, keepdims=True)
           for j in range(PEER_TOPK)]
    idx_ref[...] = jnp.concatenate(ids, axis=0).astype(jnp.int32)
    top = vt_ref[...]
    p = jnp.exp(top - top[0:1, :])
    gate_ref[...] = p / jnp.sum(p, axis=0, keepdims=True)


def peer_retrieve(q, k1, k2):
    t = q.shape[0]
    tt = min(PEER_SCORE_TOKENS, t)
    kk = PEER_TOPK
    return pl.pallas_call(
        _peer_score_kernel,
        grid=(t // tt, PEER_HEADS),
        in_specs=[pl.BlockSpec((tt, PEER_QDIM), lambda i, h: (i, h)),
                  pl.BlockSpec((PEER_NKEYS, PEER_QDIM // 2), lambda i, h: (0, 0)),
                  pl.BlockSpec((PEER_NKEYS, PEER_QDIM // 2), lambda i, h: (0, 0))],
        out_specs=[pl.BlockSpec((kk, tt), lambda i, h: (h, i)),
                   pl.BlockSpec((kk, tt), lambda i, h: (h, i))],
        out_shape=[jax.ShapeDtypeStruct((PEER_NSEL, t), jnp.int32),
                   jax.ShapeDtypeStruct((PEER_NSEL, t), F32)],
        scratch_shapes=[pltpu.VMEM((kk, tt), F32) for _ in range(6)],
        compiler_params=_cparams(("parallel", "parallel")),
        name="peer_retrieve",
    )(q, k1.astype(F32), k2.astype(F32))


def _gelu(x):
    return 0.5 * x * (1.0 + lax.erf(x * (1.0 / math.sqrt(2.0))))


def _peer_expert_kernel(idx_ref, gate_ref, h_ref, x_ref, gt_ref, uv_hbm, o_ref, buf, sem, *, tokens):
    dm = D_MODEL
    n_groups = PEER_NSEL // PEER_ROWS
    depth = PEER_SLOTS - 1

    def row_copy(e, slot, k):
        return pltpu.make_async_copy(uv_hbm.at[pl.ds(e, 1), :], buf.at[slot, pl.ds(k, 1), :], sem.at[slot])

    def issue(tok, slot, ks):
        for k in ks:
            row_copy(idx_ref[k, tok], slot, k).start()

    def wait(slot):
        pltpu.make_async_copy(uv_hbm.at[pl.ds(0, PEER_NSEL), :], buf.at[slot], sem.at[slot]).wait()

    lane = lax.broadcasted_iota(jnp.int32, (PEER_NSEL, tokens), 1)
    gates = gate_ref[...]

    def compute(j, slot, prefetch):
        t = h_ref[pl.ds(j, 1), :]
        gcol = jnp.sum(jnp.where(lane == j, gates, 0.0), axis=1, keepdims=True)
        acc = jnp.zeros((PEER_ROWS, dm), F32)
        for g in range(n_groups):
            rows = pl.ds(g * PEER_ROWS, PEER_ROWS)
            if prefetch:
                issue(j + depth, (j + depth) % PEER_SLOTS, range(g * PEER_ROWS, (g + 1) * PEER_ROWS))
            u = buf[slot, rows, 0:dm]
            act = jnp.sum(u * t, axis=1, keepdims=True)
            a = _gelu(act) * gcol[g * PEER_ROWS:(g + 1) * PEER_ROWS, :]
            acc = acc + a * buf[slot, rows, dm:2 * dm]
        out = jnp.sum(acc, axis=0, keepdims=True)
        o_ref[0, pl.ds(j, 1), :] = x_ref[0, pl.ds(j, 1), :] + gt_ref[0] * out

    for j0 in range(depth):
        issue(j0, j0 % PEER_SLOTS, range(PEER_NSEL))

    def body(j, carry):
        slot = j % PEER_SLOTS
        wait(slot)
        compute(j, slot, True)
        return carry

    lax.fori_loop(0, tokens - depth, body, 0)
    for j in range(tokens - depth, tokens):
        wait(j % PEER_SLOTS)
        compute(j, j % PEER_SLOTS, False)


def peer_experts(idx, gate, h, x, gt, uv):
    bsz, s, dm = x.shape
    t = bsz * s
    tt = min(PEER_TOKENS, s)
    nt = s // tt
    per_batch = gt.shape[0] == bsz and bsz > 1
    g_map = (lambda b, i: (b, 0, 0)) if per_batch else (lambda b, i: (0, 0, 0))
    out = pl.pallas_call(
        functools.partial(_peer_expert_kernel, tokens=tt),
        grid=(bsz, nt),
        in_specs=[pl.BlockSpec((PEER_NSEL, tt), lambda b, i: (0, b * nt + i), memory_space=pltpu.SMEM),
                  pl.BlockSpec((PEER_NSEL, tt), lambda b, i: (0, b * nt + i)),
                  pl.BlockSpec((tt, dm), lambda b, i: (b * nt + i, 0)),
                  pl.BlockSpec((1, tt, dm), lambda b, i: (b, i, 0)),
                  pl.BlockSpec((1, 1, dm), g_map),
                  pl.BlockSpec(memory_space=pl.ANY)],
        out_specs=pl.BlockSpec((1, tt, dm), lambda b, i: (b, i, 0)),
        out_shape=jax.ShapeDtypeStruct((bsz, s, dm), F32),
        scratch_shapes=[pltpu.VMEM((PEER_SLOTS, PEER_NSEL, 2 * dm), F32),
                        pltpu.SemaphoreType.DMA((PEER_SLOTS,))],
        compiler_params=pltpu.CompilerParams(dimension_semantics=("arbitrary", "arbitrary"),
                                             vmem_limit_bytes=VMEM_LIMIT),
        name="peer_experts",
    )(idx, gate, h, x.reshape(bsz, s, dm), gt, uv)
    return out


def peer_sublayer(x, g_norm, shift, scale, gt, wq, k1, k2, uv):
    bsz, s, dm = x.shape
    h_bf, h_f32 = norm_modulate(x, g_norm, shift, scale, (BF16, F32))
    q = matmul(h_bf.reshape(bsz * s, dm), wq, F32)
    idx, gate = peer_retrieve(q, k1, k2)
    return peer_experts(idx, gate, h_f32.reshape(bsz * s, dm), x, gt, uv)


def _head_major(t, n_heads):
    bsz, s, _ = t.shape
    return t.reshape(bsz, s, n_heads, -1).transpose(0, 2, 1, 3)


_PROJ_GROUPS = dict(z=(0, BF16), xbc=(1, F32), na=(3, BF16), gla_q=(4, BF16), gla_k=(5, BF16), gla_v=(6, BF16),
                    gla_r=(7, BF16), swa_q=(9, BF16), swa_k=(10, BF16), swa_v=(11, BF16), gates=(12, BF16))


def _split_w_in(w):
    o = IN_OFFS
    ws = {name: w[:, o[i]:o[i + 1]].astype(BF16) for name, (i, _) in _PROJ_GROUPS.items()}
    small = jnp.concatenate([w[:, o[2]:o[3]], w[:, o[8]:o[9]]], axis=1)
    ws['small'] = jnp.pad(small, ((0, 0), (0, V7X_LANES - small.shape[1]))).astype(BF16)
    return ws


def _project(h, ws):
    bsz, s, dm = h.shape
    hf = h.reshape(bsz * s, dm)
    out = {name: matmul(hf, ws[name], dt).reshape(bsz, s, -1) for name, (_, dt) in _PROJ_GROUPS.items()}
    sm = matmul(hf, ws['small'], F32).reshape(bsz, s, V7X_LANES)
    out['dt'] = sm[:, :, :2 * SSD_HEADS]
    out['glr'] = sm[:, :, 2 * SSD_HEADS:2 * SSD_HEADS + 2 * GLA_GATE_RANK]
    return out


def _flat_heads(t):
    return t.reshape((-1,) + t.shape[-2:])


def mixer_sublayer(hx, hc, p, rope, need_ctx):
    bsz, s, _ = hx.shape
    lc = hc.shape[1]
    ws = _split_w_in(p['w_in'])
    px = _project(hx, ws)
    pc = _project(hc, ws)
    zeros_h = jnp.zeros((2, bsz, SSD_HEADS, SSD_STATE, SSD_HEADDIM), F32)
    zero_s = jnp.zeros((bsz, GLA_HEADS, GLA_HEAD_K, GLA_HEAD_V), F32)
    zeros_s = (zero_s, zero_s)

    act_c = conv_silu(pc['xbc'], p['ssd_conv_w'], p['ssd_conv_b'])
    act_x = conv_silu(px['xbc'], p['ssd_conv_w'], p['ssd_conv_b'])
    y_c, h_c = ssd_scan(act_c, pc['dt'], p['ssd_a_log'], p['ssd_dt_bias'], zeros_h)
    y_x, _ = ssd_scan(act_x, px['dt'], p['ssd_a_log'], p['ssd_dt_bias'], h_c)
    a_x = ssd_finish(y_x, act_x, px['z'], p['ssd_d'], p['ssd_norm_g'])

    dh = NA_HEAD_DIM
    scale = dh ** -0.5

    def na_heads(t):
        t = _head_major(t, 3 * NA_HEADS)
        q = head_prep(_flat_heads(t[:, :NA_HEADS]), p['na_q_norm'], None, scale).reshape(t[:, :NA_HEADS].shape)
        k = head_prep(_flat_heads(t[:, NA_HEADS:2 * NA_HEADS]), p['na_k_norm'], None, 1.0).reshape(q.shape)
        return q, k, t[:, 2 * NA_HEADS:]

    nq, nk, nv = na_heads(px['na'])
    nqc, nkc, nvc = na_heads(pc['na'])
    b_x = na_attention(nq, nk, nv, nkc, nvc, na_bias_table(p['na_rpb']))
    b_x = b_x.transpose(0, 2, 1, 3).reshape(bsz, s, BRANCH_WIDTH)

    o_c, s_c = gla_scan(pc['gla_q'], pc['gla_k'], pc['gla_v'], pc['glr'], p['gla_w_gate'], p['gla_b_gate'], zeros_s)
    o_x, _ = gla_scan(px['gla_q'], px['gla_k'], px['gla_v'], px['glr'], p['gla_w_gate'], p['gla_b_gate'], s_c)
    c_x = gla_finish(o_x, px['gla_r'], p['gla_norm_g'])

    def swa_heads(q, k, v, tabs):
        qh = _head_major(q, SWA_HEADS)
        kh = _head_major(k, SWA_KV_HEADS)
        qn = head_prep(_flat_heads(qh), p['swa_q_norm'], tabs, SWA_HEAD_DIM ** -0.5).reshape(qh.shape)
        kn = head_prep(_flat_heads(kh), p['swa_k_norm'], tabs, 1.0).reshape(kh.shape)
        qn = qn.reshape(qh.shape[0], SWA_KV_HEADS, SWA_GROUP, qh.shape[2], SWA_HEAD_DIM)
        return qn, kn, _head_major(v, SWA_KV_HEADS)

    sq, sk, sv = swa_heads(px['swa_q'], px['swa_k'], px['swa_v'], rope)
    sqc, skc, svc = swa_heads(pc['swa_q'], pc['swa_k'], pc['swa_v'], None)
    d_x = swa_attention(sq, sk, sv, skc, svc, p['swa_sink'])
    d_x = d_x.reshape(bsz, SWA_HEADS, s, SWA_HEAD_DIM).transpose(0, 2, 1, 3).reshape(bsz, s, BRANCH_WIDTH)

    wb = p['w_branch'].astype(BF16)
    flat = lambda t: t.reshape(-1, t.shape[-1])
    m_x = merge_branches([flat(a_x), flat(b_x), flat(c_x), flat(d_x)], flat(px['gates']), wb)
    if not need_ctx:
        return m_x, None

    a_c = ssd_finish(y_c, act_c, pc['z'], p['ssd_d'], p['ssd_norm_g'])
    no_sink = jnp.full((NA_HEADS,), NEG, F32)
    b_c = ctx_attention(nqc[:, :, None], nkc, nvc, no_sink)[:, :, 0]
    b_c = b_c.transpose(0, 2, 1, 3).reshape(bsz, lc, BRANCH_WIDTH)
    c_c = gla_finish(o_c, pc['gla_r'], p['gla_norm_g'])
    d_c = ctx_attention(sqc, skc, svc, p['swa_sink'])
    d_c = d_c.reshape(bsz, SWA_HEADS, lc, SWA_HEAD_DIM).transpose(0, 2, 1, 3).reshape(bsz, lc, BRANCH_WIDTH)
    m_c = merge_branches([flat(a_c), flat(b_c), flat(c_c), flat(d_c)], flat(pc['gates']), wb)
    return m_x, m_c


def kernel(x, c, ctx, c_ctx, w_ada, b_ada, g_norm1, g_norm2, w_in, ssd_conv_w, ssd_conv_b, ssd_a_log, ssd_dt_bias, ssd_d, ssd_norm_g, na_q_norm, na_k_norm, na_rpb, gla_w_gate, gla_b_gate, gla_norm_g, swa_q_norm, swa_k_norm, swa_sink, w_branch, w_out, peer_wq, peer_k1, peer_k2, peer_u, peer_v):
    bsz, s, dm = x.shape
    rope = rope_tables(s)
    n_cond = 8
    cc = jnp.zeros((n_cond, dm), F32).at[:bsz].set(c).at[bsz].set(c_ctx)
    for l in range(DEPTH):
        need_ctx = l < DEPTH - 1
        mod = ada_mod(cc, w_ada[l], b_ada[l])
        mx = [mod[:bsz, i * dm:(i + 1) * dm].reshape(bsz, 1, dm) for i in range(6)]
        mc = [mod[bsz:bsz + 1, i * dm:(i + 1) * dm].reshape(1, 1, dm) for i in range(6)]
        p = dict(w_in=w_in[l], ssd_conv_w=ssd_conv_w[l], ssd_conv_b=ssd_conv_b[l], ssd_a_log=ssd_a_log[l],
                 ssd_dt_bias=ssd_dt_bias[l], ssd_d=ssd_d[l], ssd_norm_g=ssd_norm_g[l],
                 na_q_norm=na_q_norm[l], na_k_norm=na_k_norm[l], na_rpb=na_rpb[l],
                 gla_w_gate=gla_w_gate[l], gla_b_gate=gla_b_gate[l], gla_norm_g=gla_norm_g[l],
                 swa_q_norm=swa_q_norm[l], swa_k_norm=swa_k_norm[l], swa_sink=swa_sink[l],
                 w_branch=w_branch[l])
        (hx,) = norm_modulate(x, g_norm1[l], mx[0], mx[1], (BF16,))
        (hc,) = norm_modulate(ctx, g_norm1[l], mc[0], mc[1], (BF16,))
        m_x, m_c = mixer_sublayer(hx, hc, p, rope, need_ctx)
        wo = w_out[l].astype(BF16)
        wq = peer_wq[l].astype(BF16)
        uv = jnp.concatenate([peer_u[l], peer_v[l]], axis=1).astype(F32)
        x = matmul_gated_residual(m_x, wo, x, mx[2])
        x = peer_sublayer(x, g_norm2[l], mx[3], mx[4], mx[5], wq, peer_k1[l], peer_k2[l], uv)
        if need_ctx:
            ctx = matmul_gated_residual(m_c, wo, ctx, mc[2])
            ctx = peer_sublayer(ctx, g_norm2[l], mc[3], mc[4], mc[5], wq, peer_k1[l], peer_k2[l], uv)
    return x
```

```python
import functools
import math

import numpy as np
import jax
import jax.numpy as jnp
from jax import lax
from jax.experimental import pallas as pl
from jax.experimental.pallas import tpu as pltpu

D_MODEL = 2048
DEPTH = 2
GRID_W = 64
NORM_EPS = 1e-6
N_BRANCH = 4
BRANCH_WIDTH = D_MODEL // 2

SSD_HEADDIM = 64
SSD_D_INNER = BRANCH_WIDTH
SSD_HEADS = SSD_D_INNER // SSD_HEADDIM
SSD_GROUPS = 4
SSD_STATE = 128
SSD_CONV_DIM = SSD_D_INNER + 2 * SSD_GROUPS * SSD_STATE
SSD_CONV_K = 5
SSD_CHUNK = 128

NA_HEAD_DIM = 64
NA_HEADS = BRANCH_WIDTH // NA_HEAD_DIM
NA_WIN_R = 8
NA_WIN_C = 16

GLA_HEADS = 4
GLA_V_DIM = BRANCH_WIDTH
GLA_K_DIM = BRANCH_WIDTH // 2
GLA_HEAD_K = GLA_K_DIM // GLA_HEADS
GLA_HEAD_V = GLA_V_DIM // GLA_HEADS
GLA_GATE_RANK = 16
GLA_TAU = 16.0
GLA_CHUNK = 64

SWA_HEAD_DIM = 64
SWA_HEADS = BRANCH_WIDTH // SWA_HEAD_DIM
SWA_KV_HEADS = 4
SWA_GROUP = SWA_HEADS // SWA_KV_HEADS
SWA_WINDOW = 128
SWA_BLOCK = 128
ROPE_BASE = 10000.0
ROPE_AXIS_DIM = SWA_HEAD_DIM // 2

PEER_HEADS = 8
PEER_NKEYS = 128
PEER_QDIM = 256
PEER_TOPK = 16
PEER_NSEL = PEER_HEADS * PEER_TOPK

IN_SPLITS = (SSD_D_INNER, SSD_CONV_DIM, 2 * SSD_HEADS,
             3 * BRANCH_WIDTH,
             GLA_K_DIM, GLA_K_DIM, GLA_V_DIM, GLA_V_DIM, 2 * GLA_GATE_RANK,
             SWA_HEADS * SWA_HEAD_DIM, SWA_KV_HEADS * SWA_HEAD_DIM, SWA_KV_HEADS * SWA_HEAD_DIM,
             N_BRANCH * D_MODEL)
IN_OFFS = tuple(int(v) for v in np.cumsum((0,) + IN_SPLITS))

V7X_LANES = 128
V7X_VMEM_BYTES = 64 * 1024 * 1024
VMEM_LIMIT = 48 * 1024 * 1024

F32 = jnp.float32
BF16 = jnp.bfloat16
HI = lax.Precision.HIGHEST
NEG = -1e30


def _cparams(sem):
    return pltpu.CompilerParams(dimension_semantics=sem, vmem_limit_bytes=VMEM_LIMIT)


def _dot(a, b, precision=None):
    return jnp.dot(a, b, preferred_element_type=F32, precision=precision)


def _dot_nt(a, b, precision=None):
    return lax.dot_general(a, b, (((1,), (1,)), ((), ())), preferred_element_type=F32,
                           precision=precision)


def _silu(x):
    return x / (1.0 + jnp.exp(-x))


def _softplus(x):
    return jnp.maximum(x, 0.0) + jnp.log1p(jnp.exp(-jnp.abs(x)))


def _log_sigmoid(x):
    return jnp.minimum(x, 0.0) - jnp.log1p(jnp.exp(-jnp.abs(x)))


def _mm_kernel(a_ref, b_ref, o_ref):
    o_ref[...] = _dot(a_ref[...], b_ref[...]).astype(o_ref.dtype)


def matmul(a, b, out_dtype, tm=1024, tn=512):
    m, k = a.shape
    n = b.shape[1]
    tm = min(tm, m)
    tn = min(tn, n)
    assert m % tm == 0 and n % tn == 0, (m, n, tm, tn)
    return pl.pallas_call(
        _mm_kernel,
        grid=(m // tm, n // tn),
        in_specs=[pl.BlockSpec((tm, k), lambda i, j: (i, 0)),
                  pl.BlockSpec((k, tn), lambda i, j: (0, j))],
        out_specs=pl.BlockSpec((tm, tn), lambda i, j: (i, j)),
        out_shape=jax.ShapeDtypeStruct((m, n), out_dtype),
        compiler_params=_cparams(("parallel", "parallel")),
        name="matmul",
    )(a, b)


def _mod_kernel(c_ref, w_ref, b_ref, o_ref):
    a = _silu(c_ref[...]).astype(BF16)
    o_ref[...] = _dot(a, w_ref[...].astype(BF16)) + b_ref[...]


def ada_mod(cc, w, b, tn=1024):
    m, k = cc.shape
    n = w.shape[1]
    return pl.pallas_call(
        _mod_kernel,
        grid=(n // tn,),
        in_specs=[pl.BlockSpec((m, k), lambda j: (0, 0)),
                  pl.BlockSpec((k, tn), lambda j: (0, j)),
                  pl.BlockSpec((1, tn), lambda j: (0, j))],
        out_specs=pl.BlockSpec((m, tn), lambda j: (0, j)),
        out_shape=jax.ShapeDtypeStruct((m, n), F32),
        compiler_params=_cparams(("parallel",)),
        name="ada_mod",
    )(cc, w, b.reshape(1, n))


def _normmod_kernel(x_ref, g_ref, sh_ref, sc_ref, *o_refs):
    x = x_ref[0]
    var = jnp.mean(x * x, axis=-1, keepdims=True)
    y = x * lax.rsqrt(var + NORM_EPS) * g_ref[...]
    y = y * (1.0 + sc_ref[0]) + sh_ref[0]
    for o_ref in o_refs:
        o_ref[0] = y.astype(o_ref.dtype)


def norm_modulate(x, g, shift, scale, out_dtypes, ts=512):
    bsz, s, d = x.shape
    ts = min(ts, s)
    per_batch = shift.shape[0] == bsz and bsz > 1
    mod_map = (lambda b, i: (b, 0, 0)) if per_batch else (lambda b, i: (0, 0, 0))
    outs = pl.pallas_call(
        _normmod_kernel,
        grid=(bsz, s // ts),
        in_specs=[pl.BlockSpec((1, ts, d), lambda b, i: (b, i, 0)),
                  pl.BlockSpec((1, d), lambda b, i: (0, 0)),
                  pl.BlockSpec((1, 1, d), mod_map),
                  pl.BlockSpec((1, 1, d), mod_map)],
        out_specs=[pl.BlockSpec((1, ts, d), lambda b, i: (b, i, 0)) for _ in out_dtypes],
        out_shape=[jax.ShapeDtypeStruct((bsz, s, d), dt) for dt in out_dtypes],
        compiler_params=_cparams(("parallel", "parallel")),
        name="norm_modulate",
    )(x, g.reshape(1, d), shift, scale)
    return outs


def _mm_resid_kernel(a_ref, w_ref, x_ref, gt_ref, o_ref):
    y = _dot(a_ref[...], w_ref[...])
    o_ref[0] = x_ref[0] + gt_ref[0] * y


def matmul_gated_residual(a, w, x, gate, tm=1024, tn=512):
    bsz, s, n = x.shape
    k = a.shape[1]
    tm = min(tm, s)
    nt = s // tm
    per_batch = gate.shape[0] == bsz and bsz > 1
    g_map = (lambda b, i, j: (b, 0, j)) if per_batch else (lambda b, i, j: (0, 0, j))
    return pl.pallas_call(
        _mm_resid_kernel,
        grid=(bsz, nt, n // tn),
        in_specs=[pl.BlockSpec((tm, k), lambda b, i, j: (b * nt + i, 0)),
                  pl.BlockSpec((k, tn), lambda b, i, j: (0, j)),
                  pl.BlockSpec((1, tm, tn), lambda b, i, j: (b, i, j)),
                  pl.BlockSpec((1, 1, tn), g_map)],
        out_specs=pl.BlockSpec((1, tm, tn), lambda b, i, j: (b, i, j)),
        out_shape=jax.ShapeDtypeStruct((bsz, s, n), F32),
        compiler_params=_cparams(("parallel", "parallel", "parallel")),
        name="out_proj_residual",
    )(a, w, x, gate)


def _merge_kernel(a_ref, b_ref, c_ref, d_ref, g0_ref, g1_ref, g2_ref, g3_ref, w_ref, o_ref):
    acc = None
    for i, (br, gr) in enumerate(((a_ref, g0_ref), (b_ref, g1_ref), (c_ref, g2_ref), (d_ref, g3_ref))):
        y = _dot(br[...], w_ref[i])
        gate = 1.0 / (1.0 + jnp.exp(-gr[...].astype(F32)))
        acc = gate * y if acc is None else acc + gate * y
    o_ref[...] = acc.astype(o_ref.dtype)


def merge_branches(outs, gate_logits, w_branch, tm=512, tn=512):
    t, kb = outs[0].shape
    d = w_branch.shape[-1]
    tm = min(tm, t)
    nj = d // tn
    in_specs = [pl.BlockSpec((tm, kb), lambda i, j: (i, 0)) for _ in range(N_BRANCH)]
    in_specs += [pl.BlockSpec((tm, tn), functools.partial(lambda i, j, br: (i, br * nj + j), br=br))
                 for br in range(N_BRANCH)]
    in_specs += [pl.BlockSpec((N_BRANCH, kb, tn), lambda i, j: (0, 0, j))]
    return pl.pallas_call(
        _merge_kernel,
        grid=(t // tm, nj),
        in_specs=in_specs,
        out_specs=pl.BlockSpec((tm, tn), lambda i, j: (i, j)),
        out_shape=jax.ShapeDtypeStruct((t, d), BF16),
        compiler_params=_cparams(("parallel", "parallel")),
        name="merge_branches",
    )(*outs, gate_logits, gate_logits, gate_logits, gate_logits, w_branch)


def _headprep_kernel(x_ref, g_ref, cos_ref, sin_ref, rot_ref, o_ref, *, rope, scale):
    x = x_ref[0].astype(F32)
    var = jnp.mean(x * x, axis=-1, keepdims=True)
    y = x * lax.rsqrt(var + NORM_EPS) * g_ref[...]
    if rope:
        y = y * cos_ref[...] + _dot(y, rot_ref[...], HI) * sin_ref[...]
    o_ref[0] = (y * scale).astype(o_ref.dtype)


def head_prep(x, g, rope_tabs, scale, ts=1024):
    n, s, dh = x.shape
    ts = min(ts, s)
    rope = rope_tabs is not None
    if rope:
        cos, sin, rot = rope_tabs
    else:
        cos = sin = jnp.zeros((s, dh), F32)
        rot = jnp.zeros((dh, dh), F32)
    return pl.pallas_call(
        functools.partial(_headprep_kernel, rope=rope, scale=scale),
        grid=(n, s // ts),
        in_specs=[pl.BlockSpec((1, ts, dh), lambda h, i: (h, i, 0)),
                  pl.BlockSpec((1, dh), lambda h, i: (0, 0)),
                  pl.BlockSpec((ts, dh), lambda h, i: (i, 0)),
                  pl.BlockSpec((ts, dh), lambda h, i: (i, 0)),
                  pl.BlockSpec((dh, dh), lambda h, i: (0, 0))],
        out_specs=pl.BlockSpec((1, ts, dh), lambda h, i: (h, i, 0)),
        out_shape=jax.ShapeDtypeStruct((n, s, dh), BF16),
        compiler_params=_cparams(("parallel", "parallel")),
        name="head_prep",
    )(x, g.reshape(1, dh), cos, sin, rot)


def rope_tables(s):
    t = np.arange(s)
    row = (t // GRID_W).astype(np.float32)
    col = (t % GRID_W).astype(np.float32)
    nf = ROPE_AXIS_DIM // 2
    inv = jnp.asarray(ROPE_BASE, F32) ** (-jnp.arange(nf, dtype=F32) / nf)
    ar = jnp.asarray(row)[:, None] * inv
    ac = jnp.asarray(col)[:, None] * inv
    cos = jnp.concatenate([jnp.cos(ar), jnp.cos(ar), jnp.cos(ac), jnp.cos(ac)], axis=-1)
    sin = jnp.concatenate([jnp.sin(ar), jnp.sin(ar), jnp.sin(ac), jnp.sin(ac)], axis=-1)
    rot = np.zeros((SWA_HEAD_DIM, SWA_HEAD_DIM), np.float32)
    for d in range(SWA_HEAD_DIM):
        if d % ROPE_AXIS_DIM < nf:
            rot[d + nf, d] = -1.0
        else:
            rot[d - nf, d] = 1.0
    return cos, sin, jnp.asarray(rot)


NA_ROWS_PER_STEP = 8


def _na_kernel(q_ref, k_ref, v_ref, kc_ref, vc_ref, bias_ref, o_ref, *, n_rows):
    i = pl.program_id(2)
    kc = kc_ref[0, 0]
    vc = vc_ref[0, 0]
    n_nb = NA_WIN_R * GRID_W

    def body(rr, carry):
        r = i * NA_ROWS_PER_STEP + rr
        r0 = jnp.clip(r - NA_WIN_R // 2, 0, n_rows - NA_WIN_R)
        qoff = pl.multiple_of(rr * GRID_W, GRID_W)
        koff = pl.multiple_of(r0 * GRID_W, GRID_W)
        q = q_ref[0, 0, pl.ds(qoff, GRID_W), :]
        kn = k_ref[0, 0, pl.ds(koff, n_nb), :]
        vn = v_ref[0, 0, pl.ds(koff, n_nb), :]
        s_nb = _dot_nt(q, kn) + bias_ref[r - r0, 0]
        s_cx = _dot_nt(q, kc)
        m = jnp.maximum(jnp.max(s_nb, axis=-1, keepdims=True), jnp.max(s_cx, axis=-1, keepdims=True))
        p_nb = jnp.exp(s_nb - m)
        p_cx = jnp.exp(s_cx - m)
        den = jnp.sum(p_nb, axis=-1, keepdims=True) + jnp.sum(p_cx, axis=-1, keepdims=True)
        o = _dot(p_nb.astype(BF16), vn) + _dot(p_cx.astype(BF16), vc)
        o_ref[0, 0, pl.ds(qoff, GRID_W), :] = (o / den).astype(o_ref.dtype)
        return carry

    lax.fori_loop(0, NA_ROWS_PER_STEP, body, 0)


def na_bias_table(rpb):
    nh = rpb.shape[0]
    wr, wc, gw = NA_WIN_R, NA_WIN_C, GRID_W
    rpb = rpb.astype(F32)

    def toeplitz(vec, n, period):
        tiled = jnp.broadcast_to(vec[..., None, :], vec.shape[:-1] + (n, period))
        flat = tiled.reshape(vec.shape[:-1] + (n * period,))[..., :n * (period - 1)]
        return flat.reshape(vec.shape[:-1] + (n, period - 1))[..., :n]

    rows = jnp.concatenate([rpb[:, wr - 1:], jnp.zeros((nh, 1, 2 * wc - 1), F32), rpb[:, :wr - 1]], axis=1)
    tab = toeplitz(jnp.moveaxis(rows, 1, -1), wr, 2 * wr)
    tab = jnp.moveaxis(tab, 1, -1)
    fill = jnp.zeros(tab.shape[:-1] + (2 * gw - (2 * wc - 1),), F32)
    cols_ext = jnp.concatenate([tab[..., wc - 1:], fill, tab[..., :wc - 1]], axis=-1)
    tab = toeplitz(cols_ext, gw, 2 * gw)
    cols = np.arange(gw)
    c_start = np.clip(cols - wc // 2, 0, gw - wc)
    valid = (cols[None, :] >= c_start[:, None]) & (cols[None, :] < c_start[:, None] + wc)
    tab = jnp.where(jnp.asarray(valid)[None, None, None], tab, NEG)
    tab = tab.transpose(1, 0, 3, 2, 4)
    return tab.reshape(wr, nh, gw, wr * gw)


def na_attention(q, k, v, kc, vc, bias):
    bsz, h, s, dh = q.shape
    lc = kc.shape[2]
    n_rows = s // GRID_W
    tq = NA_ROWS_PER_STEP * GRID_W
    n_nb = NA_WIN_R * GRID_W
    return pl.pallas_call(
        functools.partial(_na_kernel, n_rows=n_rows),
        grid=(bsz, h, s // tq),
        in_specs=[pl.BlockSpec((1, 1, tq, dh), lambda b, hh, i: (b, hh, i, 0)),
                  pl.BlockSpec((1, 1, s, dh), lambda b, hh, i: (b, hh, 0, 0)),
                  pl.BlockSpec((1, 1, s, dh), lambda b, hh, i: (b, hh, 0, 0)),
                  pl.BlockSpec((1, 1, lc, dh), lambda b, hh, i: (b, hh, 0, 0)),
                  pl.BlockSpec((1, 1, lc, dh), lambda b, hh, i: (b, hh, 0, 0)),
                  pl.BlockSpec((NA_WIN_R, 1, GRID_W, n_nb), lambda b, hh, i: (0, hh, 0, 0))],
        out_specs=pl.BlockSpec((1, 1, tq, dh), lambda b, hh, i: (b, hh, i, 0)),
        out_shape=jax.ShapeDtypeStruct((bsz, h, s, dh), BF16),
        compiler_params=_cparams(("parallel", "parallel", "arbitrary")),
        name="na_attention",
    )(q, k, v, kc, vc, bias)


def _swa_kernel(sink_ref, q_ref, k_ref, v_ref, kc_ref, vc_ref, o_ref, *, seq):
    kh = pl.program_id(1)
    n = pl.program_id(2)
    span = 3 * SWA_BLOCK
    start = pl.multiple_of(jnp.clip((n - 1) * SWA_BLOCK, 0, seq - span), SWA_BLOCK)
    kw = k_ref[0, 0, pl.ds(start, span), :]
    vw = v_ref[0, 0, pl.ds(start, span), :]
    kc = kc_ref[0, 0]
    vc = vc_ref[0, 0]
    qpos = n * SWA_BLOCK + lax.broadcasted_iota(jnp.int32, (SWA_BLOCK, span), 0)
    kpos = start + lax.broadcasted_iota(jnp.int32, (SWA_BLOCK, span), 1)
    valid = jnp.abs(qpos - kpos) <= SWA_WINDOW
    for g in range(SWA_GROUP):
        q = q_ref[0, 0, g]
        s_loc = jnp.where(valid, _dot_nt(q, kw), NEG)
        s_ctx = _dot_nt(q, kc)
        sink = sink_ref[kh * SWA_GROUP + g]
        m = jnp.maximum(jnp.max(s_loc, axis=-1, keepdims=True), jnp.max(s_ctx, axis=-1, keepdims=True))
        m = jnp.maximum(m, sink)
        p_loc = jnp.exp(s_loc - m)
        p_ctx = jnp.exp(s_ctx - m)
        den = (jnp.sum(p_loc, axis=-1, keepdims=True) + jnp.sum(p_ctx, axis=-1, keepdims=True)
               + jnp.exp(sink - m))
        o = _dot(p_loc.astype(BF16), vw) + _dot(p_ctx.astype(BF16), vc)
        o_ref[0, 0, g] = (o / den).astype(o_ref.dtype)


def swa_attention(q, k, v, kc, vc, sink):
    bsz, hk, grp, s, dh = q.shape
    lc = kc.shape[2]
    assert s >= 3 * SWA_BLOCK
    return pl.pallas_call(
        functools.partial(_swa_kernel, seq=s),
        grid=(bsz, hk, s // SWA_BLOCK),
        in_specs=[pl.BlockSpec(memory_space=pltpu.SMEM),
                  pl.BlockSpec((1, 1, grp, SWA_BLOCK, dh), lambda b, h, n: (b, h, 0, n, 0)),
                  pl.BlockSpec((1, 1, s, dh), lambda b, h, n: (b, h, 0, 0)),
                  pl.BlockSpec((1, 1, s, dh), lambda b, h, n: (b, h, 0, 0)),
                  pl.BlockSpec((1, 1, lc, dh), lambda b, h, n: (b, h, 0, 0)),
                  pl.BlockSpec((1, 1, lc, dh), lambda b, h, n: (b, h, 0, 0))],
        out_specs=pl.BlockSpec((1, 1, grp, SWA_BLOCK, dh), lambda b, h, n: (b, h, 0, n, 0)),
        out_shape=jax.ShapeDtypeStruct((bsz, hk, grp, s, dh), BF16),
        compiler_params=_cparams(("parallel", "parallel", "arbitrary")),
        name="swa_attention",
    )(sink.astype(F32), q, k, v, kc, vc)


def _ctx_attn_kernel(sink_ref, q_ref, k_ref, v_ref, o_ref, *, grp):
    kh = pl.program_id(1)
    k = k_ref[0, 0]
    v = v_ref[0, 0]
    for g in range(grp):
        q = q_ref[0, 0, g]
        s = _dot_nt(q, k)
        sink = sink_ref[kh * grp + g]
        m = jnp.maximum(jnp.max(s, axis=-1, keepdims=True), sink)
        p = jnp.exp(s - m)
        den = jnp.sum(p, axis=-1, keepdims=True) + jnp.exp(sink - m)
        o_ref[0, 0, g] = (_dot(p.astype(BF16), v) / den).astype(o_ref.dtype)


def ctx_attention(q, k, v, sink):
    bsz, hk, grp, n, dh = q.shape
    return pl.pallas_call(
        functools.partial(_ctx_attn_kernel, grp=grp),
        grid=(bsz, hk),
        in_specs=[pl.BlockSpec(memory_space=pltpu.SMEM),
                  pl.BlockSpec((1, 1, grp, n, dh), lambda b, h: (b, h, 0, 0, 0)),
                  pl.BlockSpec((1, 1, n, dh), lambda b, h: (b, h, 0, 0)),
                  pl.BlockSpec((1, 1, n, dh), lambda b, h: (b, h, 0, 0))],
        out_specs=pl.BlockSpec((1, 1, grp, n, dh), lambda b, h: (b, h, 0, 0, 0)),
        out_shape=jax.ShapeDtypeStruct((bsz, hk, grp, n, dh), BF16),
        compiler_params=_cparams(("parallel", "parallel")),
        name="ctx_attention",
    )(sink.astype(F32), q, k, v)


CONV_PAD = 8
CONV_ROWS = 256
CONV_COLS = 256


def _conv_kernel(x_ref, w_ref, b_ref, o_ref, *, seq):
    rows = min(CONV_ROWS, seq)
    n = seq // rows

    def body(i, carry):
        base = pl.multiple_of(i * rows, rows)
        acc = jnp.zeros((rows, CONV_COLS), F32) + b_ref[...]
        prev_off = pl.multiple_of(jnp.maximum(base - CONV_PAD, 0), CONV_PAD)
        next_off = pl.multiple_of(jnp.minimum(base + rows, seq - CONV_PAD), CONV_PAD)
        prev = jnp.where(i > 0, x_ref[0, pl.ds(prev_off, CONV_PAD), :], 0.0)
        nxt = jnp.where(i < n - 1, x_ref[0, pl.ds(next_off, CONV_PAD), :], 0.0)
        halo = jnp.concatenate([prev, x_ref[0, pl.ds(base, rows), :], nxt], axis=0)
        for k in range(SSD_CONV_K):
            off = CONV_PAD + k - SSD_CONV_K // 2
            acc = acc + w_ref[k:k + 1, :] * halo[off:off + rows, :]
        o_ref[0, pl.ds(base, rows), :] = _silu(acc).astype(o_ref.dtype)
        return carry

    lax.fori_loop(0, seq // rows, body, 0)


def conv_silu(xbc, w, b):
    bsz, s, c = xbc.shape
    return pl.pallas_call(
        functools.partial(_conv_kernel, seq=s),
        grid=(bsz, c // CONV_COLS),
        in_specs=[pl.BlockSpec((1, s, CONV_COLS), lambda bb, j: (bb, 0, j)),
                  pl.BlockSpec((SSD_CONV_K, CONV_COLS), lambda bb, j: (0, j)),
                  pl.BlockSpec((1, CONV_COLS), lambda bb, j: (0, j))],
        out_specs=pl.BlockSpec((1, s, CONV_COLS), lambda bb, j: (bb, 0, j)),
        out_shape=jax.ShapeDtypeStruct((bsz, s, c), BF16),
        compiler_params=_cparams(("parallel", "parallel")),
        name="conv_silu",
    )(xbc, w.astype(F32), b.reshape(1, c).astype(F32))


def _scan_masks(length, d):
    row = lax.broadcasted_iota(jnp.int32, (length, length), 0)
    col = lax.broadcasted_iota(jnp.int32, (length, length), 1)
    return (row - col) * (1 - 2 * d) >= 0


def _ssd_kernel(xs_ref, bm_ref, cm_ref, bmt_ref, dt_ref, dtt_ref, alog_ref, alogt_ref, dtb_ref, dtbt_ref,
                h0_ref, y_ref, hout_ref, state_ref, *, n_chunks):
    d = pl.program_id(0)
    c = pl.program_id(2)

    @pl.when(c == 0)
    def _():
        state_ref[...] = h0_ref[0, 0]

    length = SSD_CHUNK
    incl = _scan_masks(length, d)
    tri = incl.astype(F32)
    dt = _softplus(dt_ref[0, 0] + dtb_ref[0])
    dtt = _softplus(dtt_ref[0, 0] + dtbt_ref[0])
    ad = dt * (-jnp.exp(alog_ref[0]))
    adt = dtt * (-jnp.exp(alogt_ref[0]))
    acs = _dot(tri, ad, HI)
    acst = _dot_nt(adt, tri, HI)
    tot = jnp.sum(ad, axis=0, keepdims=True)
    tott = jnp.sum(adt, axis=1, keepdims=True)
    rep = SSD_HEADS // SSD_GROUPS
    for gi in range(SSD_GROUPS):
        gsl = slice(gi * SSD_STATE, (gi + 1) * SSD_STATE)
        cg = cm_ref[0, :, gsl]
        bg = bm_ref[0, :, gsl]
        bgt = bmt_ref[0, gsl, :].astype(F32)
        cb = _dot_nt(cg, bg)
        for hh in range(rep):
            h = gi * rep + hh
            psl = slice(h * SSD_HEADDIM, (h + 1) * SSD_HEADDIM)
            a_col = acs[:, h:h + 1]
            a_row = acst[h:h + 1, :]
            lmat = jnp.exp(jnp.where(incl, a_col - a_row, NEG))
            xdt = (xs_ref[0, :, psl].astype(F32) * dt[:, h:h + 1]).astype(BF16)
            st = state_ref[h]
            y = _dot((cb * lmat).astype(BF16), xdt)
            y = y + jnp.exp(a_col) * _dot(cg, st.astype(BF16))
            dec = jnp.exp(tott[h:h + 1, :] - a_row)
            state_ref[h] = jnp.exp(tot[:, h:h + 1]) * st + _dot((bgt * dec).astype(BF16), xdt)
            y_ref[0, 0, :, psl] = y.astype(y_ref.dtype)

    @pl.when(c == n_chunks - 1)
    def _():
        hout_ref[0, 0] = state_ref[...]


def _chunk_index(d, c, n_chunks):
    return c + d * (n_chunks - 1 - 2 * c)


def ssd_scan(xbc_act, dt_raw, a_log, dt_bias, h0):
    bsz, s, _ = xbc_act.shape
    nc = s // SSD_CHUNK
    hh = SSD_HEADS
    ng = SSD_GROUPS * SSD_STATE
    bmt = jnp.swapaxes(xbc_act[:, :, SSD_D_INNER:SSD_D_INNER + ng], 1, 2)
    dt2 = dt_raw.reshape(bsz, s, 2, hh).transpose(2, 0, 1, 3)
    dtt = dt2.transpose(0, 1, 3, 2)
    a_log = a_log.astype(F32)
    dt_bias = dt_bias.astype(F32)
    cmap = functools.partial(_chunk_index, n_chunks=nc)
    nb_x = SSD_D_INNER // ng
    return pl.pallas_call(
        functools.partial(_ssd_kernel, n_chunks=nc),
        grid=(2, bsz, nc),
        in_specs=[pl.BlockSpec((1, SSD_CHUNK, SSD_D_INNER), lambda d, b, c: (b, cmap(d, c), 0)),
                  pl.BlockSpec((1, SSD_CHUNK, ng), lambda d, b, c: (b, cmap(d, c), nb_x)),
                  pl.BlockSpec((1, SSD_CHUNK, ng), lambda d, b, c: (b, cmap(d, c), nb_x + 1)),
                  pl.BlockSpec((1, ng, SSD_CHUNK), lambda d, b, c: (b, 0, cmap(d, c))),
                  pl.BlockSpec((1, 1, SSD_CHUNK, hh), lambda d, b, c: (d, b, cmap(d, c), 0)),
                  pl.BlockSpec((1, 1, hh, SSD_CHUNK), lambda d, b, c: (d, b, 0, cmap(d, c))),
                  pl.BlockSpec((1, 1, hh), lambda d, b, c: (d, 0, 0)),
                  pl.BlockSpec((1, hh, 1), lambda d, b, c: (d, 0, 0)),
                  pl.BlockSpec((1, 1, hh), lambda d, b, c: (d, 0, 0)),
                  pl.BlockSpec((1, hh, 1), lambda d, b, c: (d, 0, 0)),
                  pl.BlockSpec((1, 1, hh, SSD_STATE, SSD_HEADDIM), lambda d, b, c: (d, b, 0, 0, 0))],
        out_specs=[pl.BlockSpec((1, 1, SSD_CHUNK, SSD_D_INNER), lambda d, b, c: (d, b, cmap(d, c), 0)),
                   pl.BlockSpec((1, 1, hh, SSD_STATE, SSD_HEADDIM), lambda d, b, c: (d, b, 0, 0, 0))],
        out_shape=[jax.ShapeDtypeStruct((2, bsz, s, SSD_D_INNER), BF16),
                   jax.ShapeDtypeStruct((2, bsz, hh, SSD_STATE, SSD_HEADDIM), F32)],
        scratch_shapes=[pltpu.VMEM((hh, SSD_STATE, SSD_HEADDIM), F32)],
        compiler_params=_cparams(("parallel", "parallel", "arbitrary")),
        name="ssd_scan",
    )(xbc_act, xbc_act, xbc_act, bmt, dt2, dtt,
      a_log.reshape(2, 1, hh), a_log.reshape(2, hh, 1), dt_bias.reshape(2, 1, hh), dt_bias.reshape(2, hh, 1), h0)


def _ssd_finish_kernel(y_ref, xs_ref, z_ref, d_ref, g_ref, o_ref):
    y = y_ref[0, 0].astype(F32) + y_ref[1, 0].astype(F32) + d_ref[...] * xs_ref[0].astype(F32)
    z = z_ref[0].astype(F32)
    u = y * _silu(z)
    var = jnp.mean(u * u, axis=-1, keepdims=True)
    o_ref[0] = (u * lax.rsqrt(var + NORM_EPS) * g_ref[...]).astype(o_ref.dtype)


def ssd_finish(y, xbc_act, z, d_skip, norm_g, ts=512):
    _, bsz, s, di = y.shape
    ts = min(ts, s)
    dvec = jnp.repeat(d_skip.astype(F32), SSD_HEADDIM).reshape(1, di)
    return pl.pallas_call(
        _ssd_finish_kernel,
        grid=(bsz, s // ts),
        in_specs=[pl.BlockSpec((2, 1, ts, di), lambda b, i: (0, b, i, 0)),
                  pl.BlockSpec((1, ts, di), lambda b, i: (b, i, 0)),
                  pl.BlockSpec((1, ts, di), lambda b, i: (b, i, 0)),
                  pl.BlockSpec((1, di), lambda b, i: (0, 0)),
                  pl.BlockSpec((1, di), lambda b, i: (0, 0))],
        out_specs=pl.BlockSpec((1, ts, di), lambda b, i: (b, i, 0)),
        out_shape=jax.ShapeDtypeStruct((bsz, s, di), BF16),
        compiler_params=_cparams(("parallel", "parallel")),
        name="ssd_finish",
    )(y, xbc_act, z, dvec, norm_g.reshape(1, di).astype(F32))


GLA_STEP = 128


def _gla_kernel(q_ref, k_ref, kt_ref, v_ref, glr_ref, glrt_ref, wg_ref, wgt_ref, bg_ref, bgt_ref,
                s0_ref, o_ref, sout_ref, state_ref, *, n_steps, reverse):
    c = pl.program_id(1)

    @pl.when(c == 0)
    def _():
        state_ref[...] = s0_ref[0]

    length = GLA_CHUNK
    incl = _scan_masks(length, int(reverse))
    tri = incl.astype(F32)
    subs = range(GLA_STEP // length)
    for sub in (reversed(subs) if reverse else subs):
        tsl = slice(sub * length, (sub + 1) * length)
        g = _log_sigmoid(_dot(glr_ref[0, tsl, :], wg_ref[...], HI) + bg_ref[...]) / GLA_TAU
        gt = _log_sigmoid(_dot(wgt_ref[...], glrt_ref[0, :, tsl], HI) + bgt_ref[...]) / GLA_TAU
        gc = _dot(tri, g, HI)
        gct = _dot_nt(gt, tri, HI)
        tott = jnp.sum(gt, axis=1, keepdims=True)
        q_in = (q_ref[0, tsl, :].astype(F32) * (GLA_HEAD_K ** -0.5) * jnp.exp(gc)).astype(BF16)
        k_in = (k_ref[0, tsl, :].astype(F32) * jnp.exp(-gc)).astype(BF16)
        k_out_t = (kt_ref[0, :, tsl].astype(F32) * jnp.exp(tott - gct)).astype(BF16)
        dec_t = jnp.exp(tott)
        for h in range(GLA_HEADS):
            ksl = slice(h * GLA_HEAD_K, (h + 1) * GLA_HEAD_K)
            vsl = slice(h * GLA_HEAD_V, (h + 1) * GLA_HEAD_V)
            qh = q_in[:, ksl]
            vh = v_ref[0, tsl, vsl]
            att = jnp.where(incl, _dot_nt(qh, k_in[:, ksl]), 0.0)
            st = state_ref[h]
            o = _dot(att.astype(BF16), vh) + _dot(qh, st.astype(BF16))
            state_ref[h] = st * dec_t[ksl, :] + _dot(k_out_t[ksl, :], vh)
            o_ref[0, tsl, vsl] = o.astype(o_ref.dtype)

    @pl.when(c == n_steps - 1)
    def _():
        sout_ref[0] = state_ref[...]


def gla_scan(q, k, v, glr, w_gate, b_gate, s0):
    bsz, s, kd = q.shape
    vd = v.shape[-1]
    ns = s // GLA_STEP
    r = GLA_GATE_RANK
    kt = jnp.swapaxes(k, 1, 2)
    w_gate = w_gate.astype(F32)
    b_gate = b_gate.astype(F32)
    outs, states = [], []
    for d in range(2):
        cmap = (lambda c: ns - 1 - c) if d else (lambda c: c)
        glr_d = glr[:, :, d * r:(d + 1) * r]
        o, st = pl.pallas_call(
            functools.partial(_gla_kernel, n_steps=ns, reverse=bool(d)),
            grid=(bsz, ns),
            in_specs=[pl.BlockSpec((1, GLA_STEP, kd), lambda b, c, cmap=cmap: (b, cmap(c), 0)),
                      pl.BlockSpec((1, GLA_STEP, kd), lambda b, c, cmap=cmap: (b, cmap(c), 0)),
                      pl.BlockSpec((1, kd, GLA_STEP), lambda b, c, cmap=cmap: (b, 0, cmap(c))),
                      pl.BlockSpec((1, GLA_STEP, vd), lambda b, c, cmap=cmap: (b, cmap(c), 0)),
                      pl.BlockSpec((1, GLA_STEP, r), lambda b, c, cmap=cmap: (b, cmap(c), 0)),
                      pl.BlockSpec((1, r, GLA_STEP), lambda b, c, cmap=cmap: (b, 0, cmap(c))),
                      pl.BlockSpec((r, kd), lambda b, c: (0, 0)),
                      pl.BlockSpec((kd, r), lambda b, c: (0, 0)),
                      pl.BlockSpec((1, kd), lambda b, c: (0, 0)),
                      pl.BlockSpec((kd, 1), lambda b, c: (0, 0)),
                      pl.BlockSpec((1, GLA_HEADS, GLA_HEAD_K, GLA_HEAD_V), lambda b, c: (b, 0, 0, 0))],
            out_specs=[pl.BlockSpec((1, GLA_STEP, vd), lambda b, c, cmap=cmap: (b, cmap(c), 0)),
                       pl.BlockSpec((1, GLA_HEADS, GLA_HEAD_K, GLA_HEAD_V), lambda b, c: (b, 0, 0, 0))],
            out_shape=[jax.ShapeDtypeStruct((bsz, s, vd), BF16),
                       jax.ShapeDtypeStruct((bsz, GLA_HEADS, GLA_HEAD_K, GLA_HEAD_V), F32)],
            scratch_shapes=[pltpu.VMEM((GLA_HEADS, GLA_HEAD_K, GLA_HEAD_V), F32)],
            compiler_params=_cparams(("parallel", "arbitrary")),
            name="gla_scan_bwd" if d else "gla_scan_fwd",
        )(q, k, kt, v, glr_d, jnp.swapaxes(glr_d, 1, 2), w_gate[d], w_gate[d].T,
          b_gate[d].reshape(1, kd), b_gate[d].reshape(kd, 1), s0[d])
        outs.append(o)
        states.append(st)
    return outs, states


def _gla_finish_kernel(of_ref, ob_ref, r_ref, g_ref, out_ref):
    o = of_ref[0].astype(F32) + ob_ref[0].astype(F32)
    r = r_ref[0].astype(F32)
    for h in range(GLA_HEADS):
        vsl = slice(h * GLA_HEAD_V, (h + 1) * GLA_HEAD_V)
        oh = o[:, vsl]
        var = jnp.mean(oh * oh, axis=-1, keepdims=True)
        y = oh * lax.rsqrt(var + NORM_EPS) * g_ref[...]
        out_ref[0, :, vsl] = (y * _silu(r[:, vsl])).astype(out_ref.dtype)


def gla_finish(o, r, norm_g, ts=512):
    bsz, s, vd = o[0].shape
    ts = min(ts, s)
    return pl.pallas_call(
        _gla_finish_kernel,
        grid=(bsz, s // ts),
        in_specs=[pl.BlockSpec((1, ts, vd), lambda b, i: (b, i, 0)),
                  pl.BlockSpec((1, ts, vd), lambda b, i: (b, i, 0)),
                  pl.BlockSpec((1, ts, vd), lambda b, i: (b, i, 0)),
                  pl.BlockSpec((1, GLA_HEAD_V), lambda b, i: (0, 0))],
        out_specs=pl.BlockSpec((1, ts, vd), lambda b, i: (b, i, 0)),
        out_shape=jax.ShapeDtypeStruct((bsz, s, vd), BF16),
        compiler_params=_cparams(("parallel", "parallel")),
        name="gla_finish",
    )(o[0], o[1], r, norm_g.reshape(1, GLA_HEAD_V).astype(F32))


PEER_SCORE_TOKENS = 128
PEER_TOKENS = 128
PEER_SLOTS = 3
PEER_ROWS = 8


def _extract_topk(s, rows, vals_ref, pos_ref):
    iota = lax.broadcasted_iota(jnp.int32, s.shape, 0).astype(F32)

    def body(j, cur):
        m = jnp.max(cur, axis=0, keepdims=True)
        pos = jnp.min(jnp.where(cur == m, iota, float(rows)), axis=0, keepdims=True)
        vals_ref[pl.ds(j, 1), :] = m
        pos_ref[pl.ds(j, 1), :] = pos
        return jnp.where(iota == pos, -jnp.inf, cur)

    lax.fori_loop(0, PEER_TOPK, body, s)


def _peer_score_kernel(q_ref, k1_ref, k2_ref, idx_ref, gate_ref, v1_ref, p1_ref, v2_ref, p2_ref, vt_ref, pt_ref):
    half = PEER_QDIM // 2
    q = q_ref[...]
    _extract_topk(_dot_nt(k1_ref[...], q[:, :half], HI), PEER_NKEYS, v1_ref, p1_ref)
    _extract_topk(_dot_nt(k2_ref[...], q[:, half:], HI), PEER_NKEYS, v2_ref, p2_ref)
    v1 = v1_ref[...]
    v2 = v2_ref[...]
    p1 = p1_ref[...]
    p2 = p2_ref[...]
    cand = jnp.concatenate([v1[a:a + 1, :] + v2 for a in range(PEER_TOPK)], axis=0)
    _extract_topk(cand, PEER_TOPK * PEER_TOPK, vt_ref, pt_ref)
    pos = pt_ref[...]
    ia = jnp.floor(pos * (1.0 / PEER_TOPK))
    ib = pos - ia * float(PEER_TOPK)
    e1 = jnp.zeros_like(pos)
    e2 = jnp.zeros_like(pos)
    for a in range(PEER_TOPK):
        e1 = e1 + jnp.where(ia == float(a), p1[a:a + 1, :], 0.0)
        e2 = e2 + jnp.where(ib == float(a), p2[a:a + 1, :], 0.0)
    idx_ref[...] = (e1 * float(PEER_NKEYS) + e2).astype(jnp.int32)
    top = vt_ref[...]
    p = jnp.exp(top - top[0:1, :])
    gate_ref[...] = p / jnp.sum(p, axis=0, keepdims=True)


def peer_retrieve(q, k1, k2):
    t = q.shape[0]
    tt = min(PEER_SCORE_TOKENS, t)
    kk = PEER_TOPK
    return pl.pallas_call(
        _peer_score_kernel,
        grid=(t // tt, PEER_HEADS),
        in_specs=[pl.BlockSpec((tt, PEER_QDIM), lambda i, h: (i, h)),
                  pl.BlockSpec((PEER_NKEYS, PEER_QDIM // 2), lambda i, h: (0, 0)),
                  pl.BlockSpec((PEER_NKEYS, PEER_QDIM // 2), lambda i, h: (0, 0))],
        out_specs=[pl.BlockSpec((kk, tt), lambda i, h: (h, i)),
                   pl.BlockSpec((kk, tt), lambda i, h: (h, i))],
        out_shape=[jax.ShapeDtypeStruct((PEER_NSEL, t), jnp.int32),
                   jax.ShapeDtypeStruct((PEER_NSEL, t), F32)],
        scratch_shapes=[pltpu.VMEM((kk, tt), F32) for _ in range(6)],
        compiler_params=_cparams(("parallel", "parallel")),
        name="peer_retrieve",
    )(q, k1.astype(F32), k2.astype(F32))


def _gelu(x):
    return 0.5 * x * (1.0 + lax.erf(x * (1.0 / math.sqrt(2.0))))


def pack_expert_tables(u, v):
    def pack(tab):
        bits = lax.bitcast_convert_type(tab.astype(BF16), jnp.uint16).astype(jnp.uint32)
        half = tab.shape[1] // 2
        return bits[:, :half] | (bits[:, half:] << 16)

    return jnp.concatenate([pack(u), pack(v)], axis=1)[:, None, :]


def _unpack_pair(words):
    lo = pltpu.bitcast(words << 16, F32)
    hi = pltpu.bitcast(words & jnp.uint32(0xFFFF0000), F32)
    return lo, hi


def _peer_expert_kernel(idx_ref, gate_ref, h_ref, x_ref, gt_ref, uv_hbm, o_ref, buf, sem, *, tokens):
    half = D_MODEL // 2
    n_groups = PEER_NSEL // PEER_ROWS
    depth = PEER_SLOTS - 1

    def row_copy(e, slot, k):
        return pltpu.make_async_copy(uv_hbm.at[e], buf.at[slot, pl.ds(k, 1), :], sem.at[slot])

    def issue(tok, slot, ks):
        for k in ks:
            row_copy(idx_ref[k, tok], slot, k).start()

    def wait(slot):
        pltpu.make_async_copy(uv_hbm.at[pl.ds(0, PEER_NSEL), 0, :], buf.at[slot], sem.at[slot]).wait()

    lane = lax.broadcasted_iota(jnp.int32, (PEER_NSEL, tokens), 1)

    def compute(j, slot, prefetch):
        ta = h_ref[pl.ds(j, 1), 0:half]
        tb = h_ref[pl.ds(j, 1), half:2 * half]
        gcol = jnp.sum(jnp.where(lane == j, gate_ref[...], 0.0), axis=1, keepdims=True)
        acc_a = jnp.zeros((PEER_ROWS, half), F32)
        acc_b = jnp.zeros((PEER_ROWS, half), F32)
        for g in range(n_groups):
            rows = pl.ds(g * PEER_ROWS, PEER_ROWS)
            if prefetch:
                issue(j + depth, (slot + depth) % PEER_SLOTS, range(g * PEER_ROWS, (g + 1) * PEER_ROWS))
            ua, ub = _unpack_pair(buf[slot, rows, 0:half])
            act = jnp.sum(ua * ta + ub * tb, axis=1, keepdims=True)
            a = _gelu(act) * gcol[g * PEER_ROWS:(g + 1) * PEER_ROWS, :]
            va, vb = _unpack_pair(buf[slot, rows, half:2 * half])
            acc_a = acc_a + a * va
            acc_b = acc_b + a * vb
        gt = gt_ref[0]
        o_ref[0, pl.ds(j, 1), 0:half] = (x_ref[0, pl.ds(j, 1), 0:half]
                                         + gt[:, 0:half] * jnp.sum(acc_a, axis=0, keepdims=True))
        o_ref[0, pl.ds(j, 1), half:2 * half] = (x_ref[0, pl.ds(j, 1), half:2 * half]
                                                + gt[:, half:2 * half] * jnp.sum(acc_b, axis=0, keepdims=True))

    for j0 in range(depth):
        issue(j0, j0 % PEER_SLOTS, range(PEER_NSEL))

    def body(it, carry):
        for s in range(PEER_SLOTS):
            wait(s)
            compute(it * PEER_SLOTS + s, s, True)
        return carry

    n_main = (tokens - depth) // PEER_SLOTS
    lax.fori_loop(0, n_main, body, 0)
    for j in range(n_main * PEER_SLOTS, tokens):
        wait(j % PEER_SLOTS)
        compute(j, j % PEER_SLOTS, j + depth < tokens)


def peer_experts(idx, gate, h, x, gt, uv):
    bsz, s, dm = x.shape
    t = bsz * s
    tt = min(PEER_TOKENS, s)
    nt = s // tt
    per_batch = gt.shape[0] == bsz and bsz > 1
    g_map = (lambda b, i: (b, 0, 0)) if per_batch else (lambda b, i: (0, 0, 0))
    out = pl.pallas_call(
        functools.partial(_peer_expert_kernel, tokens=tt),
        grid=(bsz, nt),
        in_specs=[pl.BlockSpec((PEER_NSEL, tt), lambda b, i: (0, b * nt + i), memory_space=pltpu.SMEM),
                  pl.BlockSpec((PEER_NSEL, tt), lambda b, i: (0, b * nt + i)),
                  pl.BlockSpec((tt, dm), lambda b, i: (b * nt + i, 0)),
                  pl.BlockSpec((1, tt, dm), lambda b, i: (b, i, 0)),
                  pl.BlockSpec((1, 1, dm), g_map),
                  pl.BlockSpec(memory_space=pl.ANY)],
        out_specs=pl.BlockSpec((1, tt, dm), lambda b, i: (b, i, 0)),
        out_shape=jax.ShapeDtypeStruct((bsz, s, dm), F32),
        scratch_shapes=[pltpu.VMEM((PEER_SLOTS, PEER_NSEL, dm), jnp.uint32),
                        pltpu.SemaphoreType.DMA((PEER_SLOTS,))],
        compiler_params=pltpu.CompilerParams(dimension_semantics=("arbitrary", "arbitrary"),
                                             vmem_limit_bytes=VMEM_LIMIT),
        name="peer_experts",
    )(idx, gate, h, x.reshape(bsz, s, dm), gt, uv)
    return out


def peer_sublayer(x, g_norm, shift, scale, gt, wq, k1, k2, uv):
    bsz, s, dm = x.shape
    h_bf, h_f32 = norm_modulate(x, g_norm, shift, scale, (BF16, F32))
    q = matmul(h_bf.reshape(bsz * s, dm), wq, F32)
    idx, gate = peer_retrieve(q, k1, k2)
    return peer_experts(idx, gate, h_f32.reshape(bsz * s, dm), x, gt, uv)


def _head_major(t, n_heads):
    bsz, s, _ = t.shape
    return t.reshape(bsz, s, n_heads, -1).transpose(0, 2, 1, 3)


_PROJ_GROUPS = dict(z=(0, BF16), xbc=(1, F32), na=(3, BF16), gla_q=(4, BF16), gla_k=(5, BF16), gla_v=(6, BF16),
                    gla_r=(7, BF16), swa_q=(9, BF16), swa_k=(10, BF16), swa_v=(11, BF16), gates=(12, BF16))


def _split_w_in(w):
    o = IN_OFFS
    ws = {name: w[:, o[i]:o[i + 1]].astype(BF16) for name, (i, _) in _PROJ_GROUPS.items()}
    small = jnp.concatenate([w[:, o[2]:o[3]], w[:, o[8]:o[9]]], axis=1)
    ws['small'] = jnp.pad(small, ((0, 0), (0, V7X_LANES - small.shape[1]))).astype(BF16)
    return ws


def _project(h, ws):
    bsz, s, dm = h.shape
    hf = h.reshape(bsz * s, dm)
    out = {name: matmul(hf, ws[name], dt).reshape(bsz, s, -1) for name, (_, dt) in _PROJ_GROUPS.items()}
    sm = matmul(hf, ws['small'], F32).reshape(bsz, s, V7X_LANES)
    out['dt'] = sm[:, :, :2 * SSD_HEADS]
    out['glr'] = sm[:, :, 2 * SSD_HEADS:2 * SSD_HEADS + 2 * GLA_GATE_RANK]
    return out


def _flat_heads(t):
    return t.reshape((-1,) + t.shape[-2:])


def mixer_sublayer(hx, hc, p, rope, need_ctx):
    bsz, s, _ = hx.shape
    lc = hc.shape[1]
    ws = _split_w_in(p['w_in'])
    px = _project(hx, ws)
    pc = _project(hc, ws)
    zeros_h = jnp.zeros((2, bsz, SSD_HEADS, SSD_STATE, SSD_HEADDIM), F32)
    zero_s = jnp.zeros((bsz, GLA_HEADS, GLA_HEAD_K, GLA_HEAD_V), F32)
    zeros_s = (zero_s, zero_s)

    act_c = conv_silu(pc['xbc'], p['ssd_conv_w'], p['ssd_conv_b'])
    act_x = conv_silu(px['xbc'], p['ssd_conv_w'], p['ssd_conv_b'])
    y_c, h_c = ssd_scan(act_c, pc['dt'], p['ssd_a_log'], p['ssd_dt_bias'], zeros_h)
    y_x, _ = ssd_scan(act_x, px['dt'], p['ssd_a_log'], p['ssd_dt_bias'], h_c)
    a_x = ssd_finish(y_x, act_x, px['z'], p['ssd_d'], p['ssd_norm_g'])

    dh = NA_HEAD_DIM
    scale = dh ** -0.5

    def na_heads(t):
        t = _head_major(t, 3 * NA_HEADS)
        q = head_prep(_flat_heads(t[:, :NA_HEADS]), p['na_q_norm'], None, scale).reshape(t[:, :NA_HEADS].shape)
        k = head_prep(_flat_heads(t[:, NA_HEADS:2 * NA_HEADS]), p['na_k_norm'], None, 1.0).reshape(q.shape)
        return q, k, t[:, 2 * NA_HEADS:]

    nq, nk, nv = na_heads(px['na'])
    nqc, nkc, nvc = na_heads(pc['na'])
    b_x = na_attention(nq, nk, nv, nkc, nvc, na_bias_table(p['na_rpb']))
    b_x = b_x.transpose(0, 2, 1, 3).reshape(bsz, s, BRANCH_WIDTH)

    o_c, s_c = gla_scan(pc['gla_q'], pc['gla_k'], pc['gla_v'], pc['glr'], p['gla_w_gate'], p['gla_b_gate'], zeros_s)
    o_x, _ = gla_scan(px['gla_q'], px['gla_k'], px['gla_v'], px['glr'], p['gla_w_gate'], p['gla_b_gate'], s_c)
    c_x = gla_finish(o_x, px['gla_r'], p['gla_norm_g'])

    def swa_heads(q, k, v, tabs):
        qh = _head_major(q, SWA_HEADS)
        kh = _head_major(k, SWA_KV_HEADS)
        qn = head_prep(_flat_heads(qh), p['swa_q_norm'], tabs, SWA_HEAD_DIM ** -0.5).reshape(qh.shape)
        kn = head_prep(_flat_heads(kh), p['swa_k_norm'], tabs, 1.0).reshape(kh.shape)
        qn = qn.reshape(qh.shape[0], SWA_KV_HEADS, SWA_GROUP, qh.shape[2], SWA_HEAD_DIM)
        return qn, kn, _head_major(v, SWA_KV_HEADS)

    sq, sk, sv = swa_heads(px['swa_q'], px['swa_k'], px['swa_v'], rope)
    sqc, skc, svc = swa_heads(pc['swa_q'], pc['swa_k'], pc['swa_v'], None)
    d_x = swa_attention(sq, sk, sv, skc, svc, p['swa_sink'])
    d_x = d_x.reshape(bsz, SWA_HEADS, s, SWA_HEAD_DIM).transpose(0, 2, 1, 3).reshape(bsz, s, BRANCH_WIDTH)

    wb = p['w_branch'].astype(BF16)
    flat = lambda t: t.reshape(-1, t.shape[-1])
    m_x = merge_branches([flat(a_x), flat(b_x), flat(c_x), flat(d_x)], flat(px['gates']), wb)
    if not need_ctx:
        return m_x, None

    a_c = ssd_finish(y_c, act_c, pc['z'], p['ssd_d'], p['ssd_norm_g'])
    no_sink = jnp.full((NA_HEADS,), NEG, F32)
    b_c = ctx_attention(nqc[:, :, None], nkc, nvc, no_sink)[:, :, 0]
    b_c = b_c.transpose(0, 2, 1, 3).reshape(bsz, lc, BRANCH_WIDTH)
    c_c = gla_finish(o_c, pc['gla_r'], p['gla_norm_g'])
    d_c = ctx_attention(sqc, skc, svc, p['swa_sink'])
    d_c = d_c.reshape(bsz, SWA_HEADS, lc, SWA_HEAD_DIM).transpose(0, 2, 1, 3).reshape(bsz, lc, BRANCH_WIDTH)
    m_c = merge_branches([flat(a_c), flat(b_c), flat(c_c), flat(d_c)], flat(pc['gates']), wb)
    return m_x, m_c


def kernel(x, c, ctx, c_ctx, w_ada, b_ada, g_norm1, g_norm2, w_in, ssd_conv_w, ssd_conv_b, ssd_a_log, ssd_dt_bias, ssd_d, ssd_norm_g, na_q_norm, na_k_norm, na_rpb, gla_w_gate, gla_b_gate, gla_norm_g, swa_q_norm, swa_k_norm, swa_sink, w_branch, w_out, peer_wq, peer_k1, peer_k2, peer_u, peer_v):
    bsz, s, dm = x.shape
    rope = rope_tables(s)
    n_cond = 8
    cc = jnp.zeros((n_cond, dm), F32).at[:bsz].set(c).at[bsz].set(c_ctx)
    for l in range(DEPTH):
        need_ctx = l < DEPTH - 1
        mod = ada_mod(cc, w_ada[l], b_ada[l])
        mx = [mod[:bsz, i * dm:(i + 1) * dm].reshape(bsz, 1, dm) for i in range(6)]
        mc = [mod[bsz:bsz + 1, i * dm:(i + 1) * dm].reshape(1, 1, dm) for i in range(6)]
        p = dict(w_in=w_in[l], ssd_conv_w=ssd_conv_w[l], ssd_conv_b=ssd_conv_b[l], ssd_a_log=ssd_a_log[l],
                 ssd_dt_bias=ssd_dt_bias[l], ssd_d=ssd_d[l], ssd_norm_g=ssd_norm_g[l],
                 na_q_norm=na_q_norm[l], na_k_norm=na_k_norm[l], na_rpb=na_rpb[l],
                 gla_w_gate=gla_w_gate[l], gla_b_gate=gla_b_gate[l], gla_norm_g=gla_norm_g[l],
                 swa_q_norm=swa_q_norm[l], swa_k_norm=swa_k_norm[l], swa_sink=swa_sink[l],
                 w_branch=w_branch[l])
        (hx,) = norm_modulate(x, g_norm1[l], mx[0], mx[1], (BF16,))
        (hc,) = norm_modulate(ctx, g_norm1[l], mc[0], mc[1], (BF16,))
        m_x, m_c = mixer_sublayer(hx, hc, p, rope, need_ctx)
        wo = w_out[l].astype(BF16)
        wq = peer_wq[l].astype(BF16)
        uv = pack_expert_tables(peer_u[l], peer_v[l])
        x = matmul_gated_residual(m_x, wo, x, mx[2])
        x = peer_sublayer(x, g_norm2[l], mx[3], mx[4], mx[5], wq, peer_k1[l], peer_k2[l], uv)
        if need_ctx:
            ctx = matmul_gated_residual(m_c, wo, ctx, mc[2])
            ctx = peer_sublayer(ctx, g_norm2[l], mc[3], mc[4], mc[5], wq, peer_k1[l], peer_k2[l], uv)
    return x
```

```python
import functools
import math

import numpy as np
import jax
import jax.numpy as jnp
from jax import lax
from jax.experimental import pallas as pl
from jax.experimental.pallas import tpu as pltpu

D_MODEL = 2048
DEPTH = 2
GRID_W = 64
NORM_EPS = 1e-6
N_BRANCH = 4
BRANCH_WIDTH = D_MODEL // 2

SSD_HEADDIM = 64
SSD_D_INNER = BRANCH_WIDTH
SSD_HEADS = SSD_D_INNER // SSD_HEADDIM
SSD_GROUPS = 4
SSD_STATE = 128
SSD_CONV_DIM = SSD_D_INNER + 2 * SSD_GROUPS * SSD_STATE
SSD_CONV_K = 5
SSD_CHUNK = 128

NA_HEAD_DIM = 64
NA_HEADS = BRANCH_WIDTH // NA_HEAD_DIM
NA_WIN_R = 8
NA_WIN_C = 16

GLA_HEADS = 4
GLA_V_DIM = BRANCH_WIDTH
GLA_K_DIM = BRANCH_WIDTH // 2
GLA_HEAD_K = GLA_K_DIM // GLA_HEADS
GLA_HEAD_V = GLA_V_DIM // GLA_HEADS
GLA_GATE_RANK = 16
GLA_TAU = 16.0
GLA_CHUNK = 64

SWA_HEAD_DIM = 64
SWA_HEADS = BRANCH_WIDTH // SWA_HEAD_DIM
SWA_KV_HEADS = 4
SWA_GROUP = SWA_HEADS // SWA_KV_HEADS
SWA_WINDOW = 128
SWA_BLOCK = 128
ROPE_BASE = 10000.0
ROPE_AXIS_DIM = SWA_HEAD_DIM // 2

PEER_HEADS = 8
PEER_NKEYS = 128
PEER_QDIM = 256
PEER_TOPK = 16
PEER_NSEL = PEER_HEADS * PEER_TOPK

IN_SPLITS = (SSD_D_INNER, SSD_CONV_DIM, 2 * SSD_HEADS,
             3 * BRANCH_WIDTH,
             GLA_K_DIM, GLA_K_DIM, GLA_V_DIM, GLA_V_DIM, 2 * GLA_GATE_RANK,
             SWA_HEADS * SWA_HEAD_DIM, SWA_KV_HEADS * SWA_HEAD_DIM, SWA_KV_HEADS * SWA_HEAD_DIM,
             N_BRANCH * D_MODEL)
IN_OFFS = tuple(int(v) for v in np.cumsum((0,) + IN_SPLITS))

V7X_LANES = 128
V7X_VMEM_BYTES = 64 * 1024 * 1024
VMEM_LIMIT = 48 * 1024 * 1024

F32 = jnp.float32
BF16 = jnp.bfloat16
HI = lax.Precision.HIGHEST
NEG = -1e30


def _cparams(sem):
    return pltpu.CompilerParams(dimension_semantics=sem, vmem_limit_bytes=VMEM_LIMIT)


def _dot(a, b, precision=None):
    return jnp.dot(a, b, preferred_element_type=F32, precision=precision)


def _dot_nt(a, b, precision=None):
    return lax.dot_general(a, b, (((1,), (1,)), ((), ())), preferred_element_type=F32,
                           precision=precision)


def _silu(x):
    return x / (1.0 + jnp.exp(-x))


def _softplus(x):
    return jnp.maximum(x, 0.0) + jnp.log1p(jnp.exp(-jnp.abs(x)))


def _log_sigmoid(x):
    return jnp.minimum(x, 0.0) - jnp.log1p(jnp.exp(-jnp.abs(x)))


def _mm_kernel(a_ref, b_ref, o_ref):
    o_ref[...] = _dot(a_ref[...], b_ref[...]).astype(o_ref.dtype)


def matmul(a, b, out_dtype, tm=1024, tn=512):
    m, k = a.shape
    n = b.shape[1]
    tm = min(tm, m)
    tn = min(tn, n)
    assert m % tm == 0 and n % tn == 0, (m, n, tm, tn)
    return pl.pallas_call(
        _mm_kernel,
        grid=(m // tm, n // tn),
        in_specs=[pl.BlockSpec((tm, k), lambda i, j: (i, 0)),
                  pl.BlockSpec((k, tn), lambda i, j: (0, j))],
        out_specs=pl.BlockSpec((tm, tn), lambda i, j: (i, j)),
        out_shape=jax.ShapeDtypeStruct((m, n), out_dtype),
        compiler_params=_cparams(("parallel", "parallel")),
        name="matmul",
    )(a, b)


def _mod_kernel(c_ref, w_ref, b_ref, o_ref):
    a = _silu(c_ref[...]).astype(BF16)
    o_ref[...] = _dot(a, w_ref[...].astype(BF16)) + b_ref[...]


def ada_mod(cc, w, b, tn=1024):
    m, k = cc.shape
    n = w.shape[1]
    return pl.pallas_call(
        _mod_kernel,
        grid=(n // tn,),
        in_specs=[pl.BlockSpec((m, k), lambda j: (0, 0)),
                  pl.BlockSpec((k, tn), lambda j: (0, j)),
                  pl.BlockSpec((1, tn), lambda j: (0, j))],
        out_specs=pl.BlockSpec((m, tn), lambda j: (0, j)),
        out_shape=jax.ShapeDtypeStruct((m, n), F32),
        compiler_params=_cparams(("parallel",)),
        name="ada_mod",
    )(cc, w, b.reshape(1, n))


def _normmod_kernel(x_ref, g_ref, sh_ref, sc_ref, *o_refs):
    x = x_ref[0]
    var = jnp.mean(x * x, axis=-1, keepdims=True)
    y = x * lax.rsqrt(var + NORM_EPS) * g_ref[...]
    y = y * (1.0 + sc_ref[0]) + sh_ref[0]
    for o_ref in o_refs:
        o_ref[0] = y.astype(o_ref.dtype)


def norm_modulate(x, g, shift, scale, out_dtypes, ts=512):
    bsz, s, d = x.shape
    ts = min(ts, s)
    per_batch = shift.shape[0] == bsz and bsz > 1
    mod_map = (lambda b, i: (b, 0, 0)) if per_batch else (lambda b, i: (0, 0, 0))
    outs = pl.pallas_call(
        _normmod_kernel,
        grid=(bsz, s // ts),
        in_specs=[pl.BlockSpec((1, ts, d), lambda b, i: (b, i, 0)),
                  pl.BlockSpec((1, d), lambda b, i: (0, 0)),
                  pl.BlockSpec((1, 1, d), mod_map),
                  pl.BlockSpec((1, 1, d), mod_map)],
        out_specs=[pl.BlockSpec((1, ts, d), lambda b, i: (b, i, 0)) for _ in out_dtypes],
        out_shape=[jax.ShapeDtypeStruct((bsz, s, d), dt) for dt in out_dtypes],
        compiler_params=_cparams(("parallel", "parallel")),
        name="norm_modulate",
    )(x, g.reshape(1, d), shift, scale)
    return outs


def _mm_resid_kernel(a_ref, w_ref, x_ref, gt_ref, o_ref):
    y = _dot(a_ref[...], w_ref[...])
    o_ref[0] = x_ref[0] + gt_ref[0] * y


def matmul_gated_residual(a, w, x, gate, tm=1024, tn=512):
    bsz, s, n = x.shape
    k = a.shape[1]
    tm = min(tm, s)
    nt = s // tm
    per_batch = gate.shape[0] == bsz and bsz > 1
    g_map = (lambda b, i, j: (b, 0, j)) if per_batch else (lambda b, i, j: (0, 0, j))
    return pl.pallas_call(
        _mm_resid_kernel,
        grid=(bsz, nt, n // tn),
        in_specs=[pl.BlockSpec((tm, k), lambda b, i, j: (b * nt + i, 0)),
                  pl.BlockSpec((k, tn), lambda b, i, j: (0, j)),
                  pl.BlockSpec((1, tm, tn), lambda b, i, j: (b, i, j)),
                  pl.BlockSpec((1, 1, tn), g_map)],
        out_specs=pl.BlockSpec((1, tm, tn), lambda b, i, j: (b, i, j)),
        out_shape=jax.ShapeDtypeStruct((bsz, s, n), F32),
        compiler_params=_cparams(("parallel", "parallel", "parallel")),
        name="out_proj_residual",
    )(a, w, x, gate)


def _merge_kernel(a_ref, b_ref, c_ref, d_ref, g0_ref, g1_ref, g2_ref, g3_ref, w_ref, o_ref):
    acc = None
    for i, (br, gr) in enumerate(((a_ref, g0_ref), (b_ref, g1_ref), (c_ref, g2_ref), (d_ref, g3_ref))):
        y = _dot(br[...], w_ref[i])
        gate = 1.0 / (1.0 + jnp.exp(-gr[...].astype(F32)))
        acc = gate * y if acc is None else acc + gate * y
    o_ref[...] = acc.astype(o_ref.dtype)


def merge_branches(outs, gate_logits, w_branch, tm=512, tn=512):
    t, kb = outs[0].shape
    d = w_branch.shape[-1]
    tm = min(tm, t)
    nj = d // tn
    in_specs = [pl.BlockSpec((tm, kb), lambda i, j: (i, 0)) for _ in range(N_BRANCH)]
    in_specs += [pl.BlockSpec((tm, tn), functools.partial(lambda i, j, br: (i, br * nj + j), br=br))
                 for br in range(N_BRANCH)]
    in_specs += [pl.BlockSpec((N_BRANCH, kb, tn), lambda i, j: (0, 0, j))]
    return pl.pallas_call(
        _merge_kernel,
        grid=(t // tm, nj),
        in_specs=in_specs,
        out_specs=pl.BlockSpec((tm, tn), lambda i, j: (i, j)),
        out_shape=jax.ShapeDtypeStruct((t, d), BF16),
        compiler_params=_cparams(("parallel", "parallel")),
        name="merge_branches",
    )(*outs, gate_logits, gate_logits, gate_logits, gate_logits, w_branch)


def _headprep_kernel(x_ref, g_ref, cos_ref, sin_ref, rot_ref, o_ref, *, rope, scale):
    x = x_ref[0].astype(F32)
    var = jnp.mean(x * x, axis=-1, keepdims=True)
    y = x * lax.rsqrt(var + NORM_EPS) * g_ref[...]
    if rope:
        y = y * cos_ref[...] + _dot(y, rot_ref[...], HI) * sin_ref[...]
    o_ref[0] = (y * scale).astype(o_ref.dtype)


def head_prep(x, g, rope_tabs, scale, ts=1024):
    n, s, dh = x.shape
    ts = min(ts, s)
    rope = rope_tabs is not None
    if rope:
        cos, sin, rot = rope_tabs
    else:
        cos = sin = jnp.zeros((s, dh), F32)
        rot = jnp.zeros((dh, dh), F32)
    return pl.pallas_call(
        functools.partial(_headprep_kernel, rope=rope, scale=scale),
        grid=(n, s // ts),
        in_specs=[pl.BlockSpec((1, ts, dh), lambda h, i: (h, i, 0)),
                  pl.BlockSpec((1, dh), lambda h, i: (0, 0)),
                  pl.BlockSpec((ts, dh), lambda h, i: (i, 0)),
                  pl.BlockSpec((ts, dh), lambda h, i: (i, 0)),
                  pl.BlockSpec((dh, dh), lambda h, i: (0, 0))],
        out_specs=pl.BlockSpec((1, ts, dh), lambda h, i: (h, i, 0)),
        out_shape=jax.ShapeDtypeStruct((n, s, dh), BF16),
        compiler_params=_cparams(("parallel", "parallel")),
        name="head_prep",
    )(x, g.reshape(1, dh), cos, sin, rot)


def rope_tables(s):
    t = np.arange(s)
    row = (t // GRID_W).astype(np.float32)
    col = (t % GRID_W).astype(np.float32)
    nf = ROPE_AXIS_DIM // 2
    inv = jnp.asarray(ROPE_BASE, F32) ** (-jnp.arange(nf, dtype=F32) / nf)
    ar = jnp.asarray(row)[:, None] * inv
    ac = jnp.asarray(col)[:, None] * inv
    cos = jnp.concatenate([jnp.cos(ar), jnp.cos(ar), jnp.cos(ac), jnp.cos(ac)], axis=-1)
    sin = jnp.concatenate([jnp.sin(ar), jnp.sin(ar), jnp.sin(ac), jnp.sin(ac)], axis=-1)
    rot = np.zeros((SWA_HEAD_DIM, SWA_HEAD_DIM), np.float32)
    for d in range(SWA_HEAD_DIM):
        if d % ROPE_AXIS_DIM < nf:
            rot[d + nf, d] = -1.0
        else:
            rot[d - nf, d] = 1.0
    return cos, sin, jnp.asarray(rot)


NA_ROWS_PER_STEP = 8


def _na_kernel(q_ref, k_ref, v_ref, kc_ref, vc_ref, bias_ref, o_ref, *, n_rows):
    i = pl.program_id(2)
    kc = kc_ref[0, 0]
    vc = vc_ref[0, 0]
    n_nb = NA_WIN_R * GRID_W

    def body(rr, carry):
        r = i * NA_ROWS_PER_STEP + rr
        r0 = jnp.clip(r - NA_WIN_R // 2, 0, n_rows - NA_WIN_R)
        qoff = pl.multiple_of(rr * GRID_W, GRID_W)
        koff = pl.multiple_of(r0 * GRID_W, GRID_W)
        q = q_ref[0, 0, pl.ds(qoff, GRID_W), :]
        kn = k_ref[0, 0, pl.ds(koff, n_nb), :]
        vn = v_ref[0, 0, pl.ds(koff, n_nb), :]
        s_nb = _dot_nt(q, kn) + bias_ref[r - r0, 0]
        s_cx = _dot_nt(q, kc)
        m = jnp.maximum(jnp.max(s_nb, axis=-1, keepdims=True), jnp.max(s_cx, axis=-1, keepdims=True))
        p_nb = jnp.exp(s_nb - m)
        p_cx = jnp.exp(s_cx - m)
        den = jnp.sum(p_nb, axis=-1, keepdims=True) + jnp.sum(p_cx, axis=-1, keepdims=True)
        o = _dot(p_nb.astype(BF16), vn) + _dot(p_cx.astype(BF16), vc)
        o_ref[0, 0, pl.ds(qoff, GRID_W), :] = (o / den).astype(o_ref.dtype)
        return carry

    lax.fori_loop(0, NA_ROWS_PER_STEP, body, 0, unroll=True)


def na_bias_table(rpb):
    nh = rpb.shape[0]
    wr, wc, gw = NA_WIN_R, NA_WIN_C, GRID_W
    rpb = rpb.astype(F32)

    def toeplitz(vec, n, period):
        tiled = jnp.broadcast_to(vec[..., None, :], vec.shape[:-1] + (n, period))
        flat = tiled.reshape(vec.shape[:-1] + (n * period,))[..., :n * (period - 1)]
        return flat.reshape(vec.shape[:-1] + (n, period - 1))[..., :n]

    rows = jnp.concatenate([rpb[:, wr - 1:], jnp.zeros((nh, 1, 2 * wc - 1), F32), rpb[:, :wr - 1]], axis=1)
    tab = toeplitz(jnp.moveaxis(rows, 1, -1), wr, 2 * wr)
    tab = jnp.moveaxis(tab, 1, -1)
    fill = jnp.zeros(tab.shape[:-1] + (2 * gw - (2 * wc - 1),), F32)
    cols_ext = jnp.concatenate([tab[..., wc - 1:], fill, tab[..., :wc - 1]], axis=-1)
    tab = toeplitz(cols_ext, gw, 2 * gw)
    cols = np.arange(gw)
    c_start = np.clip(cols - wc // 2, 0, gw - wc)
    valid = (cols[None, :] >= c_start[:, None]) & (cols[None, :] < c_start[:, None] + wc)
    tab = jnp.where(jnp.asarray(valid)[None, None, None], tab, NEG)
    tab = tab.transpose(1, 0, 3, 2, 4)
    return tab.reshape(wr, nh, gw, wr * gw)


def na_attention(q, k, v, kc, vc, bias):
    bsz, h, s, dh = q.shape
    lc = kc.shape[2]
    n_rows = s // GRID_W
    tq = NA_ROWS_PER_STEP * GRID_W
    n_nb = NA_WIN_R * GRID_W
    return pl.pallas_call(
        functools.partial(_na_kernel, n_rows=n_rows),
        grid=(bsz, h, s // tq),
        in_specs=[pl.BlockSpec((1, 1, tq, dh), lambda b, hh, i: (b, hh, i, 0)),
                  pl.BlockSpec((1, 1, s, dh), lambda b, hh, i: (b, hh, 0, 0)),
                  pl.BlockSpec((1, 1, s, dh), lambda b, hh, i: (b, hh, 0, 0)),
                  pl.BlockSpec((1, 1, lc, dh), lambda b, hh, i: (b, hh, 0, 0)),
                  pl.BlockSpec((1, 1, lc, dh), lambda b, hh, i: (b, hh, 0, 0)),
                  pl.BlockSpec((NA_WIN_R, 1, GRID_W, n_nb), lambda b, hh, i: (0, hh, 0, 0))],
        out_specs=pl.BlockSpec((1, 1, tq, dh), lambda b, hh, i: (b, hh, i, 0)),
        out_shape=jax.ShapeDtypeStruct((bsz, h, s, dh), BF16),
        compiler_params=_cparams(("parallel", "parallel", "arbitrary")),
        name="na_attention",
    )(q, k, v, kc, vc, bias)


def _swa_kernel(sink_ref, q_ref, k_ref, v_ref, kc_ref, vc_ref, o_ref, *, seq):
    kh = pl.program_id(1)
    n = pl.program_id(2)
    span = 3 * SWA_BLOCK
    start = pl.multiple_of(jnp.clip((n - 1) * SWA_BLOCK, 0, seq - span), SWA_BLOCK)
    kw = k_ref[0, 0, pl.ds(start, span), :]
    vw = v_ref[0, 0, pl.ds(start, span), :]
    kc = kc_ref[0, 0]
    vc = vc_ref[0, 0]
    qpos = n * SWA_BLOCK + lax.broadcasted_iota(jnp.int32, (SWA_BLOCK, span), 0)
    kpos = start + lax.broadcasted_iota(jnp.int32, (SWA_BLOCK, span), 1)
    valid = jnp.abs(qpos - kpos) <= SWA_WINDOW
    for g in range(SWA_GROUP):
        q = q_ref[0, 0, g]
        s_loc = jnp.where(valid, _dot_nt(q, kw), NEG)
        s_ctx = _dot_nt(q, kc)
        sink = sink_ref[kh * SWA_GROUP + g]
        m = jnp.maximum(jnp.max(s_loc, axis=-1, keepdims=True), jnp.max(s_ctx, axis=-1, keepdims=True))
        m = jnp.maximum(m, sink)
        p_loc = jnp.exp(s_loc - m)
        p_ctx = jnp.exp(s_ctx - m)
        den = (jnp.sum(p_loc, axis=-1, keepdims=True) + jnp.sum(p_ctx, axis=-1, keepdims=True)
               + jnp.exp(sink - m))
        o = _dot(p_loc.astype(BF16), vw) + _dot(p_ctx.astype(BF16), vc)
        o_ref[0, 0, g] = (o / den).astype(o_ref.dtype)


def swa_attention(q, k, v, kc, vc, sink):
    bsz, hk, grp, s, dh = q.shape
    lc = kc.shape[2]
    assert s >= 3 * SWA_BLOCK
    return pl.pallas_call(
        functools.partial(_swa_kernel, seq=s),
        grid=(bsz, hk, s // SWA_BLOCK),
        in_specs=[pl.BlockSpec(memory_space=pltpu.SMEM),
                  pl.BlockSpec((1, 1, grp, SWA_BLOCK, dh), lambda b, h, n: (b, h, 0, n, 0)),
                  pl.BlockSpec((1, 1, s, dh), lambda b, h, n: (b, h, 0, 0)),
                  pl.BlockSpec((1, 1, s, dh), lambda b, h, n: (b, h, 0, 0)),
                  pl.BlockSpec((1, 1, lc, dh), lambda b, h, n: (b, h, 0, 0)),
                  pl.BlockSpec((1, 1, lc, dh), lambda b, h, n: (b, h, 0, 0))],
        out_specs=pl.BlockSpec((1, 1, grp, SWA_BLOCK, dh), lambda b, h, n: (b, h, 0, n, 0)),
        out_shape=jax.ShapeDtypeStruct((bsz, hk, grp, s, dh), BF16),
        compiler_params=_cparams(("parallel", "parallel", "arbitrary")),
        name="swa_attention",
    )(sink.astype(F32), q, k, v, kc, vc)


def _ctx_attn_kernel(sink_ref, q_ref, k_ref, v_ref, o_ref, *, grp):
    kh = pl.program_id(1)
    k = k_ref[0, 0]
    v = v_ref[0, 0]
    for g in range(grp):
        q = q_ref[0, 0, g]
        s = _dot_nt(q, k)
        sink = sink_ref[kh * grp + g]
        m = jnp.maximum(jnp.max(s, axis=-1, keepdims=True), sink)
        p = jnp.exp(s - m)
        den = jnp.sum(p, axis=-1, keepdims=True) + jnp.exp(sink - m)
        o_ref[0, 0, g] = (_dot(p.astype(BF16), v) / den).astype(o_ref.dtype)


def ctx_attention(q, k, v, sink):
    bsz, hk, grp, n, dh = q.shape
    return pl.pallas_call(
        functools.partial(_ctx_attn_kernel, grp=grp),
        grid=(bsz, hk),
        in_specs=[pl.BlockSpec(memory_space=pltpu.SMEM),
                  pl.BlockSpec((1, 1, grp, n, dh), lambda b, h: (b, h, 0, 0, 0)),
                  pl.BlockSpec((1, 1, n, dh), lambda b, h: (b, h, 0, 0)),
                  pl.BlockSpec((1, 1, n, dh), lambda b, h: (b, h, 0, 0))],
        out_specs=pl.BlockSpec((1, 1, grp, n, dh), lambda b, h: (b, h, 0, 0, 0)),
        out_shape=jax.ShapeDtypeStruct((bsz, hk, grp, n, dh), BF16),
        compiler_params=_cparams(("parallel", "parallel")),
        name="ctx_attention",
    )(sink.astype(F32), q, k, v)


CONV_PAD = 8
CONV_ROWS = 256
CONV_COLS = 256


def _conv_kernel(x_ref, w_ref, b_ref, o_ref, *, seq):
    rows = min(CONV_ROWS, seq)
    n = seq // rows

    def body(i, carry):
        base = pl.multiple_of(i * rows, rows)
        acc = jnp.zeros((rows, CONV_COLS), F32) + b_ref[...]
        prev_off = pl.multiple_of(jnp.maximum(base - CONV_PAD, 0), CONV_PAD)
        next_off = pl.multiple_of(jnp.minimum(base + rows, seq - CONV_PAD), CONV_PAD)
        prev = jnp.where(i > 0, x_ref[0, pl.ds(prev_off, CONV_PAD), :], 0.0)
        nxt = jnp.where(i < n - 1, x_ref[0, pl.ds(next_off, CONV_PAD), :], 0.0)
        halo = jnp.concatenate([prev, x_ref[0, pl.ds(base, rows), :], nxt], axis=0)
        for k in range(SSD_CONV_K):
            off = CONV_PAD + k - SSD_CONV_K // 2
            acc = acc + w_ref[k:k + 1, :] * halo[off:off + rows, :]
        o_ref[0, pl.ds(base, rows), :] = _silu(acc).astype(o_ref.dtype)
        return carry

    lax.fori_loop(0, seq // rows, body, 0)


def conv_silu(xbc, w, b):
    bsz, s, c = xbc.shape
    return pl.pallas_call(
        functools.partial(_conv_kernel, seq=s),
        grid=(bsz, c // CONV_COLS),
        in_specs=[pl.BlockSpec((1, s, CONV_COLS), lambda bb, j: (bb, 0, j)),
                  pl.BlockSpec((SSD_CONV_K, CONV_COLS), lambda bb, j: (0, j)),
                  pl.BlockSpec((1, CONV_COLS), lambda bb, j: (0, j))],
        out_specs=pl.BlockSpec((1, s, CONV_COLS), lambda bb, j: (bb, 0, j)),
        out_shape=jax.ShapeDtypeStruct((bsz, s, c), BF16),
        compiler_params=_cparams(("parallel", "parallel")),
        name="conv_silu",
    )(xbc, w.astype(F32), b.reshape(1, c).astype(F32))


def _scan_masks(length, d):
    row = lax.broadcasted_iota(jnp.int32, (length, length), 0)
    col = lax.broadcasted_iota(jnp.int32, (length, length), 1)
    return (row - col) * (1 - 2 * d) >= 0


def _ssd_kernel(xs_ref, bm_ref, cm_ref, bmt_ref, dt_ref, dtt_ref, alog_ref, alogt_ref, dtb_ref, dtbt_ref,
                h0_ref, y_ref, hout_ref, state_ref, *, n_chunks):
    d = pl.program_id(0)
    c = pl.program_id(2)

    @pl.when(c == 0)
    def _():
        state_ref[...] = h0_ref[0, 0]

    length = SSD_CHUNK
    incl = _scan_masks(length, d)
    tri = incl.astype(F32)
    dt = _softplus(dt_ref[0, 0] + dtb_ref[0])
    dtt = _softplus(dtt_ref[0, 0] + dtbt_ref[0])
    ad = dt * (-jnp.exp(alog_ref[0]))
    adt = dtt * (-jnp.exp(alogt_ref[0]))
    acs = _dot(tri, ad, HI)
    acst = _dot_nt(adt, tri, HI)
    tot = jnp.sum(ad, axis=0, keepdims=True)
    tott = jnp.sum(adt, axis=1, keepdims=True)
    rep = SSD_HEADS // SSD_GROUPS
    for gi in range(SSD_GROUPS):
        gsl = slice(gi * SSD_STATE, (gi + 1) * SSD_STATE)
        cg = cm_ref[0, :, gsl]
        bg = bm_ref[0, :, gsl]
        bgt = bmt_ref[0, gsl, :].astype(F32)
        cb = _dot_nt(cg, bg)
        for hh in range(rep):
            h = gi * rep + hh
            psl = slice(h * SSD_HEADDIM, (h + 1) * SSD_HEADDIM)
            a_col = acs[:, h:h + 1]
            a_row = acst[h:h + 1, :]
            lmat = jnp.exp(jnp.where(incl, a_col - a_row, NEG))
            xdt = (xs_ref[0, :, psl].astype(F32) * dt[:, h:h + 1]).astype(BF16)
            st = state_ref[h]
            y = _dot((cb * lmat).astype(BF16), xdt)
            y = y + jnp.exp(a_col) * _dot(cg, st.astype(BF16))
            dec = jnp.exp(tott[h:h + 1, :] - a_row)
            state_ref[h] = jnp.exp(tot[:, h:h + 1]) * st + _dot((bgt * dec).astype(BF16), xdt)
            y_ref[0, 0, :, psl] = y.astype(y_ref.dtype)

    @pl.when(c == n_chunks - 1)
    def _():
        hout_ref[0, 0] = state_ref[...]


def _chunk_index(d, c, n_chunks):
    return c + d * (n_chunks - 1 - 2 * c)


def ssd_scan(xbc_act, dt_raw, a_log, dt_bias, h0):
    bsz, s, _ = xbc_act.shape
    nc = s // SSD_CHUNK
    hh = SSD_HEADS
    ng = SSD_GROUPS * SSD_STATE
    bmt = jnp.swapaxes(xbc_act[:, :, SSD_D_INNER:SSD_D_INNER + ng], 1, 2)
    dt2 = dt_raw.reshape(bsz, s, 2, hh).transpose(2, 0, 1, 3)
    dtt = dt2.transpose(0, 1, 3, 2)
    a_log = a_log.astype(F32)
    dt_bias = dt_bias.astype(F32)
    cmap = functools.partial(_chunk_index, n_chunks=nc)
    nb_x = SSD_D_INNER // ng
    return pl.pallas_call(
        functools.partial(_ssd_kernel, n_chunks=nc),
        grid=(2, bsz, nc),
        in_specs=[pl.BlockSpec((1, SSD_CHUNK, SSD_D_INNER), lambda d, b, c: (b, cmap(d, c), 0)),
                  pl.BlockSpec((1, SSD_CHUNK, ng), lambda d, b, c: (b, cmap(d, c), nb_x)),
                  pl.BlockSpec((1, SSD_CHUNK, ng), lambda d, b, c: (b, cmap(d, c), nb_x + 1)),
                  pl.BlockSpec((1, ng, SSD_CHUNK), lambda d, b, c: (b, 0, cmap(d, c))),
                  pl.BlockSpec((1, 1, SSD_CHUNK, hh), lambda d, b, c: (d, b, cmap(d, c), 0)),
                  pl.BlockSpec((1, 1, hh, SSD_CHUNK), lambda d, b, c: (d, b, 0, cmap(d, c))),
                  pl.BlockSpec((1, 1, hh), lambda d, b, c: (d, 0, 0)),
                  pl.BlockSpec((1, hh, 1), lambda d, b, c: (d, 0, 0)),
                  pl.BlockSpec((1, 1, hh), lambda d, b, c: (d, 0, 0)),
                  pl.BlockSpec((1, hh, 1), lambda d, b, c: (d, 0, 0)),
                  pl.BlockSpec((1, 1, hh, SSD_STATE, SSD_HEADDIM), lambda d, b, c: (d, b, 0, 0, 0))],
        out_specs=[pl.BlockSpec((1, 1, SSD_CHUNK, SSD_D_INNER), lambda d, b, c: (d, b, cmap(d, c), 0)),
                   pl.BlockSpec((1, 1, hh, SSD_STATE, SSD_HEADDIM), lambda d, b, c: (d, b, 0, 0, 0))],
        out_shape=[jax.ShapeDtypeStruct((2, bsz, s, SSD_D_INNER), BF16),
                   jax.ShapeDtypeStruct((2, bsz, hh, SSD_STATE, SSD_HEADDIM), F32)],
        scratch_shapes=[pltpu.VMEM((hh, SSD_STATE, SSD_HEADDIM), F32)],
        compiler_params=_cparams(("parallel", "parallel", "arbitrary")),
        name="ssd_scan",
    )(xbc_act, xbc_act, xbc_act, bmt, dt2, dtt,
      a_log.reshape(2, 1, hh), a_log.reshape(2, hh, 1), dt_bias.reshape(2, 1, hh), dt_bias.reshape(2, hh, 1), h0)


def _ssd_finish_kernel(y_ref, xs_ref, z_ref, d_ref, g_ref, o_ref):
    y = y_ref[0, 0].astype(F32) + y_ref[1, 0].astype(F32) + d_ref[...] * xs_ref[0].astype(F32)
    z = z_ref[0].astype(F32)
    u = y * _silu(z)
    var = jnp.mean(u * u, axis=-1, keepdims=True)
    o_ref[0] = (u * lax.rsqrt(var + NORM_EPS) * g_ref[...]).astype(o_ref.dtype)


def ssd_finish(y, xbc_act, z, d_skip, norm_g, ts=512):
    _, bsz, s, di = y.shape
    ts = min(ts, s)
    dvec = jnp.repeat(d_skip.astype(F32), SSD_HEADDIM).reshape(1, di)
    return pl.pallas_call(
        _ssd_finish_kernel,
        grid=(bsz, s // ts),
        in_specs=[pl.BlockSpec((2, 1, ts, di), lambda b, i: (0, b, i, 0)),
                  pl.BlockSpec((1, ts, di), lambda b, i: (b, i, 0)),
                  pl.BlockSpec((1, ts, di), lambda b, i: (b, i, 0)),
                  pl.BlockSpec((1, di), lambda b, i: (0, 0)),
                  pl.BlockSpec((1, di), lambda b, i: (0, 0))],
        out_specs=pl.BlockSpec((1, ts, di), lambda b, i: (b, i, 0)),
        out_shape=jax.ShapeDtypeStruct((bsz, s, di), BF16),
        compiler_params=_cparams(("parallel", "parallel")),
        name="ssd_finish",
    )(y, xbc_act, z, dvec, norm_g.reshape(1, di).astype(F32))


GLA_STEP = 128


def _gla_kernel(q_ref, k_ref, kt_ref, v_ref, glr_ref, glrt_ref, wg_ref, wgt_ref, bg_ref, bgt_ref,
                s0_ref, o_ref, sout_ref, state_ref, *, n_steps, reverse):
    c = pl.program_id(1)

    @pl.when(c == 0)
    def _():
        state_ref[...] = s0_ref[0]

    length = GLA_CHUNK
    incl = _scan_masks(length, int(reverse))
    tri = incl.astype(F32)
    subs = range(GLA_STEP // length)
    for sub in (reversed(subs) if reverse else subs):
        tsl = slice(sub * length, (sub + 1) * length)
        g = _log_sigmoid(_dot(glr_ref[0, tsl, :], wg_ref[...], HI) + bg_ref[...]) / GLA_TAU
        gt = _log_sigmoid(_dot(wgt_ref[...], glrt_ref[0, :, tsl], HI) + bgt_ref[...]) / GLA_TAU
        gc = _dot(tri, g, HI)
        gct = _dot_nt(gt, tri, HI)
        tott = jnp.sum(gt, axis=1, keepdims=True)
        q_in = (q_ref[0, tsl, :].astype(F32) * (GLA_HEAD_K ** -0.5) * jnp.exp(gc)).astype(BF16)
        k_in = (k_ref[0, tsl, :].astype(F32) * jnp.exp(-gc)).astype(BF16)
        k_out_t = (kt_ref[0, :, tsl].astype(F32) * jnp.exp(tott - gct)).astype(BF16)
        dec_t = jnp.exp(tott)
        for h in range(GLA_HEADS):
            ksl = slice(h * GLA_HEAD_K, (h + 1) * GLA_HEAD_K)
            vsl = slice(h * GLA_HEAD_V, (h + 1) * GLA_HEAD_V)
            qh = q_in[:, ksl]
            vh = v_ref[0, tsl, vsl]
            att = jnp.where(incl, _dot_nt(qh, k_in[:, ksl]), 0.0)
            st = state_ref[h]
            o = _dot(att.astype(BF16), vh) + _dot(qh, st.astype(BF16))
            state_ref[h] = st * dec_t[ksl, :] + _dot(k_out_t[ksl, :], vh)
            o_ref[0, tsl, vsl] = o.astype(o_ref.dtype)

    @pl.when(c == n_steps - 1)
    def _():
        sout_ref[0] = state_ref[...]


def gla_scan(q, k, v, glr, w_gate, b_gate, s0):
    bsz, s, kd = q.shape
    vd = v.shape[-1]
    ns = s // GLA_STEP
    r = GLA_GATE_RANK
    kt = jnp.swapaxes(k, 1, 2)
    w_gate = w_gate.astype(F32)
    b_gate = b_gate.astype(F32)
    outs, states = [], []
    for d in range(2):
        cmap = (lambda c: ns - 1 - c) if d else (lambda c: c)
        glr_d = glr[:, :, d * r:(d + 1) * r]
        o, st = pl.pallas_call(
            functools.partial(_gla_kernel, n_steps=ns, reverse=bool(d)),
            grid=(bsz, ns),
            in_specs=[pl.BlockSpec((1, GLA_STEP, kd), lambda b, c, cmap=cmap: (b, cmap(c), 0)),
                      pl.BlockSpec((1, GLA_STEP, kd), lambda b, c, cmap=cmap: (b, cmap(c), 0)),
                      pl.BlockSpec((1, kd, GLA_STEP), lambda b, c, cmap=cmap: (b, 0, cmap(c))),
                      pl.BlockSpec((1, GLA_STEP, vd), lambda b, c, cmap=cmap: (b, cmap(c), 0)),
                      pl.BlockSpec((1, GLA_STEP, r), lambda b, c, cmap=cmap: (b, cmap(c), 0)),
                      pl.BlockSpec((1, r, GLA_STEP), lambda b, c, cmap=cmap: (b, 0, cmap(c))),
                      pl.BlockSpec((r, kd), lambda b, c: (0, 0)),
                      pl.BlockSpec((kd, r), lambda b, c: (0, 0)),
                      pl.BlockSpec((1, kd), lambda b, c: (0, 0)),
                      pl.BlockSpec((kd, 1), lambda b, c: (0, 0)),
                      pl.BlockSpec((1, GLA_HEADS, GLA_HEAD_K, GLA_HEAD_V), lambda b, c: (b, 0, 0, 0))],
            out_specs=[pl.BlockSpec((1, GLA_STEP, vd), lambda b, c, cmap=cmap: (b, cmap(c), 0)),
                       pl.BlockSpec((1, GLA_HEADS, GLA_HEAD_K, GLA_HEAD_V), lambda b, c: (b, 0, 0, 0))],
            out_shape=[jax.ShapeDtypeStruct((bsz, s, vd), BF16),
                       jax.ShapeDtypeStruct((bsz, GLA_HEADS, GLA_HEAD_K, GLA_HEAD_V), F32)],
            scratch_shapes=[pltpu.VMEM((GLA_HEADS, GLA_HEAD_K, GLA_HEAD_V), F32)],
            compiler_params=_cparams(("parallel", "arbitrary")),
            name="gla_scan_bwd" if d else "gla_scan_fwd",
        )(q, k, kt, v, glr_d, jnp.swapaxes(glr_d, 1, 2), w_gate[d], w_gate[d].T,
          b_gate[d].reshape(1, kd), b_gate[d].reshape(kd, 1), s0[d])
        outs.append(o)
        states.append(st)
    return outs, states


def _gla_finish_kernel(of_ref, ob_ref, r_ref, g_ref, out_ref):
    o = of_ref[0].astype(F32) + ob_ref[0].astype(F32)
    r = r_ref[0].astype(F32)
    for h in range(GLA_HEADS):
        vsl = slice(h * GLA_HEAD_V, (h + 1) * GLA_HEAD_V)
        oh = o[:, vsl]
        var = jnp.mean(oh * oh, axis=-1, keepdims=True)
        y = oh * lax.rsqrt(var + NORM_EPS) * g_ref[...]
        out_ref[0, :, vsl] = (y * _silu(r[:, vsl])).astype(out_ref.dtype)


def gla_finish(o, r, norm_g, ts=512):
    bsz, s, vd = o[0].shape
    ts = min(ts, s)
    return pl.pallas_call(
        _gla_finish_kernel,
        grid=(bsz, s // ts),
        in_specs=[pl.BlockSpec((1, ts, vd), lambda b, i: (b, i, 0)),
                  pl.BlockSpec((1, ts, vd), lambda b, i: (b, i, 0)),
                  pl.BlockSpec((1, ts, vd), lambda b, i: (b, i, 0)),
                  pl.BlockSpec((1, GLA_HEAD_V), lambda b, i: (0, 0))],
        out_specs=pl.BlockSpec((1, ts, vd), lambda b, i: (b, i, 0)),
        out_shape=jax.ShapeDtypeStruct((bsz, s, vd), BF16),
        compiler_params=_cparams(("parallel", "parallel")),
        name="gla_finish",
    )(o[0], o[1], r, norm_g.reshape(1, GLA_HEAD_V).astype(F32))


PEER_SCORE_TOKENS = 128
PEER_TOKENS = 128
PEER_SLOTS = 3
PEER_ROWS = 8


def _extract_topk(s, rows, vals_ref, pos_ref):
    iota = lax.broadcasted_iota(jnp.int32, s.shape, 0).astype(F32)

    def body(j, cur):
        m = jnp.max(cur, axis=0, keepdims=True)
        pos = jnp.min(jnp.where(cur == m, iota, float(rows)), axis=0, keepdims=True)
        vals_ref[pl.ds(j, 1), :] = m
        pos_ref[pl.ds(j, 1), :] = pos
        return jnp.where(iota == pos, -jnp.inf, cur)

    lax.fori_loop(0, PEER_TOPK, body, s)


def _peer_score_kernel(q_ref, k1_ref, k2_ref, idx_ref, gate_ref, v12_ref, p12_ref, vt_ref, pt_ref):
    half = PEER_QDIM // 2
    tt = q_ref.shape[0]
    q = q_ref[...]
    s12 = jnp.concatenate([_dot_nt(k1_ref[...], q[:, :half], HI), _dot_nt(k2_ref[...], q[:, half:], HI)], axis=1)
    _extract_topk(s12, PEER_NKEYS, v12_ref, p12_ref)
    v1 = v12_ref[:, 0:tt]
    v2 = v12_ref[:, tt:2 * tt]
    p1 = p12_ref[:, 0:tt]
    p2 = p12_ref[:, tt:2 * tt]
    cand = jnp.concatenate([v1[a:a + 1, :] + v2 for a in range(PEER_TOPK)], axis=0)
    _extract_topk(cand, PEER_TOPK * PEER_TOPK, vt_ref, pt_ref)
    pos = pt_ref[...]
    ia = jnp.floor(pos * (1.0 / PEER_TOPK))
    ib = pos - ia * float(PEER_TOPK)
    e1 = jnp.zeros_like(pos)
    e2 = jnp.zeros_like(pos)
    for a in range(PEER_TOPK):
        e1 = e1 + jnp.where(ia == float(a), p1[a:a + 1, :], 0.0)
        e2 = e2 + jnp.where(ib == float(a), p2[a:a + 1, :], 0.0)
    idx_ref[...] = (e1 * float(PEER_NKEYS) + e2).astype(jnp.int32)
    top = vt_ref[...]
    p = jnp.exp(top - top[0:1, :])
    gate_ref[...] = p / jnp.sum(p, axis=0, keepdims=True)


def peer_retrieve(q, k1, k2):
    t = q.shape[0]
    tt = min(PEER_SCORE_TOKENS, t)
    kk = PEER_TOPK
    return pl.pallas_call(
        _peer_score_kernel,
        grid=(t // tt, PEER_HEADS),
        in_specs=[pl.BlockSpec((tt, PEER_QDIM), lambda i, h: (i, h)),
                  pl.BlockSpec((PEER_NKEYS, PEER_QDIM // 2), lambda i, h: (0, 0)),
                  pl.BlockSpec((PEER_NKEYS, PEER_QDIM // 2), lambda i, h: (0, 0))],
        out_specs=[pl.BlockSpec((kk, tt), lambda i, h: (h, i)),
                   pl.BlockSpec((kk, tt), lambda i, h: (h, i))],
        out_shape=[jax.ShapeDtypeStruct((PEER_NSEL, t), jnp.int32),
                   jax.ShapeDtypeStruct((PEER_NSEL, t), F32)],
        scratch_shapes=[pltpu.VMEM((kk, 2 * tt), F32), pltpu.VMEM((kk, 2 * tt), F32),
                        pltpu.VMEM((kk, tt), F32), pltpu.VMEM((kk, tt), F32)],
        compiler_params=_cparams(("parallel", "parallel")),
        name="peer_retrieve",
    )(q, k1.astype(F32), k2.astype(F32))


def _gelu(x):
    return 0.5 * x * (1.0 + lax.erf(x * (1.0 / math.sqrt(2.0))))


def pack_expert_tables(u, v):
    def pack(tab):
        bits = lax.bitcast_convert_type(tab.astype(BF16), jnp.uint16).astype(jnp.uint32)
        half = tab.shape[1] // 2
        return bits[:, :half] | (bits[:, half:] << 16)

    return jnp.concatenate([pack(u), pack(v)], axis=1)[:, None, :]


def _unpack_pair(words):
    lo = pltpu.bitcast(words << 16, F32)
    hi = pltpu.bitcast(words & jnp.uint32(0xFFFF0000), F32)
    return lo, hi


def _peer_expert_kernel(idx_ref, gate_ref, h_ref, x_ref, gt_ref, uv_hbm, o_ref, *scratch, tokens):
    bufs, sem = scratch[:PEER_SLOTS], scratch[PEER_SLOTS]
    half = D_MODEL // 2
    n_groups = PEER_NSEL // PEER_ROWS
    depth = PEER_SLOTS - 1

    def row_copy(e, slot, k):
        return pltpu.make_async_copy(uv_hbm.at[e], bufs[slot].at[pl.ds(k, 1), :], sem.at[slot])

    def issue(tok, slot, ks):
        for k in ks:
            row_copy(idx_ref[k, tok], slot, k).start(priority=k % 2)

    def wait(slot):
        pltpu.make_async_copy(uv_hbm.at[pl.ds(0, PEER_NSEL), 0, :], bufs[slot], sem.at[slot]).wait()

    lane = lax.broadcasted_iota(jnp.int32, (PEER_NSEL, tokens), 1)

    def compute(j, slot, prefetch):
        ta = h_ref[pl.ds(j, 1), 0:half]
        tb = h_ref[pl.ds(j, 1), half:2 * half]
        gcol = jnp.sum(jnp.where(lane == j, gate_ref[...], 0.0), axis=1, keepdims=True)
        acc_a = jnp.zeros((PEER_ROWS, half), F32)
        acc_b = jnp.zeros((PEER_ROWS, half), F32)
        for g in range(n_groups):
            rows = pl.ds(g * PEER_ROWS, PEER_ROWS)
            if prefetch:
                issue(j + depth, (slot + depth) % PEER_SLOTS, range(g * PEER_ROWS, (g + 1) * PEER_ROWS))
            ua, ub = _unpack_pair(bufs[slot][rows, 0:half])
            act = jnp.sum(ua * ta + ub * tb, axis=1, keepdims=True)
            a = _gelu(act) * gcol[g * PEER_ROWS:(g + 1) * PEER_ROWS, :]
            va, vb = _unpack_pair(bufs[slot][rows, half:2 * half])
            acc_a = acc_a + a * va
            acc_b = acc_b + a * vb
        gt = gt_ref[0]
        o_ref[0, pl.ds(j, 1), 0:half] = (x_ref[0, pl.ds(j, 1), 0:half]
                                         + gt[:, 0:half] * jnp.sum(acc_a, axis=0, keepdims=True))
        o_ref[0, pl.ds(j, 1), half:2 * half] = (x_ref[0, pl.ds(j, 1), half:2 * half]
                                                + gt[:, half:2 * half] * jnp.sum(acc_b, axis=0, keepdims=True))

    for j0 in range(depth):
        issue(j0, j0 % PEER_SLOTS, range(PEER_NSEL))

    def body(it, carry):
        for s in range(PEER_SLOTS):
            wait(s)
            compute(it * PEER_SLOTS + s, s, True)
        return carry

    n_main = (tokens - depth) // PEER_SLOTS
    lax.fori_loop(0, n_main, body, 0)
    for j in range(n_main * PEER_SLOTS, tokens):
        wait(j % PEER_SLOTS)
        compute(j, j % PEER_SLOTS, j + depth < tokens)


def peer_experts(idx, gate, h, x, gt, uv):
    bsz, s, dm = x.shape
    t = bsz * s
    tt = min(PEER_TOKENS, s)
    nt = s // tt
    per_batch = gt.shape[0] == bsz and bsz > 1
    g_map = (lambda b, i: (b, 0, 0)) if per_batch else (lambda b, i: (0, 0, 0))
    out = pl.pallas_call(
        functools.partial(_peer_expert_kernel, tokens=tt),
        grid=(bsz, nt),
        in_specs=[pl.BlockSpec((PEER_NSEL, tt), lambda b, i: (0, b * nt + i), memory_space=pltpu.SMEM),
                  pl.BlockSpec((PEER_NSEL, tt), lambda b, i: (0, b * nt + i)),
                  pl.BlockSpec((tt, dm), lambda b, i: (b * nt + i, 0)),
                  pl.BlockSpec((1, tt, dm), lambda b, i: (b, i, 0)),
                  pl.BlockSpec((1, 1, dm), g_map),
                  pl.BlockSpec(memory_space=pl.ANY)],
        out_specs=pl.BlockSpec((1, tt, dm), lambda b, i: (b, i, 0)),
        out_shape=jax.ShapeDtypeStruct((bsz, s, dm), F32),
        scratch_shapes=[pltpu.VMEM((PEER_NSEL, dm), jnp.uint32) for _ in range(PEER_SLOTS)]
        + [pltpu.SemaphoreType.DMA((PEER_SLOTS,))],
        compiler_params=pltpu.CompilerParams(dimension_semantics=("arbitrary", "arbitrary"),
                                             vmem_limit_bytes=VMEM_LIMIT),
        name="peer_experts",
    )(idx, gate, h, x.reshape(bsz, s, dm), gt, uv)
    return out


def peer_sublayer(x, g_norm, shift, scale, gt, wq, k1, k2, uv):
    bsz, s, dm = x.shape
    h_bf, h_f32 = norm_modulate(x, g_norm, shift, scale, (BF16, F32))
    q = matmul(h_bf.reshape(bsz * s, dm), wq, F32)
    idx, gate = peer_retrieve(q, k1, k2)
    return peer_experts(idx, gate, h_f32.reshape(bsz * s, dm), x, gt, uv)


def _head_major(t, n_heads):
    bsz, s, _ = t.shape
    return t.reshape(bsz, s, n_heads, -1).transpose(0, 2, 1, 3)


_PROJ_GROUPS = dict(z=(0, BF16), xbc=(1, F32), na=(3, BF16), gla_q=(4, BF16), gla_k=(5, BF16), gla_v=(6, BF16),
                    gla_r=(7, BF16), swa_q=(9, BF16), swa_k=(10, BF16), swa_v=(11, BF16), gates=(12, BF16))


def _split_w_in(w):
    o = IN_OFFS
    ws = {name: w[:, o[i]:o[i + 1]].astype(BF16) for name, (i, _) in _PROJ_GROUPS.items()}
    small = jnp.concatenate([w[:, o[2]:o[3]], w[:, o[8]:o[9]]], axis=1)
    ws['small'] = jnp.pad(small, ((0, 0), (0, V7X_LANES - small.shape[1]))).astype(BF16)
    return ws


def _project(h, ws):
    bsz, s, dm = h.shape
    hf = h.reshape(bsz * s, dm)
    out = {name: matmul(hf, ws[name], dt).reshape(bsz, s, -1) for name, (_, dt) in _PROJ_GROUPS.items()}
    sm = matmul(hf, ws['small'], F32).reshape(bsz, s, V7X_LANES)
    out['dt'] = sm[:, :, :2 * SSD_HEADS]
    out['glr'] = sm[:, :, 2 * SSD_HEADS:2 * SSD_HEADS + 2 * GLA_GATE_RANK]
    return out


def _flat_heads(t):
    return t.reshape((-1,) + t.shape[-2:])


def mixer_sublayer(hx, hc, p, rope, need_ctx):
    bsz, s, _ = hx.shape
    lc = hc.shape[1]
    ws = _split_w_in(p['w_in'])
    px = _project(hx, ws)
    pc = _project(hc, ws)
    zeros_h = jnp.zeros((2, bsz, SSD_HEADS, SSD_STATE, SSD_HEADDIM), F32)
    zero_s = jnp.zeros((bsz, GLA_HEADS, GLA_HEAD_K, GLA_HEAD_V), F32)
    zeros_s = (zero_s, zero_s)

    act_c = conv_silu(pc['xbc'], p['ssd_conv_w'], p['ssd_conv_b'])
    act_x = conv_silu(px['xbc'], p['ssd_conv_w'], p['ssd_conv_b'])
    y_c, h_c = ssd_scan(act_c, pc['dt'], p['ssd_a_log'], p['ssd_dt_bias'], zeros_h)
    y_x, _ = ssd_scan(act_x, px['dt'], p['ssd_a_log'], p['ssd_dt_bias'], h_c)
    a_x = ssd_finish(y_x, act_x, px['z'], p['ssd_d'], p['ssd_norm_g'])

    dh = NA_HEAD_DIM
    scale = dh ** -0.5

    def na_heads(t):
        t = _head_major(t, 3 * NA_HEADS)
        q = head_prep(_flat_heads(t[:, :NA_HEADS]), p['na_q_norm'], None, scale).reshape(t[:, :NA_HEADS].shape)
        k = head_prep(_flat_heads(t[:, NA_HEADS:2 * NA_HEADS]), p['na_k_norm'], None, 1.0).reshape(q.shape)
        return q, k, t[:, 2 * NA_HEADS:]

    nq, nk, nv = na_heads(px['na'])
    nqc, nkc, nvc = na_heads(pc['na'])
    b_x = na_attention(nq, nk, nv, nkc, nvc, na_bias_table(p['na_rpb']))
    b_x = b_x.transpose(0, 2, 1, 3).reshape(bsz, s, BRANCH_WIDTH)

    o_c, s_c = gla_scan(pc['gla_q'], pc['gla_k'], pc['gla_v'], pc['glr'], p['gla_w_gate'], p['gla_b_gate'], zeros_s)
    o_x, _ = gla_scan(px['gla_q'], px['gla_k'], px['gla_v'], px['glr'], p['gla_w_gate'], p['gla_b_gate'], s_c)
    c_x = gla_finish(o_x, px['gla_r'], p['gla_norm_g'])

    def swa_heads(q, k, v, tabs):
        qh = _head_major(q, SWA_HEADS)
        kh = _head_major(k, SWA_KV_HEADS)
        qn = head_prep(_flat_heads(qh), p['swa_q_norm'], tabs, SWA_HEAD_DIM ** -0.5).reshape(qh.shape)
        kn = head_prep(_flat_heads(kh), p['swa_k_norm'], tabs, 1.0).reshape(kh.shape)
        qn = qn.reshape(qh.shape[0], SWA_KV_HEADS, SWA_GROUP, qh.shape[2], SWA_HEAD_DIM)
        return qn, kn, _head_major(v, SWA_KV_HEADS)

    sq, sk, sv = swa_heads(px['swa_q'], px['swa_k'], px['swa_v'], rope)
    sqc, skc, svc = swa_heads(pc['swa_q'], pc['swa_k'], pc['swa_v'], None)
    d_x = swa_attention(sq, sk, sv, skc, svc, p['swa_sink'])
    d_x = d_x.reshape(bsz, SWA_HEADS, s, SWA_HEAD_DIM).transpose(0, 2, 1, 3).reshape(bsz, s, BRANCH_WIDTH)

    wb = p['w_branch'].astype(BF16)
    flat = lambda t: t.reshape(-1, t.shape[-1])
    m_x = merge_branches([flat(a_x), flat(b_x), flat(c_x), flat(d_x)], flat(px['gates']), wb)
    if not need_ctx:
        return m_x, None

    a_c = ssd_finish(y_c, act_c, pc['z'], p['ssd_d'], p['ssd_norm_g'])
    no_sink = jnp.full((NA_HEADS,), NEG, F32)
    b_c = ctx_attention(nqc[:, :, None], nkc, nvc, no_sink)[:, :, 0]
    b_c = b_c.transpose(0, 2, 1, 3).reshape(bsz, lc, BRANCH_WIDTH)
    c_c = gla_finish(o_c, pc['gla_r'], p['gla_norm_g'])
    d_c = ctx_attention(sqc, skc, svc, p['swa_sink'])
    d_c = d_c.reshape(bsz, SWA_HEADS, lc, SWA_HEAD_DIM).transpose(0, 2, 1, 3).reshape(bsz, lc, BRANCH_WIDTH)
    m_c = merge_branches([flat(a_c), flat(b_c), flat(c_c), flat(d_c)], flat(pc['gates']), wb)
    return m_x, m_c


def kernel(x, c, ctx, c_ctx, w_ada, b_ada, g_norm1, g_norm2, w_in, ssd_conv_w, ssd_conv_b, ssd_a_log, ssd_dt_bias, ssd_d, ssd_norm_g, na_q_norm, na_k_norm, na_rpb, gla_w_gate, gla_b_gate, gla_norm_g, swa_q_norm, swa_k_norm, swa_sink, w_branch, w_out, peer_wq, peer_k1, peer_k2, peer_u, peer_v):
    bsz, s, dm = x.shape
    rope = rope_tables(s)
    n_cond = 8
    cc = jnp.zeros((n_cond, dm), F32).at[:bsz].set(c).at[bsz].set(c_ctx)
    for l in range(DEPTH):
        need_ctx = l < DEPTH - 1
        mod = ada_mod(cc, w_ada[l], b_ada[l])
        mx = [mod[:bsz, i * dm:(i + 1) * dm].reshape(bsz, 1, dm) for i in range(6)]
        mc = [mod[bsz:bsz + 1, i * dm:(i + 1) * dm].reshape(1, 1, dm) for i in range(6)]
        p = dict(w_in=w_in[l], ssd_conv_w=ssd_conv_w[l], ssd_conv_b=ssd_conv_b[l], ssd_a_log=ssd_a_log[l],
                 ssd_dt_bias=ssd_dt_bias[l], ssd_d=ssd_d[l], ssd_norm_g=ssd_norm_g[l],
                 na_q_norm=na_q_norm[l], na_k_norm=na_k_norm[l], na_rpb=na_rpb[l],
                 gla_w_gate=gla_w_gate[l], gla_b_gate=gla_b_gate[l], gla_norm_g=gla_norm_g[l],
                 swa_q_norm=swa_q_norm[l], swa_k_norm=swa_k_norm[l], swa_sink=swa_sink[l],
                 w_branch=w_branch[l])
        (hx,) = norm_modulate(x, g_norm1[l], mx[0], mx[1], (BF16,))
        (hc,) = norm_modulate(ctx, g_norm1[l], mc[0], mc[1], (BF16,))
        m_x, m_c = mixer_sublayer(hx, hc, p, rope, need_ctx)
        wo = w_out[l].astype(BF16)
        wq = peer_wq[l].astype(BF16)
        uv = pack_expert_tables(peer_u[l], peer_v[l])
        x = matmul_gated_residual(m_x, wo, x, mx[2])
        x = peer_sublayer(x, g_norm2[l], mx[3], mx[4], mx[5], wq, peer_k1[l], peer_k2[l], uv)
        if need_ctx:
            ctx = matmul_gated_residual(m_c, wo, ctx, mc[2])
            ctx = peer_sublayer(ctx, g_norm2[l], mc[3], mc[4], mc[5], wq, peer_k1[l], peer_k2[l], uv)
    return x
```

```python
import functools
import math

import numpy as np
import jax
import jax.numpy as jnp
from jax import lax
from jax.experimental import pallas as pl
from jax.experimental.pallas import tpu as pltpu

D_MODEL = 2048
DEPTH = 2
GRID_W = 64
NORM_EPS = 1e-6
N_BRANCH = 4
BRANCH_WIDTH = D_MODEL // 2

SSD_HEADDIM = 64
SSD_D_INNER = BRANCH_WIDTH
SSD_HEADS = SSD_D_INNER // SSD_HEADDIM
SSD_GROUPS = 4
SSD_STATE = 128
SSD_CONV_DIM = SSD_D_INNER + 2 * SSD_GROUPS * SSD_STATE
SSD_CONV_K = 5
SSD_CHUNK = 128

NA_HEAD_DIM = 64
NA_HEADS = BRANCH_WIDTH // NA_HEAD_DIM
NA_WIN_R = 8
NA_WIN_C = 16

GLA_HEADS = 4
GLA_V_DIM = BRANCH_WIDTH
GLA_K_DIM = BRANCH_WIDTH // 2
GLA_HEAD_K = GLA_K_DIM // GLA_HEADS
GLA_HEAD_V = GLA_V_DIM // GLA_HEADS
GLA_GATE_RANK = 16
GLA_TAU = 16.0
GLA_CHUNK = 64

SWA_HEAD_DIM = 64
SWA_HEADS = BRANCH_WIDTH // SWA_HEAD_DIM
SWA_KV_HEADS = 4
SWA_GROUP = SWA_HEADS // SWA_KV_HEADS
SWA_WINDOW = 128
SWA_BLOCK = 128
ROPE_BASE = 10000.0
ROPE_AXIS_DIM = SWA_HEAD_DIM // 2

PEER_HEADS = 8
PEER_NKEYS = 128
PEER_QDIM = 256
PEER_TOPK = 16
PEER_NSEL = PEER_HEADS * PEER_TOPK

IN_SPLITS = (SSD_D_INNER, SSD_CONV_DIM, 2 * SSD_HEADS,
             3 * BRANCH_WIDTH,
             GLA_K_DIM, GLA_K_DIM, GLA_V_DIM, GLA_V_DIM, 2 * GLA_GATE_RANK,
             SWA_HEADS * SWA_HEAD_DIM, SWA_KV_HEADS * SWA_HEAD_DIM, SWA_KV_HEADS * SWA_HEAD_DIM,
             N_BRANCH * D_MODEL)
IN_OFFS = tuple(int(v) for v in np.cumsum((0,) + IN_SPLITS))

V7X_LANES = 128
V7X_VMEM_BYTES = 64 * 1024 * 1024
VMEM_LIMIT = 48 * 1024 * 1024

F32 = jnp.float32
BF16 = jnp.bfloat16
HI = lax.Precision.HIGHEST
NEG = -1e30


def _cparams(sem):
    return pltpu.CompilerParams(dimension_semantics=sem, vmem_limit_bytes=VMEM_LIMIT)


def _dot(a, b, precision=None):
    return jnp.dot(a, b, preferred_element_type=F32, precision=precision)


def _dot_nt(a, b, precision=None):
    return lax.dot_general(a, b, (((1,), (1,)), ((), ())), preferred_element_type=F32,
                           precision=precision)


def _silu(x):
    return x / (1.0 + jnp.exp(-x))


def _softplus(x):
    return jnp.maximum(x, 0.0) + jnp.log1p(jnp.exp(-jnp.abs(x)))


def _log_sigmoid(x):
    return jnp.minimum(x, 0.0) - jnp.log1p(jnp.exp(-jnp.abs(x)))


def _mm_kernel(a_ref, b_ref, o_ref):
    o_ref[...] = _dot(a_ref[...], b_ref[...]).astype(o_ref.dtype)


def matmul(a, b, out_dtype, tm=1024, tn=512):
    m, k = a.shape
    n = b.shape[1]
    tm = min(tm, m)
    tn = min(tn, n)
    assert m % tm == 0 and n % tn == 0, (m, n, tm, tn)
    return pl.pallas_call(
        _mm_kernel,
        grid=(m // tm, n // tn),
        in_specs=[pl.BlockSpec((tm, k), lambda i, j: (i, 0)),
                  pl.BlockSpec((k, tn), lambda i, j: (0, j))],
        out_specs=pl.BlockSpec((tm, tn), lambda i, j: (i, j)),
        out_shape=jax.ShapeDtypeStruct((m, n), out_dtype),
        compiler_params=_cparams(("parallel", "parallel")),
        name="matmul",
    )(a, b)


def _mod_kernel(c_ref, w_ref, b_ref, o_ref):
    a = _silu(c_ref[...]).astype(BF16)
    o_ref[...] = _dot(a, w_ref[...].astype(BF16)) + b_ref[...]


def ada_mod(cc, w, b, tn=1024):
    m, k = cc.shape
    n = w.shape[1]
    return pl.pallas_call(
        _mod_kernel,
        grid=(n // tn,),
        in_specs=[pl.BlockSpec((m, k), lambda j: (0, 0)),
                  pl.BlockSpec((k, tn), lambda j: (0, j)),
                  pl.BlockSpec((1, tn), lambda j: (0, j))],
        out_specs=pl.BlockSpec((m, tn), lambda j: (0, j)),
        out_shape=jax.ShapeDtypeStruct((m, n), F32),
        compiler_params=_cparams(("parallel",)),
        name="ada_mod",
    )(cc, w, b.reshape(1, n))


def _normmod_kernel(x_ref, g_ref, sh_ref, sc_ref, *o_refs):
    x = x_ref[0]
    var = jnp.mean(x * x, axis=-1, keepdims=True)
    y = x * lax.rsqrt(var + NORM_EPS) * g_ref[...]
    y = y * (1.0 + sc_ref[0]) + sh_ref[0]
    for o_ref in o_refs:
        o_ref[0] = y.astype(o_ref.dtype)


def norm_modulate(x, g, shift, scale, out_dtypes, ts=512):
    bsz, s, d = x.shape
    ts = min(ts, s)
    per_batch = shift.shape[0] == bsz and bsz > 1
    mod_map = (lambda b, i: (b, 0, 0)) if per_batch else (lambda b, i: (0, 0, 0))
    outs = pl.pallas_call(
        _normmod_kernel,
        grid=(bsz, s // ts),
        in_specs=[pl.BlockSpec((1, ts, d), lambda b, i: (b, i, 0)),
                  pl.BlockSpec((1, d), lambda b, i: (0, 0)),
                  pl.BlockSpec((1, 1, d), mod_map),
                  pl.BlockSpec((1, 1, d), mod_map)],
        out_specs=[pl.BlockSpec((1, ts, d), lambda b, i: (b, i, 0)) for _ in out_dtypes],
        out_shape=[jax.ShapeDtypeStruct((bsz, s, d), dt) for dt in out_dtypes],
        compiler_params=_cparams(("parallel", "parallel")),
        name="norm_modulate",
    )(x, g.reshape(1, d), shift, scale)
    return outs


def _mm_resid_kernel(a_ref, w_ref, x_ref, gt_ref, o_ref):
    y = _dot(a_ref[...], w_ref[...])
    o_ref[0] = x_ref[0] + gt_ref[0] * y


def matmul_gated_residual(a, w, x, gate, tm=1024, tn=512):
    bsz, s, n = x.shape
    k = a.shape[1]
    tm = min(tm, s)
    nt = s // tm
    per_batch = gate.shape[0] == bsz and bsz > 1
    g_map = (lambda b, i, j: (b, 0, j)) if per_batch else (lambda b, i, j: (0, 0, j))
    return pl.pallas_call(
        _mm_resid_kernel,
        grid=(bsz, nt, n // tn),
        in_specs=[pl.BlockSpec((tm, k), lambda b, i, j: (b * nt + i, 0)),
                  pl.BlockSpec((k, tn), lambda b, i, j: (0, j)),
                  pl.BlockSpec((1, tm, tn), lambda b, i, j: (b, i, j)),
                  pl.BlockSpec((1, 1, tn), g_map)],
        out_specs=pl.BlockSpec((1, tm, tn), lambda b, i, j: (b, i, j)),
        out_shape=jax.ShapeDtypeStruct((bsz, s, n), F32),
        compiler_params=_cparams(("parallel", "parallel", "parallel")),
        name="out_proj_residual",
    )(a, w, x, gate)


def _merge_kernel(a_ref, b_ref, c_ref, d_ref, g0_ref, g1_ref, g2_ref, g3_ref, w_ref, o_ref):
    acc = None
    for i, (br, gr) in enumerate(((a_ref, g0_ref), (b_ref, g1_ref), (c_ref, g2_ref), (d_ref, g3_ref))):
        y = _dot(br[...], w_ref[i])
        gate = 1.0 / (1.0 + jnp.exp(-gr[...].astype(F32)))
        acc = gate * y if acc is None else acc + gate * y
    o_ref[...] = acc.astype(o_ref.dtype)


def merge_branches(outs, gate_logits, w_branch, tm=512, tn=512):
    t, kb = outs[0].shape
    d = w_branch.shape[-1]
    tm = min(tm, t)
    nj = d // tn
    in_specs = [pl.BlockSpec((tm, kb), lambda i, j: (i, 0)) for _ in range(N_BRANCH)]
    in_specs += [pl.BlockSpec((tm, tn), functools.partial(lambda i, j, br: (i, br * nj + j), br=br))
                 for br in range(N_BRANCH)]
    in_specs += [pl.BlockSpec((N_BRANCH, kb, tn), lambda i, j: (0, 0, j))]
    return pl.pallas_call(
        _merge_kernel,
        grid=(t // tm, nj),
        in_specs=in_specs,
        out_specs=pl.BlockSpec((tm, tn), lambda i, j: (i, j)),
        out_shape=jax.ShapeDtypeStruct((t, d), BF16),
        compiler_params=_cparams(("parallel", "parallel")),
        name="merge_branches",
    )(*outs, gate_logits, gate_logits, gate_logits, gate_logits, w_branch)


def _headprep_kernel(x_ref, g_ref, cos_ref, sin_ref, rot_ref, o_ref, *, rope, scale):
    x = x_ref[0].astype(F32)
    var = jnp.mean(x * x, axis=-1, keepdims=True)
    y = x * lax.rsqrt(var + NORM_EPS) * g_ref[...]
    if rope:
        y = y * cos_ref[...] + _dot(y, rot_ref[...], HI) * sin_ref[...]
    o_ref[0] = (y * scale).astype(o_ref.dtype)


def head_prep(x, g, rope_tabs, scale, ts=1024):
    n, s, dh = x.shape
    ts = min(ts, s)
    rope = rope_tabs is not None
    if rope:
        cos, sin, rot = rope_tabs
    else:
        cos = sin = jnp.zeros((s, dh), F32)
        rot = jnp.zeros((dh, dh), F32)
    return pl.pallas_call(
        functools.partial(_headprep_kernel, rope=rope, scale=scale),
        grid=(n, s // ts),
        in_specs=[pl.BlockSpec((1, ts, dh), lambda h, i: (h, i, 0)),
                  pl.BlockSpec((1, dh), lambda h, i: (0, 0)),
                  pl.BlockSpec((ts, dh), lambda h, i: (i, 0)),
                  pl.BlockSpec((ts, dh), lambda h, i: (i, 0)),
                  pl.BlockSpec((dh, dh), lambda h, i: (0, 0))],
        out_specs=pl.BlockSpec((1, ts, dh), lambda h, i: (h, i, 0)),
        out_shape=jax.ShapeDtypeStruct((n, s, dh), BF16),
        compiler_params=_cparams(("parallel", "parallel")),
        name="head_prep",
    )(x, g.reshape(1, dh), cos, sin, rot)


def rope_tables(s):
    t = np.arange(s)
    row = (t // GRID_W).astype(np.float32)
    col = (t % GRID_W).astype(np.float32)
    nf = ROPE_AXIS_DIM // 2
    inv = jnp.asarray(ROPE_BASE, F32) ** (-jnp.arange(nf, dtype=F32) / nf)
    ar = jnp.asarray(row)[:, None] * inv
    ac = jnp.asarray(col)[:, None] * inv
    cos = jnp.concatenate([jnp.cos(ar), jnp.cos(ar), jnp.cos(ac), jnp.cos(ac)], axis=-1)
    sin = jnp.concatenate([jnp.sin(ar), jnp.sin(ar), jnp.sin(ac), jnp.sin(ac)], axis=-1)
    rot = np.zeros((SWA_HEAD_DIM, SWA_HEAD_DIM), np.float32)
    for d in range(SWA_HEAD_DIM):
        if d % ROPE_AXIS_DIM < nf:
            rot[d + nf, d] = -1.0
        else:
            rot[d - nf, d] = 1.0
    return cos, sin, jnp.asarray(rot)


NA_ROWS_PER_STEP = 8


def _na_kernel(q_ref, k_ref, v_ref, kc_ref, vc_ref, bias_ref, o_ref, *, n_rows):
    i = pl.program_id(2)
    kc = kc_ref[0, 0]
    vc = vc_ref[0, 0]
    n_nb = NA_WIN_R * GRID_W

    def body(rr, carry):
        r = i * NA_ROWS_PER_STEP + rr
        r0 = jnp.clip(r - NA_WIN_R // 2, 0, n_rows - NA_WIN_R)
        qoff = pl.multiple_of(rr * GRID_W, GRID_W)
        koff = pl.multiple_of(r0 * GRID_W, GRID_W)
        q = q_ref[0, 0, pl.ds(qoff, GRID_W), :]
        kn = k_ref[0, 0, pl.ds(koff, n_nb), :]
        vn = v_ref[0, 0, pl.ds(koff, n_nb), :]
        s_nb = _dot_nt(q, kn) + bias_ref[r - r0, 0]
        s_cx = _dot_nt(q, kc)
        m = jnp.maximum(jnp.max(s_nb, axis=-1, keepdims=True), jnp.max(s_cx, axis=-1, keepdims=True))
        p_nb = jnp.exp(s_nb - m)
        p_cx = jnp.exp(s_cx - m)
        den = jnp.sum(p_nb, axis=-1, keepdims=True) + jnp.sum(p_cx, axis=-1, keepdims=True)
        o = _dot(p_nb.astype(BF16), vn) + _dot(p_cx.astype(BF16), vc)
        o_ref[0, 0, pl.ds(qoff, GRID_W), :] = (o / den).astype(o_ref.dtype)
        return carry

    lax.fori_loop(0, NA_ROWS_PER_STEP, body, 0, unroll=True)


def na_bias_table(rpb):
    nh = rpb.shape[0]
    wr, wc, gw = NA_WIN_R, NA_WIN_C, GRID_W
    rpb = rpb.astype(F32)

    def toeplitz(vec, n, period):
        tiled = jnp.broadcast_to(vec[..., None, :], vec.shape[:-1] + (n, period))
        flat = tiled.reshape(vec.shape[:-1] + (n * period,))[..., :n * (period - 1)]
        return flat.reshape(vec.shape[:-1] + (n, period - 1))[..., :n]

    rows = jnp.concatenate([rpb[:, wr - 1:], jnp.zeros((nh, 1, 2 * wc - 1), F32), rpb[:, :wr - 1]], axis=1)
    tab = toeplitz(jnp.moveaxis(rows, 1, -1), wr, 2 * wr)
    tab = jnp.moveaxis(tab, 1, -1)
    fill = jnp.zeros(tab.shape[:-1] + (2 * gw - (2 * wc - 1),), F32)
    cols_ext = jnp.concatenate([tab[..., wc - 1:], fill, tab[..., :wc - 1]], axis=-1)
    tab = toeplitz(cols_ext, gw, 2 * gw)
    cols = np.arange(gw)
    c_start = np.clip(cols - wc // 2, 0, gw - wc)
    valid = (cols[None, :] >= c_start[:, None]) & (cols[None, :] < c_start[:, None] + wc)
    tab = jnp.where(jnp.asarray(valid)[None, None, None], tab, NEG)
    tab = tab.transpose(1, 0, 3, 2, 4)
    return tab.reshape(wr, nh, gw, wr * gw)


def na_attention(q, k, v, kc, vc, bias):
    bsz, h, s, dh = q.shape
    lc = kc.shape[2]
    n_rows = s // GRID_W
    tq = NA_ROWS_PER_STEP * GRID_W
    n_nb = NA_WIN_R * GRID_W
    return pl.pallas_call(
        functools.partial(_na_kernel, n_rows=n_rows),
        grid=(bsz, h, s // tq),
        in_specs=[pl.BlockSpec((1, 1, tq, dh), lambda b, hh, i: (b, hh, i, 0)),
                  pl.BlockSpec((1, 1, s, dh), lambda b, hh, i: (b, hh, 0, 0)),
                  pl.BlockSpec((1, 1, s, dh), lambda b, hh, i: (b, hh, 0, 0)),
                  pl.BlockSpec((1, 1, lc, dh), lambda b, hh, i: (b, hh, 0, 0)),
                  pl.BlockSpec((1, 1, lc, dh), lambda b, hh, i: (b, hh, 0, 0)),
                  pl.BlockSpec((NA_WIN_R, 1, GRID_W, n_nb), lambda b, hh, i: (0, hh, 0, 0))],
        out_specs=pl.BlockSpec((1, 1, tq, dh), lambda b, hh, i: (b, hh, i, 0)),
        out_shape=jax.ShapeDtypeStruct((bsz, h, s, dh), BF16),
        compiler_params=_cparams(("parallel", "parallel", "arbitrary")),
        name="na_attention",
    )(q, k, v, kc, vc, bias)


def _swa_kernel(sink_ref, q_ref, k_ref, v_ref, kc_ref, vc_ref, o_ref, *, seq):
    kh = pl.program_id(1)
    n = pl.program_id(2)
    span = 3 * SWA_BLOCK
    start = pl.multiple_of(jnp.clip((n - 1) * SWA_BLOCK, 0, seq - span), SWA_BLOCK)
    kw = k_ref[0, 0, pl.ds(start, span), :]
    vw = v_ref[0, 0, pl.ds(start, span), :]
    kc = kc_ref[0, 0]
    vc = vc_ref[0, 0]
    qpos = n * SWA_BLOCK + lax.broadcasted_iota(jnp.int32, (SWA_BLOCK, span), 0)
    kpos = start + lax.broadcasted_iota(jnp.int32, (SWA_BLOCK, span), 1)
    valid = jnp.abs(qpos - kpos) <= SWA_WINDOW
    for g in range(SWA_GROUP):
        q = q_ref[0, 0, g]
        s_loc = jnp.where(valid, _dot_nt(q, kw), NEG)
        s_ctx = _dot_nt(q, kc)
        sink = sink_ref[kh * SWA_GROUP + g]
        m = jnp.maximum(jnp.max(s_loc, axis=-1, keepdims=True), jnp.max(s_ctx, axis=-1, keepdims=True))
        m = jnp.maximum(m, sink)
        p_loc = jnp.exp(s_loc - m)
        p_ctx = jnp.exp(s_ctx - m)
        den = (jnp.sum(p_loc, axis=-1, keepdims=True) + jnp.sum(p_ctx, axis=-1, keepdims=True)
               + jnp.exp(sink - m))
        o = _dot(p_loc.astype(BF16), vw) + _dot(p_ctx.astype(BF16), vc)
        o_ref[0, 0, g] = (o / den).astype(o_ref.dtype)


def swa_attention(q, k, v, kc, vc, sink):
    bsz, hk, grp, s, dh = q.shape
    lc = kc.shape[2]
    assert s >= 3 * SWA_BLOCK
    return pl.pallas_call(
        functools.partial(_swa_kernel, seq=s),
        grid=(bsz, hk, s // SWA_BLOCK),
        in_specs=[pl.BlockSpec(memory_space=pltpu.SMEM),
                  pl.BlockSpec((1, 1, grp, SWA_BLOCK, dh), lambda b, h, n: (b, h, 0, n, 0)),
                  pl.BlockSpec((1, 1, s, dh), lambda b, h, n: (b, h, 0, 0)),
                  pl.BlockSpec((1, 1, s, dh), lambda b, h, n: (b, h, 0, 0)),
                  pl.BlockSpec((1, 1, lc, dh), lambda b, h, n: (b, h, 0, 0)),
                  pl.BlockSpec((1, 1, lc, dh), lambda b, h, n: (b, h, 0, 0))],
        out_specs=pl.BlockSpec((1, 1, grp, SWA_BLOCK, dh), lambda b, h, n: (b, h, 0, n, 0)),
        out_shape=jax.ShapeDtypeStruct((bsz, hk, grp, s, dh), BF16),
        compiler_params=_cparams(("parallel", "parallel", "arbitrary")),
        name="swa_attention",
    )(sink.astype(F32), q, k, v, kc, vc)


def _ctx_attn_kernel(sink_ref, q_ref, k_ref, v_ref, o_ref, *, grp):
    kh = pl.program_id(1)
    k = k_ref[0, 0]
    v = v_ref[0, 0]
    for g in range(grp):
        q = q_ref[0, 0, g]
        s = _dot_nt(q, k)
        sink = sink_ref[kh * grp + g]
        m = jnp.maximum(jnp.max(s, axis=-1, keepdims=True), sink)
        p = jnp.exp(s - m)
        den = jnp.sum(p, axis=-1, keepdims=True) + jnp.exp(sink - m)
        o_ref[0, 0, g] = (_dot(p.astype(BF16), v) / den).astype(o_ref.dtype)


def ctx_attention(q, k, v, sink):
    bsz, hk, grp, n, dh = q.shape
    return pl.pallas_call(
        functools.partial(_ctx_attn_kernel, grp=grp),
        grid=(bsz, hk),
        in_specs=[pl.BlockSpec(memory_space=pltpu.SMEM),
                  pl.BlockSpec((1, 1, grp, n, dh), lambda b, h: (b, h, 0, 0, 0)),
                  pl.BlockSpec((1, 1, n, dh), lambda b, h: (b, h, 0, 0)),
                  pl.BlockSpec((1, 1, n, dh), lambda b, h: (b, h, 0, 0))],
        out_specs=pl.BlockSpec((1, 1, grp, n, dh), lambda b, h: (b, h, 0, 0, 0)),
        out_shape=jax.ShapeDtypeStruct((bsz, hk, grp, n, dh), BF16),
        compiler_params=_cparams(("parallel", "parallel")),
        name="ctx_attention",
    )(sink.astype(F32), q, k, v)


CONV_PAD = 8
CONV_ROWS = 256
CONV_COLS = 256


def _conv_kernel(x_ref, w_ref, b_ref, o_ref, *, seq):
    rows = min(CONV_ROWS, seq)
    n = seq // rows

    def body(i, carry):
        base = pl.multiple_of(i * rows, rows)
        acc = jnp.zeros((rows, CONV_COLS), F32) + b_ref[...]
        prev_off = pl.multiple_of(jnp.maximum(base - CONV_PAD, 0), CONV_PAD)
        next_off = pl.multiple_of(jnp.minimum(base + rows, seq - CONV_PAD), CONV_PAD)
        prev = jnp.where(i > 0, x_ref[0, pl.ds(prev_off, CONV_PAD), :], 0.0)
        nxt = jnp.where(i < n - 1, x_ref[0, pl.ds(next_off, CONV_PAD), :], 0.0)
        halo = jnp.concatenate([prev, x_ref[0, pl.ds(base, rows), :], nxt], axis=0)
        for k in range(SSD_CONV_K):
            off = CONV_PAD + k - SSD_CONV_K // 2
            acc = acc + w_ref[k:k + 1, :] * halo[off:off + rows, :]
        o_ref[0, pl.ds(base, rows), :] = _silu(acc).astype(o_ref.dtype)
        return carry

    lax.fori_loop(0, seq // rows, body, 0)


def conv_silu(xbc, w, b):
    bsz, s, c = xbc.shape
    return pl.pallas_call(
        functools.partial(_conv_kernel, seq=s),
        grid=(bsz, c // CONV_COLS),
        in_specs=[pl.BlockSpec((1, s, CONV_COLS), lambda bb, j: (bb, 0, j)),
                  pl.BlockSpec((SSD_CONV_K, CONV_COLS), lambda bb, j: (0, j)),
                  pl.BlockSpec((1, CONV_COLS), lambda bb, j: (0, j))],
        out_specs=pl.BlockSpec((1, s, CONV_COLS), lambda bb, j: (bb, 0, j)),
        out_shape=jax.ShapeDtypeStruct((bsz, s, c), BF16),
        compiler_params=_cparams(("parallel", "parallel")),
        name="conv_silu",
    )(xbc, w.astype(F32), b.reshape(1, c).astype(F32))


def _scan_masks(length, d):
    row = lax.broadcasted_iota(jnp.int32, (length, length), 0)
    col = lax.broadcasted_iota(jnp.int32, (length, length), 1)
    return (row - col) * (1 - 2 * d) >= 0


def _ssd_kernel(xs_ref, bm_ref, cm_ref, bmt_ref, dt_ref, dtt_ref, alog_ref, alogt_ref, dtb_ref, dtbt_ref,
                h0_ref, y_ref, hout_ref, state_ref, *, n_chunks):
    d = pl.program_id(0)
    c = pl.program_id(2)

    @pl.when(c == 0)
    def _():
        state_ref[...] = h0_ref[0, 0]

    length = SSD_CHUNK
    incl = _scan_masks(length, d)
    tri = incl.astype(F32)
    dt = _softplus(dt_ref[0, 0] + dtb_ref[0])
    dtt = _softplus(dtt_ref[0, 0] + dtbt_ref[0])
    ad = dt * (-jnp.exp(alog_ref[0]))
    adt = dtt * (-jnp.exp(alogt_ref[0]))
    acs = _dot(tri, ad, HI)
    acst = _dot_nt(adt, tri, HI)
    tot = jnp.sum(ad, axis=0, keepdims=True)
    tott = jnp.sum(adt, axis=1, keepdims=True)
    rep = SSD_HEADS // SSD_GROUPS
    for gi in range(SSD_GROUPS):
        gsl = slice(gi * SSD_STATE, (gi + 1) * SSD_STATE)
        cg = cm_ref[0, :, gsl]
        bg = bm_ref[0, :, gsl]
        bgt = bmt_ref[0, gsl, :].astype(F32)
        cb = _dot_nt(cg, bg)
        for hh in range(rep):
            h = gi * rep + hh
            psl = slice(h * SSD_HEADDIM, (h + 1) * SSD_HEADDIM)
            a_col = acs[:, h:h + 1]
            a_row = acst[h:h + 1, :]
            lmat = jnp.exp(jnp.where(incl, a_col - a_row, NEG))
            xdt = (xs_ref[0, :, psl].astype(F32) * dt[:, h:h + 1]).astype(BF16)
            st = state_ref[h]
            y = _dot((cb * lmat).astype(BF16), xdt)
            y = y + jnp.exp(a_col) * _dot(cg, st.astype(BF16))
            dec = jnp.exp(tott[h:h + 1, :] - a_row)
            state_ref[h] = jnp.exp(tot[:, h:h + 1]) * st + _dot((bgt * dec).astype(BF16), xdt)
            y_ref[0, 0, :, psl] = y.astype(y_ref.dtype)

    @pl.when(c == n_chunks - 1)
    def _():
        hout_ref[0, 0] = state_ref[...]


def _chunk_index(d, c, n_chunks):
    return c + d * (n_chunks - 1 - 2 * c)


def ssd_scan(xbc_act, dt_raw, a_log, dt_bias, h0):
    bsz, s, _ = xbc_act.shape
    nc = s // SSD_CHUNK
    hh = SSD_HEADS
    ng = SSD_GROUPS * SSD_STATE
    bmt = jnp.swapaxes(xbc_act[:, :, SSD_D_INNER:SSD_D_INNER + ng], 1, 2)
    dt2 = dt_raw.reshape(bsz, s, 2, hh).transpose(2, 0, 1, 3)
    dtt = dt2.transpose(0, 1, 3, 2)
    a_log = a_log.astype(F32)
    dt_bias = dt_bias.astype(F32)
    cmap = functools.partial(_chunk_index, n_chunks=nc)
    nb_x = SSD_D_INNER // ng
    return pl.pallas_call(
        functools.partial(_ssd_kernel, n_chunks=nc),
        grid=(2, bsz, nc),
        in_specs=[pl.BlockSpec((1, SSD_CHUNK, SSD_D_INNER), lambda d, b, c: (b, cmap(d, c), 0)),
                  pl.BlockSpec((1, SSD_CHUNK, ng), lambda d, b, c: (b, cmap(d, c), nb_x)),
                  pl.BlockSpec((1, SSD_CHUNK, ng), lambda d, b, c: (b, cmap(d, c), nb_x + 1)),
                  pl.BlockSpec((1, ng, SSD_CHUNK), lambda d, b, c: (b, 0, cmap(d, c))),
                  pl.BlockSpec((1, 1, SSD_CHUNK, hh), lambda d, b, c: (d, b, cmap(d, c), 0)),
                  pl.BlockSpec((1, 1, hh, SSD_CHUNK), lambda d, b, c: (d, b, 0, cmap(d, c))),
                  pl.BlockSpec((1, 1, hh), lambda d, b, c: (d, 0, 0)),
                  pl.BlockSpec((1, hh, 1), lambda d, b, c: (d, 0, 0)),
                  pl.BlockSpec((1, 1, hh), lambda d, b, c: (d, 0, 0)),
                  pl.BlockSpec((1, hh, 1), lambda d, b, c: (d, 0, 0)),
                  pl.BlockSpec((1, 1, hh, SSD_STATE, SSD_HEADDIM), lambda d, b, c: (d, b, 0, 0, 0))],
        out_specs=[pl.BlockSpec((1, 1, SSD_CHUNK, SSD_D_INNER), lambda d, b, c: (d, b, cmap(d, c), 0)),
                   pl.BlockSpec((1, 1, hh, SSD_STATE, SSD_HEADDIM), lambda d, b, c: (d, b, 0, 0, 0))],
        out_shape=[jax.ShapeDtypeStruct((2, bsz, s, SSD_D_INNER), BF16),
                   jax.ShapeDtypeStruct((2, bsz, hh, SSD_STATE, SSD_HEADDIM), F32)],
        scratch_shapes=[pltpu.VMEM((hh, SSD_STATE, SSD_HEADDIM), F32)],
        compiler_params=_cparams(("parallel", "parallel", "arbitrary")),
        name="ssd_scan",
    )(xbc_act, xbc_act, xbc_act, bmt, dt2, dtt,
      a_log.reshape(2, 1, hh), a_log.reshape(2, hh, 1), dt_bias.reshape(2, 1, hh), dt_bias.reshape(2, hh, 1), h0)


def _ssd_finish_kernel(y_ref, xs_ref, z_ref, d_ref, g_ref, o_ref):
    y = y_ref[0, 0].astype(F32) + y_ref[1, 0].astype(F32) + d_ref[...] * xs_ref[0].astype(F32)
    z = z_ref[0].astype(F32)
    u = y * _silu(z)
    var = jnp.mean(u * u, axis=-1, keepdims=True)
    o_ref[0] = (u * lax.rsqrt(var + NORM_EPS) * g_ref[...]).astype(o_ref.dtype)


def ssd_finish(y, xbc_act, z, d_skip, norm_g, ts=512):
    _, bsz, s, di = y.shape
    ts = min(ts, s)
    dvec = jnp.repeat(d_skip.astype(F32), SSD_HEADDIM).reshape(1, di)
    return pl.pallas_call(
        _ssd_finish_kernel,
        grid=(bsz, s // ts),
        in_specs=[pl.BlockSpec((2, 1, ts, di), lambda b, i: (0, b, i, 0)),
                  pl.BlockSpec((1, ts, di), lambda b, i: (b, i, 0)),
                  pl.BlockSpec((1, ts, di), lambda b, i: (b, i, 0)),
                  pl.BlockSpec((1, di), lambda b, i: (0, 0)),
                  pl.BlockSpec((1, di), lambda b, i: (0, 0))],
        out_specs=pl.BlockSpec((1, ts, di), lambda b, i: (b, i, 0)),
        out_shape=jax.ShapeDtypeStruct((bsz, s, di), BF16),
        compiler_params=_cparams(("parallel", "parallel")),
        name="ssd_finish",
    )(y, xbc_act, z, dvec, norm_g.reshape(1, di).astype(F32))


GLA_STEP = 128


def _gla_kernel(q_ref, k_ref, kt_ref, v_ref, glr_ref, glrt_ref, wg_ref, wgt_ref, bg_ref, bgt_ref,
                s0_ref, o_ref, sout_ref, state_ref, *, n_steps, reverse):
    c = pl.program_id(1)

    @pl.when(c == 0)
    def _():
        state_ref[...] = s0_ref[0]

    length = GLA_CHUNK
    incl = _scan_masks(length, int(reverse))
    tri = incl.astype(F32)
    subs = range(GLA_STEP // length)
    for sub in (reversed(subs) if reverse else subs):
        tsl = slice(sub * length, (sub + 1) * length)
        g = _log_sigmoid(_dot(glr_ref[0, tsl, :], wg_ref[...], HI) + bg_ref[...]) / GLA_TAU
        gt = _log_sigmoid(_dot(wgt_ref[...], glrt_ref[0, :, tsl], HI) + bgt_ref[...]) / GLA_TAU
        gc = _dot(tri, g, HI)
        gct = _dot_nt(gt, tri, HI)
        tott = jnp.sum(gt, axis=1, keepdims=True)
        q_in = (q_ref[0, tsl, :].astype(F32) * (GLA_HEAD_K ** -0.5) * jnp.exp(gc)).astype(BF16)
        k_in = (k_ref[0, tsl, :].astype(F32) * jnp.exp(-gc)).astype(BF16)
        k_out_t = (kt_ref[0, :, tsl].astype(F32) * jnp.exp(tott - gct)).astype(BF16)
        dec_t = jnp.exp(tott)
        for h in range(GLA_HEADS):
            ksl = slice(h * GLA_HEAD_K, (h + 1) * GLA_HEAD_K)
            vsl = slice(h * GLA_HEAD_V, (h + 1) * GLA_HEAD_V)
            qh = q_in[:, ksl]
            vh = v_ref[0, tsl, vsl]
            att = jnp.where(incl, _dot_nt(qh, k_in[:, ksl]), 0.0)
            st = state_ref[h]
            o = _dot(att.astype(BF16), vh) + _dot(qh, st.astype(BF16))
            state_ref[h] = st * dec_t[ksl, :] + _dot(k_out_t[ksl, :], vh)
            o_ref[0, tsl, vsl] = o.astype(o_ref.dtype)

    @pl.when(c == n_steps - 1)
    def _():
        sout_ref[0] = state_ref[...]


def gla_scan(q, k, v, glr, w_gate, b_gate, s0):
    bsz, s, kd = q.shape
    vd = v.shape[-1]
    ns = s // GLA_STEP
    r = GLA_GATE_RANK
    kt = jnp.swapaxes(k, 1, 2)
    w_gate = w_gate.astype(F32)
    b_gate = b_gate.astype(F32)
    outs, states = [], []
    for d in range(2):
        cmap = (lambda c: ns - 1 - c) if d else (lambda c: c)
        glr_d = glr[:, :, d * r:(d + 1) * r]
        o, st = pl.pallas_call(
            functools.partial(_gla_kernel, n_steps=ns, reverse=bool(d)),
            grid=(bsz, ns),
            in_specs=[pl.BlockSpec((1, GLA_STEP, kd), lambda b, c, cmap=cmap: (b, cmap(c), 0)),
                      pl.BlockSpec((1, GLA_STEP, kd), lambda b, c, cmap=cmap: (b, cmap(c), 0)),
                      pl.BlockSpec((1, kd, GLA_STEP), lambda b, c, cmap=cmap: (b, 0, cmap(c))),
                      pl.BlockSpec((1, GLA_STEP, vd), lambda b, c, cmap=cmap: (b, cmap(c), 0)),
                      pl.BlockSpec((1, GLA_STEP, r), lambda b, c, cmap=cmap: (b, cmap(c), 0)),
                      pl.BlockSpec((1, r, GLA_STEP), lambda b, c, cmap=cmap: (b, 0, cmap(c))),
                      pl.BlockSpec((r, kd), lambda b, c: (0, 0)),
                      pl.BlockSpec((kd, r), lambda b, c: (0, 0)),
                      pl.BlockSpec((1, kd), lambda b, c: (0, 0)),
                      pl.BlockSpec((kd, 1), lambda b, c: (0, 0)),
                      pl.BlockSpec((1, GLA_HEADS, GLA_HEAD_K, GLA_HEAD_V), lambda b, c: (b, 0, 0, 0))],
            out_specs=[pl.BlockSpec((1, GLA_STEP, vd), lambda b, c, cmap=cmap: (b, cmap(c), 0)),
                       pl.BlockSpec((1, GLA_HEADS, GLA_HEAD_K, GLA_HEAD_V), lambda b, c: (b, 0, 0, 0))],
            out_shape=[jax.ShapeDtypeStruct((bsz, s, vd), BF16),
                       jax.ShapeDtypeStruct((bsz, GLA_HEADS, GLA_HEAD_K, GLA_HEAD_V), F32)],
            scratch_shapes=[pltpu.VMEM((GLA_HEADS, GLA_HEAD_K, GLA_HEAD_V), F32)],
            compiler_params=_cparams(("parallel", "arbitrary")),
            name="gla_scan_bwd" if d else "gla_scan_fwd",
        )(q, k, kt, v, glr_d, jnp.swapaxes(glr_d, 1, 2), w_gate[d], w_gate[d].T,
          b_gate[d].reshape(1, kd), b_gate[d].reshape(kd, 1), s0[d])
        outs.append(o)
        states.append(st)
    return outs, states


def _gla_finish_kernel(of_ref, ob_ref, r_ref, g_ref, out_ref):
    o = of_ref[0].astype(F32) + ob_ref[0].astype(F32)
    r = r_ref[0].astype(F32)
    for h in range(GLA_HEADS):
        vsl = slice(h * GLA_HEAD_V, (h + 1) * GLA_HEAD_V)
        oh = o[:, vsl]
        var = jnp.mean(oh * oh, axis=-1, keepdims=True)
        y = oh * lax.rsqrt(var + NORM_EPS) * g_ref[...]
        out_ref[0, :, vsl] = (y * _silu(r[:, vsl])).astype(out_ref.dtype)


def gla_finish(o, r, norm_g, ts=512):
    bsz, s, vd = o[0].shape
    ts = min(ts, s)
    return pl.pallas_call(
        _gla_finish_kernel,
        grid=(bsz, s // ts),
        in_specs=[pl.BlockSpec((1, ts, vd), lambda b, i: (b, i, 0)),
                  pl.BlockSpec((1, ts, vd), lambda b, i: (b, i, 0)),
                  pl.BlockSpec((1, ts, vd), lambda b, i: (b, i, 0)),
                  pl.BlockSpec((1, GLA_HEAD_V), lambda b, i: (0, 0))],
        out_specs=pl.BlockSpec((1, ts, vd), lambda b, i: (b, i, 0)),
        out_shape=jax.ShapeDtypeStruct((bsz, s, vd), BF16),
        compiler_params=_cparams(("parallel", "parallel")),
        name="gla_finish",
    )(o[0], o[1], r, norm_g.reshape(1, GLA_HEAD_V).astype(F32))


PEER_SCORE_TOKENS = 128
PEER_TOKENS = 128
PEER_SLOTS = 3
PEER_SLAB = 2 * (D_MODEL // 2) // V7X_LANES
PEER_PITCH = PEER_SLAB + 8


def _extract_topk(s, rows, vals_ref, pos_ref):
    iota = lax.broadcasted_iota(jnp.int32, s.shape, 0).astype(F32)

    def body(j, cur):
        m = jnp.max(cur, axis=0, keepdims=True)
        pos = jnp.min(jnp.where(cur == m, iota, float(rows)), axis=0, keepdims=True)
        vals_ref[pl.ds(j, 1), :] = m
        pos_ref[pl.ds(j, 1), :] = pos
        return jnp.where(iota == pos, -jnp.inf, cur)

    lax.fori_loop(0, PEER_TOPK, body, s)


def _peer_score_kernel(q_ref, k1_ref, k2_ref, idx_ref, gate_ref, v12_ref, p12_ref, vt_ref, pt_ref):
    half = PEER_QDIM // 2
    tt = q_ref.shape[0]
    q = q_ref[...]
    s12 = jnp.concatenate([_dot_nt(k1_ref[...], q[:, :half], HI), _dot_nt(k2_ref[...], q[:, half:], HI)], axis=1)
    _extract_topk(s12, PEER_NKEYS, v12_ref, p12_ref)
    v1 = v12_ref[:, 0:tt]
    v2 = v12_ref[:, tt:2 * tt]
    p1 = p12_ref[:, 0:tt]
    p2 = p12_ref[:, tt:2 * tt]
    cand = jnp.concatenate([v1[a:a + 1, :] + v2 for a in range(PEER_TOPK)], axis=0)
    _extract_topk(cand, PEER_TOPK * PEER_TOPK, vt_ref, pt_ref)
    pos = pt_ref[...]
    ia = jnp.floor(pos * (1.0 / PEER_TOPK))
    ib = pos - ia * float(PEER_TOPK)
    e1 = jnp.zeros_like(pos)
    e2 = jnp.zeros_like(pos)
    for a in range(PEER_TOPK):
        e1 = e1 + jnp.where(ia == float(a), p1[a:a + 1, :], 0.0)
        e2 = e2 + jnp.where(ib == float(a), p2[a:a + 1, :], 0.0)
    idx_ref[...] = (e1 * float(PEER_NKEYS) + e2).astype(jnp.int32)
    top = vt_ref[...]
    p = jnp.exp(top - top[0:1, :])
    gate_ref[...] = p / jnp.sum(p, axis=0, keepdims=True)


def peer_retrieve(q, k1, k2):
    t = q.shape[0]
    tt = min(PEER_SCORE_TOKENS, t)
    kk = PEER_TOPK
    return pl.pallas_call(
        _peer_score_kernel,
        grid=(t // tt, PEER_HEADS),
        in_specs=[pl.BlockSpec((tt, PEER_QDIM), lambda i, h: (i, h)),
                  pl.BlockSpec((PEER_NKEYS, PEER_QDIM // 2), lambda i, h: (0, 0)),
                  pl.BlockSpec((PEER_NKEYS, PEER_QDIM // 2), lambda i, h: (0, 0))],
        out_specs=[pl.BlockSpec((kk, tt), lambda i, h: (h, i)),
                   pl.BlockSpec((kk, tt), lambda i, h: (h, i))],
        out_shape=[jax.ShapeDtypeStruct((PEER_NSEL, t), jnp.int32),
                   jax.ShapeDtypeStruct((PEER_NSEL, t), F32)],
        scratch_shapes=[pltpu.VMEM((kk, 2 * tt), F32), pltpu.VMEM((kk, 2 * tt), F32),
                        pltpu.VMEM((kk, tt), F32), pltpu.VMEM((kk, tt), F32)],
        compiler_params=_cparams(("parallel", "parallel")),
        name="peer_retrieve",
    )(q, k1.astype(F32), k2.astype(F32))


def _gelu(x):
    return 0.5 * x * (1.0 + lax.erf(x * (1.0 / math.sqrt(2.0))))


def pack_expert_tables(u, v):
    def pack(tab):
        bits = lax.bitcast_convert_type(tab.astype(BF16), jnp.uint16).astype(jnp.uint32)
        half = tab.shape[1] // 2
        return bits[:, :half] | (bits[:, half:] << 16)

    n_exp = u.shape[0]
    slabs = jnp.concatenate([pack(u).reshape(n_exp, -1, V7X_LANES), pack(v).reshape(n_exp, -1, V7X_LANES)], axis=1)
    return slabs.reshape(n_exp * PEER_SLAB, V7X_LANES)


def _unpack_pair(words):
    lo = pltpu.bitcast(words << 16, F32)
    hi = pltpu.bitcast(words & jnp.uint32(0xFFFF0000), F32)
    return lo, hi


def _peer_expert_kernel(idx_ref, gate_ref, h_ref, x_ref, gt_ref, uv_hbm, o_ref, *scratch, tokens):
    bufs, sem = scratch[:PEER_SLOTS], scratch[PEER_SLOTS]
    half = D_MODEL // 2
    n_chunks = PEER_SLAB // 2
    per_step = PEER_NSEL // PEER_SLAB
    depth = PEER_SLOTS - 1

    def row_copy(e, slot, k):
        src = uv_hbm.at[pl.ds(pl.multiple_of(e * PEER_SLAB, PEER_SLAB), PEER_SLAB), :]
        return pltpu.make_async_copy(src, bufs[slot].at[pl.ds(k * PEER_PITCH, PEER_SLAB), :], sem.at[slot])

    def issue(tok, slot, ks):
        for k in ks:
            row_copy(idx_ref[k, tok], slot, k).start(priority=k % 2)

    def wait(slot):
        n = PEER_NSEL * PEER_SLAB
        pltpu.make_async_copy(uv_hbm.at[pl.ds(0, n), :], bufs[slot].at[pl.ds(0, n), :], sem.at[slot]).wait()

    lane = lax.broadcasted_iota(jnp.int32, (PEER_NSEL, tokens), 1)

    def chunk(slot, row):
        return _unpack_pair(bufs[slot][pl.ds(row, PEER_NSEL, stride=PEER_PITCH), :])

    def compute(j, slot, prefetch):
        def maybe_issue(step):
            if prefetch:
                issue(j + depth, (slot + depth) % PEER_SLOTS, range(step * per_step, (step + 1) * per_step))

        gcol = jnp.sum(jnp.where(lane == j, gate_ref[...], 0.0), axis=1, keepdims=True)
        p = jnp.zeros((PEER_NSEL, V7X_LANES), F32)
        t_lo = h_ref[pl.ds(j, 1), 0:half]
        t_hi = h_ref[pl.ds(j, 1), half:2 * half]
        for c in range(n_chunks):
            maybe_issue(c)
            ua, ub = chunk(slot, c)
            cols = slice(c * V7X_LANES, (c + 1) * V7X_LANES)
            p = p + ua * t_lo[:, cols] + ub * t_hi[:, cols]
        act = jnp.sum(p, axis=1, keepdims=True)
        a = jnp.broadcast_to(_gelu(act) * gcol, (PEER_NSEL, V7X_LANES))
        out_lo, out_hi = [], []
        for c in range(n_chunks):
            maybe_issue(n_chunks + c)
            va, vb = chunk(slot, n_chunks + c)
            out_lo.append(jnp.sum(a * va, axis=0, keepdims=True))
            out_hi.append(jnp.sum(a * vb, axis=0, keepdims=True))
        gt = gt_ref[0]
        for cols, parts in ((slice(0, half), out_lo), (slice(half, 2 * half), out_hi)):
            o_ref[0, pl.ds(j, 1), cols] = (x_ref[0, pl.ds(j, 1), cols]
                                           + gt[:, cols] * jnp.concatenate(parts, axis=1))

    for j0 in range(depth):
        issue(j0, j0 % PEER_SLOTS, range(PEER_NSEL))

    def body(it, carry):
        for s in range(PEER_SLOTS):
            wait(s)
            compute(it * PEER_SLOTS + s, s, True)
        return carry

    n_main = (tokens - depth) // PEER_SLOTS
    lax.fori_loop(0, n_main, body, 0)
    for j in range(n_main * PEER_SLOTS, tokens):
        wait(j % PEER_SLOTS)
        compute(j, j % PEER_SLOTS, j + depth < tokens)


def peer_experts(idx, gate, h, x, gt, uv):
    bsz, s, dm = x.shape
    t = bsz * s
    tt = min(PEER_TOKENS, s)
    nt = s // tt
    per_batch = gt.shape[0] == bsz and bsz > 1
    g_map = (lambda b, i: (b, 0, 0)) if per_batch else (lambda b, i: (0, 0, 0))
    out = pl.pallas_call(
        functools.partial(_peer_expert_kernel, tokens=tt),
        grid=(bsz, nt),
        in_specs=[pl.BlockSpec((PEER_NSEL, tt), lambda b, i: (0, b * nt + i), memory_space=pltpu.SMEM),
                  pl.BlockSpec((PEER_NSEL, tt), lambda b, i: (0, b * nt + i)),
                  pl.BlockSpec((tt, dm), lambda b, i: (b * nt + i, 0)),
                  pl.BlockSpec((1, tt, dm), lambda b, i: (b, i, 0)),
                  pl.BlockSpec((1, 1, dm), g_map),
                  pl.BlockSpec(memory_space=pl.ANY)],
        out_specs=pl.BlockSpec((1, tt, dm), lambda b, i: (b, i, 0)),
        out_shape=jax.ShapeDtypeStruct((bsz, s, dm), F32),
        scratch_shapes=[pltpu.VMEM((PEER_NSEL * PEER_PITCH, V7X_LANES), jnp.uint32) for _ in range(PEER_SLOTS)]
        + [pltpu.SemaphoreType.DMA((PEER_SLOTS,))],
        compiler_params=pltpu.CompilerParams(dimension_semantics=("arbitrary", "arbitrary"),
                                             vmem_limit_bytes=VMEM_LIMIT),
        name="peer_experts",
    )(idx, gate, h, x.reshape(bsz, s, dm), gt, uv)
    return out


def peer_sublayer(x, g_norm, shift, scale, gt, wq, k1, k2, uv):
    bsz, s, dm = x.shape
    h_bf, h_f32 = norm_modulate(x, g_norm, shift, scale, (BF16, F32))
    q = matmul(h_bf.reshape(bsz * s, dm), wq, F32)
    idx, gate = peer_retrieve(q, k1, k2)
    return peer_experts(idx, gate, h_f32.reshape(bsz * s, dm), x, gt, uv)


def _head_major(t, n_heads):
    bsz, s, _ = t.shape
    return t.reshape(bsz, s, n_heads, -1).transpose(0, 2, 1, 3)


_PROJ_GROUPS = dict(z=(0, BF16), xbc=(1, F32), na=(3, BF16), gla_q=(4, BF16), gla_k=(5, BF16), gla_v=(6, BF16),
                    gla_r=(7, BF16), swa_q=(9, BF16), swa_k=(10, BF16), swa_v=(11, BF16), gates=(12, BF16))


def _split_w_in(w):
    o = IN_OFFS
    ws = {name: w[:, o[i]:o[i + 1]].astype(BF16) for name, (i, _) in _PROJ_GROUPS.items()}
    small = jnp.concatenate([w[:, o[2]:o[3]], w[:, o[8]:o[9]]], axis=1)
    ws['small'] = jnp.pad(small, ((0, 0), (0, V7X_LANES - small.shape[1]))).astype(BF16)
    return ws


def _project(h, ws):
    bsz, s, dm = h.shape
    hf = h.reshape(bsz * s, dm)
    out = {name: matmul(hf, ws[name], dt).reshape(bsz, s, -1) for name, (_, dt) in _PROJ_GROUPS.items()}
    sm = matmul(hf, ws['small'], F32).reshape(bsz, s, V7X_LANES)
    out['dt'] = sm[:, :, :2 * SSD_HEADS]
    out['glr'] = sm[:, :, 2 * SSD_HEADS:2 * SSD_HEADS + 2 * GLA_GATE_RANK]
    return out


def _flat_heads(t):
    return t.reshape((-1,) + t.shape[-2:])


def mixer_sublayer(hx, hc, p, rope, need_ctx):
    bsz, s, _ = hx.shape
    lc = hc.shape[1]
    ws = _split_w_in(p['w_in'])
    px = _project(hx, ws)
    pc = _project(hc, ws)
    zeros_h = jnp.zeros((2, bsz, SSD_HEADS, SSD_STATE, SSD_HEADDIM), F32)
    zero_s = jnp.zeros((bsz, GLA_HEADS, GLA_HEAD_K, GLA_HEAD_V), F32)
    zeros_s = (zero_s, zero_s)

    act_c = conv_silu(pc['xbc'], p['ssd_conv_w'], p['ssd_conv_b'])
    act_x = conv_silu(px['xbc'], p['ssd_conv_w'], p['ssd_conv_b'])
    y_c, h_c = ssd_scan(act_c, pc['dt'], p['ssd_a_log'], p['ssd_dt_bias'], zeros_h)
    y_x, _ = ssd_scan(act_x, px['dt'], p['ssd_a_log'], p['ssd_dt_bias'], h_c)
    a_x = ssd_finish(y_x, act_x, px['z'], p['ssd_d'], p['ssd_norm_g'])

    dh = NA_HEAD_DIM
    scale = dh ** -0.5

    def na_heads(t):
        t = _head_major(t, 3 * NA_HEADS)
        q = head_prep(_flat_heads(t[:, :NA_HEADS]), p['na_q_norm'], None, scale).reshape(t[:, :NA_HEADS].shape)
        k = head_prep(_flat_heads(t[:, NA_HEADS:2 * NA_HEADS]), p['na_k_norm'], None, 1.0).reshape(q.shape)
        return q, k, t[:, 2 * NA_HEADS:]

    nq, nk, nv = na_heads(px['na'])
    nqc, nkc, nvc = na_heads(pc['na'])
    b_x = na_attention(nq, nk, nv, nkc, nvc, na_bias_table(p['na_rpb']))
    b_x = b_x.transpose(0, 2, 1, 3).reshape(bsz, s, BRANCH_WIDTH)

    o_c, s_c = gla_scan(pc['gla_q'], pc['gla_k'], pc['gla_v'], pc['glr'], p['gla_w_gate'], p['gla_b_gate'], zeros_s)
    o_x, _ = gla_scan(px['gla_q'], px['gla_k'], px['gla_v'], px['glr'], p['gla_w_gate'], p['gla_b_gate'], s_c)
    c_x = gla_finish(o_x, px['gla_r'], p['gla_norm_g'])

    def swa_heads(q, k, v, tabs):
        qh = _head_major(q, SWA_HEADS)
        kh = _head_major(k, SWA_KV_HEADS)
        qn = head_prep(_flat_heads(qh), p['swa_q_norm'], tabs, SWA_HEAD_DIM ** -0.5).reshape(qh.shape)
        kn = head_prep(_flat_heads(kh), p['swa_k_norm'], tabs, 1.0).reshape(kh.shape)
        qn = qn.reshape(qh.shape[0], SWA_KV_HEADS, SWA_GROUP, qh.shape[2], SWA_HEAD_DIM)
        return qn, kn, _head_major(v, SWA_KV_HEADS)

    sq, sk, sv = swa_heads(px['swa_q'], px['swa_k'], px['swa_v'], rope)
    sqc, skc, svc = swa_heads(pc['swa_q'], pc['swa_k'], pc['swa_v'], None)
    d_x = swa_attention(sq, sk, sv, skc, svc, p['swa_sink'])
    d_x = d_x.reshape(bsz, SWA_HEADS, s, SWA_HEAD_DIM).transpose(0, 2, 1, 3).reshape(bsz, s, BRANCH_WIDTH)

    wb = p['w_branch'].astype(BF16)
    flat = lambda t: t.reshape(-1, t.shape[-1])
    m_x = merge_branches([flat(a_x), flat(b_x), flat(c_x), flat(d_x)], flat(px['gates']), wb)
    if not need_ctx:
        return m_x, None

    a_c = ssd_finish(y_c, act_c, pc['z'], p['ssd_d'], p['ssd_norm_g'])
    no_sink = jnp.full((NA_HEADS,), NEG, F32)
    b_c = ctx_attention(nqc[:, :, None], nkc, nvc, no_sink)[:, :, 0]
    b_c = b_c.transpose(0, 2, 1, 3).reshape(bsz, lc, BRANCH_WIDTH)
    c_c = gla_finish(o_c, pc['gla_r'], p['gla_norm_g'])
    d_c = ctx_attention(sqc, skc, svc, p['swa_sink'])
    d_c = d_c.reshape(bsz, SWA_HEADS, lc, SWA_HEAD_DIM).transpose(0, 2, 1, 3).reshape(bsz, lc, BRANCH_WIDTH)
    m_c = merge_branches([flat(a_c), flat(b_c), flat(c_c), flat(d_c)], flat(pc['gates']), wb)
    return m_x, m_c


def kernel(x, c, ctx, c_ctx, w_ada, b_ada, g_norm1, g_norm2, w_in, ssd_conv_w, ssd_conv_b, ssd_a_log, ssd_dt_bias, ssd_d, ssd_norm_g, na_q_norm, na_k_norm, na_rpb, gla_w_gate, gla_b_gate, gla_norm_g, swa_q_norm, swa_k_norm, swa_sink, w_branch, w_out, peer_wq, peer_k1, peer_k2, peer_u, peer_v):
    bsz, s, dm = x.shape
    rope = rope_tables(s)
    n_cond = 8
    cc = jnp.zeros((n_cond, dm), F32).at[:bsz].set(c).at[bsz].set(c_ctx)
    for l in range(DEPTH):
        need_ctx = l < DEPTH - 1
        mod = ada_mod(cc, w_ada[l], b_ada[l])
        mx = [mod[:bsz, i * dm:(i + 1) * dm].reshape(bsz, 1, dm) for i in range(6)]
        mc = [mod[bsz:bsz + 1, i * dm:(i + 1) * dm].reshape(1, 1, dm) for i in range(6)]
        p = dict(w_in=w_in[l], ssd_conv_w=ssd_conv_w[l], ssd_conv_b=ssd_conv_b[l], ssd_a_log=ssd_a_log[l],
                 ssd_dt_bias=ssd_dt_bias[l], ssd_d=ssd_d[l], ssd_norm_g=ssd_norm_g[l],
                 na_q_norm=na_q_norm[l], na_k_norm=na_k_norm[l], na_rpb=na_rpb[l],
                 gla_w_gate=gla_w_gate[l], gla_b_gate=gla_b_gate[l], gla_norm_g=gla_norm_g[l],
                 swa_q_norm=swa_q_norm[l], swa_k_norm=swa_k_norm[l], swa_sink=swa_sink[l],
                 w_branch=w_branch[l])
        (hx,) = norm_modulate(x, g_norm1[l], mx[0], mx[1], (BF16,))
        (hc,) = norm_modulate(ctx, g_norm1[l], mc[0], mc[1], (BF16,))
        m_x, m_c = mixer_sublayer(hx, hc, p, rope, need_ctx)
        wo = w_out[l].astype(BF16)
        wq = peer_wq[l].astype(BF16)
        uv = pack_expert_tables(peer_u[l], peer_v[l])
        x = matmul_gated_residual(m_x, wo, x, mx[2])
        x = peer_sublayer(x, g_norm2[l], mx[3], mx[4], mx[5], wq, peer_k1[l], peer_k2[l], uv)
        if need_ctx:
            ctx = matmul_gated_residual(m_c, wo, ctx, mc[2])
            ctx = peer_sublayer(ctx, g_norm2[l], mc[3], mc[4], mc[5], wq, peer_k1[l], peer_k2[l], uv)
    return x
```

```python
import functools
import math

import numpy as np
import jax
import jax.numpy as jnp
from jax import lax
from jax.experimental import pallas as pl
from jax.experimental.pallas import tpu as pltpu

D_MODEL = 2048
DEPTH = 2
GRID_W = 64
NORM_EPS = 1e-6
N_BRANCH = 4
BRANCH_WIDTH = D_MODEL // 2

SSD_HEADDIM = 64
SSD_D_INNER = BRANCH_WIDTH
SSD_HEADS = SSD_D_INNER // SSD_HEADDIM
SSD_GROUPS = 4
SSD_STATE = 128
SSD_CONV_DIM = SSD_D_INNER + 2 * SSD_GROUPS * SSD_STATE
SSD_CONV_K = 5
SSD_CHUNK = 128

NA_HEAD_DIM = 64
NA_HEADS = BRANCH_WIDTH // NA_HEAD_DIM
NA_WIN_R = 8
NA_WIN_C = 16

GLA_HEADS = 4
GLA_V_DIM = BRANCH_WIDTH
GLA_K_DIM = BRANCH_WIDTH // 2
GLA_HEAD_K = GLA_K_DIM // GLA_HEADS
GLA_HEAD_V = GLA_V_DIM // GLA_HEADS
GLA_GATE_RANK = 16
GLA_TAU = 16.0
GLA_CHUNK = 64

SWA_HEAD_DIM = 64
SWA_HEADS = BRANCH_WIDTH // SWA_HEAD_DIM
SWA_KV_HEADS = 4
SWA_GROUP = SWA_HEADS // SWA_KV_HEADS
SWA_WINDOW = 128
SWA_BLOCK = 128
ROPE_BASE = 10000.0
ROPE_AXIS_DIM = SWA_HEAD_DIM // 2

PEER_HEADS = 8
PEER_NKEYS = 128
PEER_QDIM = 256
PEER_TOPK = 16
PEER_NSEL = PEER_HEADS * PEER_TOPK

IN_SPLITS = (SSD_D_INNER, SSD_CONV_DIM, 2 * SSD_HEADS,
             3 * BRANCH_WIDTH,
             GLA_K_DIM, GLA_K_DIM, GLA_V_DIM, GLA_V_DIM, 2 * GLA_GATE_RANK,
             SWA_HEADS * SWA_HEAD_DIM, SWA_KV_HEADS * SWA_HEAD_DIM, SWA_KV_HEADS * SWA_HEAD_DIM,
             N_BRANCH * D_MODEL)
IN_OFFS = tuple(int(v) for v in np.cumsum((0,) + IN_SPLITS))

V7X_LANES = 128
V7X_VMEM_BYTES = 64 * 1024 * 1024
VMEM_LIMIT = 48 * 1024 * 1024

F32 = jnp.float32
BF16 = jnp.bfloat16
HI = lax.Precision.HIGHEST
NEG = -1e30


def _cparams(sem):
    return pltpu.CompilerParams(dimension_semantics=sem, vmem_limit_bytes=VMEM_LIMIT)


def _dot(a, b, precision=None):
    return jnp.dot(a, b, preferred_element_type=F32, precision=precision)


def _dot_nt(a, b, precision=None):
    return lax.dot_general(a, b, (((1,), (1,)), ((), ())), preferred_element_type=F32,
                           precision=precision)


def _silu(x):
    return x / (1.0 + jnp.exp(-x))


def _softplus(x):
    return jnp.maximum(x, 0.0) + jnp.log1p(jnp.exp(-jnp.abs(x)))


def _log_sigmoid(x):
    return jnp.minimum(x, 0.0) - jnp.log1p(jnp.exp(-jnp.abs(x)))


def _mm_kernel(a_ref, b_ref, o_ref):
    o_ref[...] = _dot(a_ref[...], b_ref[...]).astype(o_ref.dtype)


def matmul(a, b, out_dtype, tm=1024, tn=512):
    m, k = a.shape
    n = b.shape[1]
    tm = min(tm, m)
    tn = min(tn, n)
    assert m % tm == 0 and n % tn == 0, (m, n, tm, tn)
    return pl.pallas_call(
        _mm_kernel,
        grid=(m // tm, n // tn),
        in_specs=[pl.BlockSpec((tm, k), lambda i, j: (i, 0)),
                  pl.BlockSpec((k, tn), lambda i, j: (0, j))],
        out_specs=pl.BlockSpec((tm, tn), lambda i, j: (i, j)),
        out_shape=jax.ShapeDtypeStruct((m, n), out_dtype),
        compiler_params=_cparams(("parallel", "parallel")),
        name="matmul",
    )(a, b)


def _mod_kernel(c_ref, w_ref, b_ref, o_ref):
    a = _silu(c_ref[...]).astype(BF16)
    o_ref[...] = _dot(a, w_ref[...].astype(BF16)) + b_ref[...]


def ada_mod(cc, w, b, tn=1024):
    m, k = cc.shape
    n = w.shape[1]
    return pl.pallas_call(
        _mod_kernel,
        grid=(n // tn,),
        in_specs=[pl.BlockSpec((m, k), lambda j: (0, 0)),
                  pl.BlockSpec((k, tn), lambda j: (0, j)),
                  pl.BlockSpec((1, tn), lambda j: (0, j))],
        out_specs=pl.BlockSpec((m, tn), lambda j: (0, j)),
        out_shape=jax.ShapeDtypeStruct((m, n), F32),
        compiler_params=_cparams(("parallel",)),
        name="ada_mod",
    )(cc, w, b.reshape(1, n))


def _normmod_kernel(x_ref, g_ref, sh_ref, sc_ref, *o_refs):
    x = x_ref[0]
    var = jnp.mean(x * x, axis=-1, keepdims=True)
    y = x * lax.rsqrt(var + NORM_EPS) * g_ref[...]
    y = y * (1.0 + sc_ref[0]) + sh_ref[0]
    for o_ref in o_refs:
        o_ref[0] = y.astype(o_ref.dtype)


def norm_modulate(x, g, shift, scale, out_dtypes, ts=512):
    bsz, s, d = x.shape
    ts = min(ts, s)
    per_batch = shift.shape[0] == bsz and bsz > 1
    mod_map = (lambda b, i: (b, 0, 0)) if per_batch else (lambda b, i: (0, 0, 0))
    outs = pl.pallas_call(
        _normmod_kernel,
        grid=(bsz, s // ts),
        in_specs=[pl.BlockSpec((1, ts, d), lambda b, i: (b, i, 0)),
                  pl.BlockSpec((1, d), lambda b, i: (0, 0)),
                  pl.BlockSpec((1, 1, d), mod_map),
                  pl.BlockSpec((1, 1, d), mod_map)],
        out_specs=[pl.BlockSpec((1, ts, d), lambda b, i: (b, i, 0)) for _ in out_dtypes],
        out_shape=[jax.ShapeDtypeStruct((bsz, s, d), dt) for dt in out_dtypes],
        compiler_params=_cparams(("parallel", "parallel")),
        name="norm_modulate",
    )(x, g.reshape(1, d), shift, scale)
    return outs


def _mm_resid_kernel(a_ref, w_ref, x_ref, gt_ref, o_ref):
    y = _dot(a_ref[...], w_ref[...])
    o_ref[0] = x_ref[0] + gt_ref[0] * y


def matmul_gated_residual(a, w, x, gate, tm=1024, tn=512):
    bsz, s, n = x.shape
    k = a.shape[1]
    tm = min(tm, s)
    nt = s // tm
    per_batch = gate.shape[0] == bsz and bsz > 1
    g_map = (lambda b, i, j: (b, 0, j)) if per_batch else (lambda b, i, j: (0, 0, j))
    return pl.pallas_call(
        _mm_resid_kernel,
        grid=(bsz, nt, n // tn),
        in_specs=[pl.BlockSpec((tm, k), lambda b, i, j: (b * nt + i, 0)),
                  pl.BlockSpec((k, tn), lambda b, i, j: (0, j)),
                  pl.BlockSpec((1, tm, tn), lambda b, i, j: (b, i, j)),
                  pl.BlockSpec((1, 1, tn), g_map)],
        out_specs=pl.BlockSpec((1, tm, tn), lambda b, i, j: (b, i, j)),
        out_shape=jax.ShapeDtypeStruct((bsz, s, n), F32),
        compiler_params=_cparams(("parallel", "parallel", "parallel")),
        name="out_proj_residual",
    )(a, w, x, gate)


def _merge_kernel(a_ref, b_ref, c_ref, d_ref, g0_ref, g1_ref, g2_ref, g3_ref, w_ref, o_ref):
    acc = None
    for i, (br, gr) in enumerate(((a_ref, g0_ref), (b_ref, g1_ref), (c_ref, g2_ref), (d_ref, g3_ref))):
        y = _dot(br[...], w_ref[i])
        gate = 1.0 / (1.0 + jnp.exp(-gr[...].astype(F32)))
        acc = gate * y if acc is None else acc + gate * y
    o_ref[...] = acc.astype(o_ref.dtype)


def merge_branches(outs, gate_logits, w_branch, tm=512, tn=512):
    t, kb = outs[0].shape
    d = w_branch.shape[-1]
    tm = min(tm, t)
    nj = d // tn
    in_specs = [pl.BlockSpec((tm, kb), lambda i, j: (i, 0)) for _ in range(N_BRANCH)]
    in_specs += [pl.BlockSpec((tm, tn), functools.partial(lambda i, j, br: (i, br * nj + j), br=br))
                 for br in range(N_BRANCH)]
    in_specs += [pl.BlockSpec((N_BRANCH, kb, tn), lambda i, j: (0, 0, j))]
    return pl.pallas_call(
        _merge_kernel,
        grid=(t // tm, nj),
        in_specs=in_specs,
        out_specs=pl.BlockSpec((tm, tn), lambda i, j: (i, j)),
        out_shape=jax.ShapeDtypeStruct((t, d), BF16),
        compiler_params=_cparams(("parallel", "parallel")),
        name="merge_branches",
    )(*outs, gate_logits, gate_logits, gate_logits, gate_logits, w_branch)


def _headprep_kernel(x_ref, g_ref, cos_ref, sin_ref, rot_ref, o_ref, *, rope, scale):
    x = x_ref[0].astype(F32)
    var = jnp.mean(x * x, axis=-1, keepdims=True)
    y = x * lax.rsqrt(var + NORM_EPS) * g_ref[...]
    if rope:
        y = y * cos_ref[...] + _dot(y, rot_ref[...], HI) * sin_ref[...]
    o_ref[0] = (y * scale).astype(o_ref.dtype)


def head_prep(x, g, rope_tabs, scale, ts=1024):
    n, s, dh = x.shape
    ts = min(ts, s)
    rope = rope_tabs is not None
    if rope:
        cos, sin, rot = rope_tabs
    else:
        cos = sin = jnp.zeros((s, dh), F32)
        rot = jnp.zeros((dh, dh), F32)
    return pl.pallas_call(
        functools.partial(_headprep_kernel, rope=rope, scale=scale),
        grid=(n, s // ts),
        in_specs=[pl.BlockSpec((1, ts, dh), lambda h, i: (h, i, 0)),
                  pl.BlockSpec((1, dh), lambda h, i: (0, 0)),
                  pl.BlockSpec((ts, dh), lambda h, i: (i, 0)),
                  pl.BlockSpec((ts, dh), lambda h, i: (i, 0)),
                  pl.BlockSpec((dh, dh), lambda h, i: (0, 0))],
        out_specs=pl.BlockSpec((1, ts, dh), lambda h, i: (h, i, 0)),
        out_shape=jax.ShapeDtypeStruct((n, s, dh), BF16),
        compiler_params=_cparams(("parallel", "parallel")),
        name="head_prep",
    )(x, g.reshape(1, dh), cos, sin, rot)


def rope_tables(s):
    t = np.arange(s)
    row = (t // GRID_W).astype(np.float32)
    col = (t % GRID_W).astype(np.float32)
    nf = ROPE_AXIS_DIM // 2
    inv = jnp.asarray(ROPE_BASE, F32) ** (-jnp.arange(nf, dtype=F32) / nf)
    ar = jnp.asarray(row)[:, None] * inv
    ac = jnp.asarray(col)[:, None] * inv
    cos = jnp.concatenate([jnp.cos(ar), jnp.cos(ar), jnp.cos(ac), jnp.cos(ac)], axis=-1)
    sin = jnp.concatenate([jnp.sin(ar), jnp.sin(ar), jnp.sin(ac), jnp.sin(ac)], axis=-1)
    rot = np.zeros((SWA_HEAD_DIM, SWA_HEAD_DIM), np.float32)
    for d in range(SWA_HEAD_DIM):
        if d % ROPE_AXIS_DIM < nf:
            rot[d + nf, d] = -1.0
        else:
            rot[d - nf, d] = 1.0
    return cos, sin, jnp.asarray(rot)


NA_ROWS_PER_STEP = 8


def _na_kernel(q_ref, k_ref, v_ref, kc_ref, vc_ref, bias_ref, o_ref, *, n_rows):
    i = pl.program_id(2)
    kc = kc_ref[0, 0]
    vc = vc_ref[0, 0]
    n_nb = NA_WIN_R * GRID_W

    def body(rr, carry):
        r = i * NA_ROWS_PER_STEP + rr
        r0 = jnp.clip(r - NA_WIN_R // 2, 0, n_rows - NA_WIN_R)
        qoff = pl.multiple_of(rr * GRID_W, GRID_W)
        koff = pl.multiple_of(r0 * GRID_W, GRID_W)
        q = q_ref[0, 0, pl.ds(qoff, GRID_W), :]
        kn = k_ref[0, 0, pl.ds(koff, n_nb), :]
        vn = v_ref[0, 0, pl.ds(koff, n_nb), :]
        s_nb = _dot_nt(q, kn) + bias_ref[r - r0, 0]
        s_cx = _dot_nt(q, kc)
        m = jnp.maximum(jnp.max(s_nb, axis=-1, keepdims=True), jnp.max(s_cx, axis=-1, keepdims=True))
        p_nb = jnp.exp(s_nb - m)
        p_cx = jnp.exp(s_cx - m)
        den = jnp.sum(p_nb, axis=-1, keepdims=True) + jnp.sum(p_cx, axis=-1, keepdims=True)
        o = _dot(p_nb.astype(BF16), vn) + _dot(p_cx.astype(BF16), vc)
        o_ref[0, 0, pl.ds(qoff, GRID_W), :] = (o / den).astype(o_ref.dtype)
        return carry

    lax.fori_loop(0, NA_ROWS_PER_STEP, body, 0, unroll=True)


def na_bias_table(rpb):
    nh = rpb.shape[0]
    wr, wc, gw = NA_WIN_R, NA_WIN_C, GRID_W
    rpb = rpb.astype(F32)

    def toeplitz(vec, n, period):
        tiled = jnp.broadcast_to(vec[..., None, :], vec.shape[:-1] + (n, period))
        flat = tiled.reshape(vec.shape[:-1] + (n * period,))[..., :n * (period - 1)]
        return flat.reshape(vec.shape[:-1] + (n, period - 1))[..., :n]

    rows = jnp.concatenate([rpb[:, wr - 1:], jnp.zeros((nh, 1, 2 * wc - 1), F32), rpb[:, :wr - 1]], axis=1)
    tab = toeplitz(jnp.moveaxis(rows, 1, -1), wr, 2 * wr)
    tab = jnp.moveaxis(tab, 1, -1)
    fill = jnp.zeros(tab.shape[:-1] + (2 * gw - (2 * wc - 1),), F32)
    cols_ext = jnp.concatenate([tab[..., wc - 1:], fill, tab[..., :wc - 1]], axis=-1)
    tab = toeplitz(cols_ext, gw, 2 * gw)
    cols = np.arange(gw)
    c_start = np.clip(cols - wc // 2, 0, gw - wc)
    valid = (cols[None, :] >= c_start[:, None]) & (cols[None, :] < c_start[:, None] + wc)
    tab = jnp.where(jnp.asarray(valid)[None, None, None], tab, NEG)
    tab = tab.transpose(1, 0, 3, 2, 4)
    return tab.reshape(wr, nh, gw, wr * gw)


def na_attention(q, k, v, kc, vc, bias):
    bsz, h, s, dh = q.shape
    lc = kc.shape[2]
    n_rows = s // GRID_W
    tq = NA_ROWS_PER_STEP * GRID_W
    n_nb = NA_WIN_R * GRID_W
    return pl.pallas_call(
        functools.partial(_na_kernel, n_rows=n_rows),
        grid=(bsz, h, s // tq),
        in_specs=[pl.BlockSpec((1, 1, tq, dh), lambda b, hh, i: (b, hh, i, 0)),
                  pl.BlockSpec((1, 1, s, dh), lambda b, hh, i: (b, hh, 0, 0)),
                  pl.BlockSpec((1, 1, s, dh), lambda b, hh, i: (b, hh, 0, 0)),
                  pl.BlockSpec((1, 1, lc, dh), lambda b, hh, i: (b, hh, 0, 0)),
                  pl.BlockSpec((1, 1, lc, dh), lambda b, hh, i: (b, hh, 0, 0)),
                  pl.BlockSpec((NA_WIN_R, 1, GRID_W, n_nb), lambda b, hh, i: (0, hh, 0, 0))],
        out_specs=pl.BlockSpec((1, 1, tq, dh), lambda b, hh, i: (b, hh, i, 0)),
        out_shape=jax.ShapeDtypeStruct((bsz, h, s, dh), BF16),
        compiler_params=_cparams(("parallel", "parallel", "arbitrary")),
        name="na_attention",
    )(q, k, v, kc, vc, bias)


def _swa_kernel(sink_ref, q_ref, k_ref, v_ref, kc_ref, vc_ref, o_ref, *, seq):
    kh = pl.program_id(1)
    n = pl.program_id(2)
    span = 3 * SWA_BLOCK
    start = pl.multiple_of(jnp.clip((n - 1) * SWA_BLOCK, 0, seq - span), SWA_BLOCK)
    kw = k_ref[0, 0, pl.ds(start, span), :]
    vw = v_ref[0, 0, pl.ds(start, span), :]
    kc = kc_ref[0, 0]
    vc = vc_ref[0, 0]
    qpos = n * SWA_BLOCK + lax.broadcasted_iota(jnp.int32, (SWA_BLOCK, span), 0)
    kpos = start + lax.broadcasted_iota(jnp.int32, (SWA_BLOCK, span), 1)
    valid = jnp.abs(qpos - kpos) <= SWA_WINDOW
    for g in range(SWA_GROUP):
        q = q_ref[0, 0, g]
        s_loc = jnp.where(valid, _dot_nt(q, kw), NEG)
        s_ctx = _dot_nt(q, kc)
        sink = sink_ref[kh * SWA_GROUP + g]
        m = jnp.maximum(jnp.max(s_loc, axis=-1, keepdims=True), jnp.max(s_ctx, axis=-1, keepdims=True))
        m = jnp.maximum(m, sink)
        p_loc = jnp.exp(s_loc - m)
        p_ctx = jnp.exp(s_ctx - m)
        den = (jnp.sum(p_loc, axis=-1, keepdims=True) + jnp.sum(p_ctx, axis=-1, keepdims=True)
               + jnp.exp(sink - m))
        o = _dot(p_loc.astype(BF16), vw) + _dot(p_ctx.astype(BF16), vc)
        o_ref[0, 0, g] = (o / den).astype(o_ref.dtype)


def swa_attention(q, k, v, kc, vc, sink):
    bsz, hk, grp, s, dh = q.shape
    lc = kc.shape[2]
    assert s >= 3 * SWA_BLOCK
    return pl.pallas_call(
        functools.partial(_swa_kernel, seq=s),
        grid=(bsz, hk, s // SWA_BLOCK),
        in_specs=[pl.BlockSpec(memory_space=pltpu.SMEM),
                  pl.BlockSpec((1, 1, grp, SWA_BLOCK, dh), lambda b, h, n: (b, h, 0, n, 0)),
                  pl.BlockSpec((1, 1, s, dh), lambda b, h, n: (b, h, 0, 0)),
                  pl.BlockSpec((1, 1, s, dh), lambda b, h, n: (b, h, 0, 0)),
                  pl.BlockSpec((1, 1, lc, dh), lambda b, h, n: (b, h, 0, 0)),
                  pl.BlockSpec((1, 1, lc, dh), lambda b, h, n: (b, h, 0, 0))],
        out_specs=pl.BlockSpec((1, 1, grp, SWA_BLOCK, dh), lambda b, h, n: (b, h, 0, n, 0)),
        out_shape=jax.ShapeDtypeStruct((bsz, hk, grp, s, dh), BF16),
        compiler_params=_cparams(("parallel", "parallel", "arbitrary")),
        name="swa_attention",
    )(sink.astype(F32), q, k, v, kc, vc)


def _ctx_attn_kernel(sink_ref, q_ref, k_ref, v_ref, o_ref, *, grp):
    kh = pl.program_id(1)
    k = k_ref[0, 0]
    v = v_ref[0, 0]
    for g in range(grp):
        q = q_ref[0, 0, g]
        s = _dot_nt(q, k)
        sink = sink_ref[kh * grp + g]
        m = jnp.maximum(jnp.max(s, axis=-1, keepdims=True), sink)
        p = jnp.exp(s - m)
        den = jnp.sum(p, axis=-1, keepdims=True) + jnp.exp(sink - m)
        o_ref[0, 0, g] = (_dot(p.astype(BF16), v) / den).astype(o_ref.dtype)


def ctx_attention(q, k, v, sink):
    bsz, hk, grp, n, dh = q.shape
    return pl.pallas_call(
        functools.partial(_ctx_attn_kernel, grp=grp),
        grid=(bsz, hk),
        in_specs=[pl.BlockSpec(memory_space=pltpu.SMEM),
                  pl.BlockSpec((1, 1, grp, n, dh), lambda b, h: (b, h, 0, 0, 0)),
                  pl.BlockSpec((1, 1, n, dh), lambda b, h: (b, h, 0, 0)),
                  pl.BlockSpec((1, 1, n, dh), lambda b, h: (b, h, 0, 0))],
        out_specs=pl.BlockSpec((1, 1, grp, n, dh), lambda b, h: (b, h, 0, 0, 0)),
        out_shape=jax.ShapeDtypeStruct((bsz, hk, grp, n, dh), BF16),
        compiler_params=_cparams(("parallel", "parallel")),
        name="ctx_attention",
    )(sink.astype(F32), q, k, v)


CONV_PAD = 8
CONV_ROWS = 256
CONV_COLS = 256


def _conv_kernel(x_ref, w_ref, b_ref, o_ref, *, seq):
    rows = min(CONV_ROWS, seq)
    n = seq // rows

    def body(i, carry):
        base = pl.multiple_of(i * rows, rows)
        acc = jnp.zeros((rows, CONV_COLS), F32) + b_ref[...]
        prev_off = pl.multiple_of(jnp.maximum(base - CONV_PAD, 0), CONV_PAD)
        next_off = pl.multiple_of(jnp.minimum(base + rows, seq - CONV_PAD), CONV_PAD)
        prev = jnp.where(i > 0, x_ref[0, pl.ds(prev_off, CONV_PAD), :], 0.0)
        nxt = jnp.where(i < n - 1, x_ref[0, pl.ds(next_off, CONV_PAD), :], 0.0)
        halo = jnp.concatenate([prev, x_ref[0, pl.ds(base, rows), :], nxt], axis=0)
        for k in range(SSD_CONV_K):
            off = CONV_PAD + k - SSD_CONV_K // 2
            acc = acc + w_ref[k:k + 1, :] * halo[off:off + rows, :]
        o_ref[0, pl.ds(base, rows), :] = _silu(acc).astype(o_ref.dtype)
        return carry

    lax.fori_loop(0, seq // rows, body, 0)


def conv_silu(xbc, w, b):
    bsz, s, c = xbc.shape
    return pl.pallas_call(
        functools.partial(_conv_kernel, seq=s),
        grid=(bsz, c // CONV_COLS),
        in_specs=[pl.BlockSpec((1, s, CONV_COLS), lambda bb, j: (bb, 0, j)),
                  pl.BlockSpec((SSD_CONV_K, CONV_COLS), lambda bb, j: (0, j)),
                  pl.BlockSpec((1, CONV_COLS), lambda bb, j: (0, j))],
        out_specs=pl.BlockSpec((1, s, CONV_COLS), lambda bb, j: (bb, 0, j)),
        out_shape=jax.ShapeDtypeStruct((bsz, s, c), BF16),
        compiler_params=_cparams(("parallel", "parallel")),
        name="conv_silu",
    )(xbc, w.astype(F32), b.reshape(1, c).astype(F32))


def _scan_masks(length, d):
    row = lax.broadcasted_iota(jnp.int32, (length, length), 0)
    col = lax.broadcasted_iota(jnp.int32, (length, length), 1)
    return (row - col) * (1 - 2 * d) >= 0


def _ssd_kernel(xs_ref, bm_ref, cm_ref, bmt_ref, dt_ref, dtt_ref, alog_ref, alogt_ref, dtb_ref, dtbt_ref,
                h0_ref, y_ref, hout_ref, state_ref, *, n_chunks):
    d = pl.program_id(0)
    c = pl.program_id(2)

    @pl.when(c == 0)
    def _():
        state_ref[...] = h0_ref[0, 0]

    length = SSD_CHUNK
    incl = _scan_masks(length, d)
    tri = incl.astype(F32)
    dt = _softplus(dt_ref[0, 0] + dtb_ref[0])
    dtt = _softplus(dtt_ref[0, 0] + dtbt_ref[0])
    ad = dt * (-jnp.exp(alog_ref[0]))
    adt = dtt * (-jnp.exp(alogt_ref[0]))
    acs = _dot(tri, ad, HI)
    acst = _dot_nt(adt, tri, HI)
    tot = jnp.sum(ad, axis=0, keepdims=True)
    tott = jnp.sum(adt, axis=1, keepdims=True)
    rep = SSD_HEADS // SSD_GROUPS
    for gi in range(SSD_GROUPS):
        gsl = slice(gi * SSD_STATE, (gi + 1) * SSD_STATE)
        cg = cm_ref[0, :, gsl]
        bg = bm_ref[0, :, gsl]
        bgt = bmt_ref[0, gsl, :].astype(F32)
        cb = _dot_nt(cg, bg)
        for hh in range(rep):
            h = gi * rep + hh
            psl = slice(h * SSD_HEADDIM, (h + 1) * SSD_HEADDIM)
            a_col = acs[:, h:h + 1]
            a_row = acst[h:h + 1, :]
            lmat = jnp.exp(jnp.where(incl, a_col - a_row, NEG))
            xdt = (xs_ref[0, :, psl].astype(F32) * dt[:, h:h + 1]).astype(BF16)
            st = state_ref[h]
            y = _dot((cb * lmat).astype(BF16), xdt)
            y = y + jnp.exp(a_col) * _dot(cg, st.astype(BF16))
            dec = jnp.exp(tott[h:h + 1, :] - a_row)
            state_ref[h] = jnp.exp(tot[:, h:h + 1]) * st + _dot((bgt * dec).astype(BF16), xdt)
            y_ref[0, 0, :, psl] = y.astype(y_ref.dtype)

    @pl.when(c == n_chunks - 1)
    def _():
        hout_ref[0, 0] = state_ref[...]


def _chunk_index(d, c, n_chunks):
    return c + d * (n_chunks - 1 - 2 * c)


def ssd_scan(xbc_act, dt_raw, a_log, dt_bias, h0):
    bsz, s, _ = xbc_act.shape
    nc = s // SSD_CHUNK
    hh = SSD_HEADS
    ng = SSD_GROUPS * SSD_STATE
    bmt = jnp.swapaxes(xbc_act[:, :, SSD_D_INNER:SSD_D_INNER + ng], 1, 2)
    dt2 = dt_raw.reshape(bsz, s, 2, hh).transpose(2, 0, 1, 3)
    dtt = dt2.transpose(0, 1, 3, 2)
    a_log = a_log.astype(F32)
    dt_bias = dt_bias.astype(F32)
    cmap = functools.partial(_chunk_index, n_chunks=nc)
    nb_x = SSD_D_INNER // ng
    return pl.pallas_call(
        functools.partial(_ssd_kernel, n_chunks=nc),
        grid=(2, bsz, nc),
        in_specs=[pl.BlockSpec((1, SSD_CHUNK, SSD_D_INNER), lambda d, b, c: (b, cmap(d, c), 0)),
                  pl.BlockSpec((1, SSD_CHUNK, ng), lambda d, b, c: (b, cmap(d, c), nb_x)),
                  pl.BlockSpec((1, SSD_CHUNK, ng), lambda d, b, c: (b, cmap(d, c), nb_x + 1)),
                  pl.BlockSpec((1, ng, SSD_CHUNK), lambda d, b, c: (b, 0, cmap(d, c))),
                  pl.BlockSpec((1, 1, SSD_CHUNK, hh), lambda d, b, c: (d, b, cmap(d, c), 0)),
                  pl.BlockSpec((1, 1, hh, SSD_CHUNK), lambda d, b, c: (d, b, 0, cmap(d, c))),
                  pl.BlockSpec((1, 1, hh), lambda d, b, c: (d, 0, 0)),
                  pl.BlockSpec((1, hh, 1), lambda d, b, c: (d, 0, 0)),
                  pl.BlockSpec((1, 1, hh), lambda d, b, c: (d, 0, 0)),
                  pl.BlockSpec((1, hh, 1), lambda d, b, c: (d, 0, 0)),
                  pl.BlockSpec((1, 1, hh, SSD_STATE, SSD_HEADDIM), lambda d, b, c: (d, b, 0, 0, 0))],
        out_specs=[pl.BlockSpec((1, 1, SSD_CHUNK, SSD_D_INNER), lambda d, b, c: (d, b, cmap(d, c), 0)),
                   pl.BlockSpec((1, 1, hh, SSD_STATE, SSD_HEADDIM), lambda d, b, c: (d, b, 0, 0, 0))],
        out_shape=[jax.ShapeDtypeStruct((2, bsz, s, SSD_D_INNER), BF16),
                   jax.ShapeDtypeStruct((2, bsz, hh, SSD_STATE, SSD_HEADDIM), F32)],
        scratch_shapes=[pltpu.VMEM((hh, SSD_STATE, SSD_HEADDIM), F32)],
        compiler_params=_cparams(("parallel", "parallel", "arbitrary")),
        name="ssd_scan",
    )(xbc_act, xbc_act, xbc_act, bmt, dt2, dtt,
      a_log.reshape(2, 1, hh), a_log.reshape(2, hh, 1), dt_bias.reshape(2, 1, hh), dt_bias.reshape(2, hh, 1), h0)


def _ssd_finish_kernel(y_ref, xs_ref, z_ref, d_ref, g_ref, o_ref):
    y = y_ref[0, 0].astype(F32) + y_ref[1, 0].astype(F32) + d_ref[...] * xs_ref[0].astype(F32)
    z = z_ref[0].astype(F32)
    u = y * _silu(z)
    var = jnp.mean(u * u, axis=-1, keepdims=True)
    o_ref[0] = (u * lax.rsqrt(var + NORM_EPS) * g_ref[...]).astype(o_ref.dtype)


def ssd_finish(y, xbc_act, z, d_skip, norm_g, ts=512):
    _, bsz, s, di = y.shape
    ts = min(ts, s)
    dvec = jnp.repeat(d_skip.astype(F32), SSD_HEADDIM).reshape(1, di)
    return pl.pallas_call(
        _ssd_finish_kernel,
        grid=(bsz, s // ts),
        in_specs=[pl.BlockSpec((2, 1, ts, di), lambda b, i: (0, b, i, 0)),
                  pl.BlockSpec((1, ts, di), lambda b, i: (b, i, 0)),
                  pl.BlockSpec((1, ts, di), lambda b, i: (b, i, 0)),
                  pl.BlockSpec((1, di), lambda b, i: (0, 0)),
                  pl.BlockSpec((1, di), lambda b, i: (0, 0))],
        out_specs=pl.BlockSpec((1, ts, di), lambda b, i: (b, i, 0)),
        out_shape=jax.ShapeDtypeStruct((bsz, s, di), BF16),
        compiler_params=_cparams(("parallel", "parallel")),
        name="ssd_finish",
    )(y, xbc_act, z, dvec, norm_g.reshape(1, di).astype(F32))


GLA_STEP = 128


def _gla_kernel(q_ref, k_ref, kt_ref, v_ref, glr_ref, glrt_ref, wg_ref, wgt_ref, bg_ref, bgt_ref,
                s0_ref, o_ref, sout_ref, state_ref, *, n_steps, reverse):
    c = pl.program_id(1)

    @pl.when(c == 0)
    def _():
        state_ref[...] = s0_ref[0]

    length = GLA_CHUNK
    incl = _scan_masks(length, int(reverse))
    tri = incl.astype(F32)
    subs = range(GLA_STEP // length)
    for sub in (reversed(subs) if reverse else subs):
        tsl = slice(sub * length, (sub + 1) * length)
        g = _log_sigmoid(_dot(glr_ref[0, tsl, :], wg_ref[...], HI) + bg_ref[...]) / GLA_TAU
        gt = _log_sigmoid(_dot(wgt_ref[...], glrt_ref[0, :, tsl], HI) + bgt_ref[...]) / GLA_TAU
        gc = _dot(tri, g, HI)
        gct = _dot_nt(gt, tri, HI)
        tott = jnp.sum(gt, axis=1, keepdims=True)
        q_in = (q_ref[0, tsl, :].astype(F32) * (GLA_HEAD_K ** -0.5) * jnp.exp(gc)).astype(BF16)
        k_in = (k_ref[0, tsl, :].astype(F32) * jnp.exp(-gc)).astype(BF16)
        k_out_t = (kt_ref[0, :, tsl].astype(F32) * jnp.exp(tott - gct)).astype(BF16)
        dec_t = jnp.exp(tott)
        for h in range(GLA_HEADS):
            ksl = slice(h * GLA_HEAD_K, (h + 1) * GLA_HEAD_K)
            vsl = slice(h * GLA_HEAD_V, (h + 1) * GLA_HEAD_V)
            qh = q_in[:, ksl]
            vh = v_ref[0, tsl, vsl]
            att = jnp.where(incl, _dot_nt(qh, k_in[:, ksl]), 0.0)
            st = state_ref[h]
            o = _dot(att.astype(BF16), vh) + _dot(qh, st.astype(BF16))
            state_ref[h] = st * dec_t[ksl, :] + _dot(k_out_t[ksl, :], vh)
            o_ref[0, tsl, vsl] = o.astype(o_ref.dtype)

    @pl.when(c == n_steps - 1)
    def _():
        sout_ref[0] = state_ref[...]


def gla_scan(q, k, v, glr, w_gate, b_gate, s0):
    bsz, s, kd = q.shape
    vd = v.shape[-1]
    ns = s // GLA_STEP
    r = GLA_GATE_RANK
    kt = jnp.swapaxes(k, 1, 2)
    w_gate = w_gate.astype(F32)
    b_gate = b_gate.astype(F32)
    outs, states = [], []
    for d in range(2):
        cmap = (lambda c: ns - 1 - c) if d else (lambda c: c)
        glr_d = glr[:, :, d * r:(d + 1) * r]
        o, st = pl.pallas_call(
            functools.partial(_gla_kernel, n_steps=ns, reverse=bool(d)),
            grid=(bsz, ns),
            in_specs=[pl.BlockSpec((1, GLA_STEP, kd), lambda b, c, cmap=cmap: (b, cmap(c), 0)),
                      pl.BlockSpec((1, GLA_STEP, kd), lambda b, c, cmap=cmap: (b, cmap(c), 0)),
                      pl.BlockSpec((1, kd, GLA_STEP), lambda b, c, cmap=cmap: (b, 0, cmap(c))),
                      pl.BlockSpec((1, GLA_STEP, vd), lambda b, c, cmap=cmap: (b, cmap(c), 0)),
                      pl.BlockSpec((1, GLA_STEP, r), lambda b, c, cmap=cmap: (b, cmap(c), 0)),
                      pl.BlockSpec((1, r, GLA_STEP), lambda b, c, cmap=cmap: (b, 0, cmap(c))),
                      pl.BlockSpec((r, kd), lambda b, c: (0, 0)),
                      pl.BlockSpec((kd, r), lambda b, c: (0, 0)),
                      pl.BlockSpec((1, kd), lambda b, c: (0, 0)),
                      pl.BlockSpec((kd, 1), lambda b, c: (0, 0)),
                      pl.BlockSpec((1, GLA_HEADS, GLA_HEAD_K, GLA_HEAD_V), lambda b, c: (b, 0, 0, 0))],
            out_specs=[pl.BlockSpec((1, GLA_STEP, vd), lambda b, c, cmap=cmap: (b, cmap(c), 0)),
                       pl.BlockSpec((1, GLA_HEADS, GLA_HEAD_K, GLA_HEAD_V), lambda b, c: (b, 0, 0, 0))],
            out_shape=[jax.ShapeDtypeStruct((bsz, s, vd), BF16),
                       jax.ShapeDtypeStruct((bsz, GLA_HEADS, GLA_HEAD_K, GLA_HEAD_V), F32)],
            scratch_shapes=[pltpu.VMEM((GLA_HEADS, GLA_HEAD_K, GLA_HEAD_V), F32)],
            compiler_params=_cparams(("parallel", "arbitrary")),
            name="gla_scan_bwd" if d else "gla_scan_fwd",
        )(q, k, kt, v, glr_d, jnp.swapaxes(glr_d, 1, 2), w_gate[d], w_gate[d].T,
          b_gate[d].reshape(1, kd), b_gate[d].reshape(kd, 1), s0[d])
        outs.append(o)
        states.append(st)
    return outs, states


def _gla_finish_kernel(of_ref, ob_ref, r_ref, g_ref, out_ref):
    o = of_ref[0].astype(F32) + ob_ref[0].astype(F32)
    r = r_ref[0].astype(F32)
    for h in range(GLA_HEADS):
        vsl = slice(h * GLA_HEAD_V, (h + 1) * GLA_HEAD_V)
        oh = o[:, vsl]
        var = jnp.mean(oh * oh, axis=-1, keepdims=True)
        y = oh * lax.rsqrt(var + NORM_EPS) * g_ref[...]
        out_ref[0, :, vsl] = (y * _silu(r[:, vsl])).astype(out_ref.dtype)


def gla_finish(o, r, norm_g, ts=512):
    bsz, s, vd = o[0].shape
    ts = min(ts, s)
    return pl.pallas_call(
        _gla_finish_kernel,
        grid=(bsz, s // ts),
        in_specs=[pl.BlockSpec((1, ts, vd), lambda b, i: (b, i, 0)),
                  pl.BlockSpec((1, ts, vd), lambda b, i: (b, i, 0)),
                  pl.BlockSpec((1, ts, vd), lambda b, i: (b, i, 0)),
                  pl.BlockSpec((1, GLA_HEAD_V), lambda b, i: (0, 0))],
        out_specs=pl.BlockSpec((1, ts, vd), lambda b, i: (b, i, 0)),
        out_shape=jax.ShapeDtypeStruct((bsz, s, vd), BF16),
        compiler_params=_cparams(("parallel", "parallel")),
        name="gla_finish",
    )(o[0], o[1], r, norm_g.reshape(1, GLA_HEAD_V).astype(F32))


PEER_SCORE_TOKENS = 128
PEER_TOKENS = 128
PEER_SLOTS = 6
PEER_CAND_ROWS = -(-sum(PEER_TOPK // (a + 1) for a in range(PEER_TOPK)) // 8) * 8
PEER_SLAB = 2 * (D_MODEL // 2) // V7X_LANES
PEER_PITCH = PEER_SLAB + 8


def _extract_topk(s, rows, vals_ref, pos_ref):
    iota = lax.broadcasted_iota(jnp.int32, s.shape, 0).astype(F32)

    def body(j, cur):
        m = jnp.max(cur, axis=0, keepdims=True)
        pos = jnp.min(jnp.where(cur == m, iota, float(rows)), axis=0, keepdims=True)
        vals_ref[pl.ds(j, 1), :] = m
        pos_ref[pl.ds(j, 1), :] = pos
        return jnp.where(iota == pos, -jnp.inf, cur)

    lax.fori_loop(0, PEER_TOPK, body, s)


def _peer_score_kernel(q_ref, k1_ref, k2_ref, idx_ref, gate_ref, v12_ref, p12_ref, vt_ref, pt_ref, cand_ref):
    half = PEER_QDIM // 2
    tt = q_ref.shape[0]
    q = q_ref[...]
    s12 = jnp.concatenate([_dot_nt(k1_ref[...], q[:, :half], HI), _dot_nt(k2_ref[...], q[:, half:], HI)], axis=1)
    _extract_topk(s12, PEER_NKEYS, v12_ref, p12_ref)
    v1 = v12_ref[:, 0:tt]
    v2 = v12_ref[:, tt:2 * tt]
    p1 = p12_ref[:, 0:tt]
    p2 = p12_ref[:, tt:2 * tt]
    widths = [PEER_TOPK // (a + 1) for a in range(PEER_TOPK)]
    starts = [sum(widths[:a]) for a in range(PEER_TOPK)]
    n_cand = sum(widths)
    for a in range(PEER_TOPK):
        cand_ref[starts[a]:starts[a] + widths[a], :] = v1[a:a + 1, :] + v2[0:widths[a], :]
    cand_ref[n_cand:, :] = jnp.full((cand_ref.shape[0] - n_cand, tt), -jnp.inf, F32)
    _extract_topk(cand_ref[...], cand_ref.shape[0], vt_ref, pt_ref)
    pos = pt_ref[...]
    ia = jnp.zeros_like(pos)
    ib = pos
    for a in range(1, PEER_TOPK):
        past = pos >= float(starts[a])
        ia = ia + jnp.where(past, 1.0, 0.0)
        ib = ib - jnp.where(past, float(widths[a - 1]), 0.0)
    e1 = jnp.zeros_like(pos)
    e2 = jnp.zeros_like(pos)
    for a in range(PEER_TOPK):
        e1 = e1 + jnp.where(ia == float(a), p1[a:a + 1, :], 0.0)
        e2 = e2 + jnp.where(ib == float(a), p2[a:a + 1, :], 0.0)
    idx_ref[...] = (e1 * float(PEER_NKEYS) + e2).astype(jnp.int32)
    top = vt_ref[...]
    p = jnp.exp(top - top[0:1, :])
    gate_ref[...] = p / jnp.sum(p, axis=0, keepdims=True)


def peer_retrieve(q, k1, k2):
    t = q.shape[0]
    tt = min(PEER_SCORE_TOKENS, t)
    kk = PEER_TOPK
    return pl.pallas_call(
        _peer_score_kernel,
        grid=(t // tt, PEER_HEADS),
        in_specs=[pl.BlockSpec((tt, PEER_QDIM), lambda i, h: (i, h)),
                  pl.BlockSpec((PEER_NKEYS, PEER_QDIM // 2), lambda i, h: (0, 0)),
                  pl.BlockSpec((PEER_NKEYS, PEER_QDIM // 2), lambda i, h: (0, 0))],
        out_specs=[pl.BlockSpec((kk, tt), lambda i, h: (h, i)),
                   pl.BlockSpec((kk, tt), lambda i, h: (h, i))],
        out_shape=[jax.ShapeDtypeStruct((PEER_NSEL, t), jnp.int32),
                   jax.ShapeDtypeStruct((PEER_NSEL, t), F32)],
        scratch_shapes=[pltpu.VMEM((kk, 2 * tt), F32), pltpu.VMEM((kk, 2 * tt), F32),
                        pltpu.VMEM((kk, tt), F32), pltpu.VMEM((kk, tt), F32),
                        pltpu.VMEM((PEER_CAND_ROWS, tt), F32)],
        compiler_params=_cparams(("parallel", "parallel")),
        name="peer_retrieve",
    )(q, k1.astype(F32), k2.astype(F32))


def _gelu(x):
    return 0.5 * x * (1.0 + lax.erf(x * (1.0 / math.sqrt(2.0))))


def pack_expert_tables(u, v):
    def pack(tab):
        bits = lax.bitcast_convert_type(tab.astype(BF16), jnp.uint16).astype(jnp.uint32)
        half = tab.shape[1] // 2
        return bits[:, :half] | (bits[:, half:] << 16)

    n_exp = u.shape[0]
    slabs = jnp.concatenate([pack(u).reshape(n_exp, -1, V7X_LANES), pack(v).reshape(n_exp, -1, V7X_LANES)], axis=1)
    return slabs.reshape(n_exp * PEER_SLAB, V7X_LANES)


def _unpack_pair(words):
    lo = pltpu.bitcast(words << 16, F32)
    hi = pltpu.bitcast(words & jnp.uint32(0xFFFF0000), F32)
    return lo, hi


def _peer_expert_kernel(idx_ref, gate_ref, h_ref, x_ref, gt_ref, uv_hbm, o_ref, *scratch, tokens):
    bufs, sem = scratch[:PEER_SLOTS], scratch[PEER_SLOTS]
    half = D_MODEL // 2
    n_chunks = PEER_SLAB // 2
    per_step = PEER_NSEL // PEER_SLAB
    depth = PEER_SLOTS - 1

    def row_copy(e, slot, k):
        src = uv_hbm.at[pl.ds(pl.multiple_of(e * PEER_SLAB, PEER_SLAB), PEER_SLAB), :]
        return pltpu.make_async_copy(src, bufs[slot].at[pl.ds(k * PEER_PITCH, PEER_SLAB), :], sem.at[slot])

    def issue(tok, slot, ks):
        for k in ks:
            row_copy(idx_ref[k, tok], slot, k).start(priority=k % 2)

    def wait(slot):
        n = PEER_NSEL * PEER_SLAB
        pltpu.make_async_copy(uv_hbm.at[pl.ds(0, n), :], bufs[slot].at[pl.ds(0, n), :], sem.at[slot]).wait()

    lane = lax.broadcasted_iota(jnp.int32, (PEER_NSEL, tokens), 1)

    def chunk(slot, row):
        return _unpack_pair(bufs[slot][pl.ds(row, PEER_NSEL, stride=PEER_PITCH), :])

    def compute(j, slot, prefetch):
        def maybe_issue(step):
            if prefetch:
                issue(j + depth, (slot + depth) % PEER_SLOTS, range(step * per_step, (step + 1) * per_step))

        gcol = jnp.sum(jnp.where(lane == j, gate_ref[...], 0.0), axis=1, keepdims=True)
        p = jnp.zeros((PEER_NSEL, V7X_LANES), F32)
        t_lo = h_ref[pl.ds(j, 1), 0:half]
        t_hi = h_ref[pl.ds(j, 1), half:2 * half]
        for c in range(n_chunks):
            maybe_issue(c)
            ua, ub = chunk(slot, c)
            cols = slice(c * V7X_LANES, (c + 1) * V7X_LANES)
            p = p + ua * t_lo[:, cols] + ub * t_hi[:, cols]
        act = jnp.sum(p, axis=1, keepdims=True)
        a = jnp.broadcast_to(_gelu(act) * gcol, (PEER_NSEL, V7X_LANES))
        out_lo, out_hi = [], []
        for c in range(n_chunks):
            maybe_issue(n_chunks + c)
            va, vb = chunk(slot, n_chunks + c)
            out_lo.append(jnp.sum(a * va, axis=0, keepdims=True))
            out_hi.append(jnp.sum(a * vb, axis=0, keepdims=True))
        gt = gt_ref[0]
        for cols, parts in ((slice(0, half), out_lo), (slice(half, 2 * half), out_hi)):
            o_ref[0, pl.ds(j, 1), cols] = (x_ref[0, pl.ds(j, 1), cols]
                                           + gt[:, cols] * jnp.concatenate(parts, axis=1))

    for j0 in range(depth):
        issue(j0, j0 % PEER_SLOTS, range(PEER_NSEL))

    def body(it, carry):
        for s in range(PEER_SLOTS):
            wait(s)
            compute(it * PEER_SLOTS + s, s, True)
        return carry

    n_main = (tokens - depth) // PEER_SLOTS
    lax.fori_loop(0, n_main, body, 0)
    for j in range(n_main * PEER_SLOTS, tokens):
        wait(j % PEER_SLOTS)
        compute(j, j % PEER_SLOTS, j + depth < tokens)


def peer_experts(idx, gate, h, x, gt, uv):
    bsz, s, dm = x.shape
    t = bsz * s
    tt = min(PEER_TOKENS, s)
    nt = s // tt
    per_batch = gt.shape[0] == bsz and bsz > 1
    g_map = (lambda b, i: (b, 0, 0)) if per_batch else (lambda b, i: (0, 0, 0))
    out = pl.pallas_call(
        functools.partial(_peer_expert_kernel, tokens=tt),
        grid=(bsz, nt),
        in_specs=[pl.BlockSpec((PEER_NSEL, tt), lambda b, i: (0, b * nt + i), memory_space=pltpu.SMEM),
                  pl.BlockSpec((PEER_NSEL, tt), lambda b, i: (0, b * nt + i)),
                  pl.BlockSpec((tt, dm), lambda b, i: (b * nt + i, 0)),
                  pl.BlockSpec((1, tt, dm), lambda b, i: (b, i, 0)),
                  pl.BlockSpec((1, 1, dm), g_map),
                  pl.BlockSpec(memory_space=pl.ANY)],
        out_specs=pl.BlockSpec((1, tt, dm), lambda b, i: (b, i, 0)),
        out_shape=jax.ShapeDtypeStruct((bsz, s, dm), F32),
        scratch_shapes=[pltpu.VMEM((PEER_NSEL * PEER_PITCH, V7X_LANES), jnp.uint32) for _ in range(PEER_SLOTS)]
        + [pltpu.SemaphoreType.DMA((PEER_SLOTS,))],
        compiler_params=pltpu.CompilerParams(dimension_semantics=("arbitrary", "arbitrary"),
                                             vmem_limit_bytes=VMEM_LIMIT),
        name="peer_experts",
    )(idx, gate, h, x.reshape(bsz, s, dm), gt, uv)
    return out


def peer_sublayer(x, g_norm, shift, scale, gt, wq, k1, k2, uv):
    bsz, s, dm = x.shape
    h_bf, h_f32 = norm_modulate(x, g_norm, shift, scale, (BF16, F32))
    q = matmul(h_bf.reshape(bsz * s, dm), wq, F32)
    idx, gate = peer_retrieve(q, k1, k2)
    return peer_experts(idx, gate, h_f32.reshape(bsz * s, dm), x, gt, uv)


def _head_major(t, n_heads):
    bsz, s, _ = t.shape
    return t.reshape(bsz, s, n_heads, -1).transpose(0, 2, 1, 3)


_PROJ_GROUPS = dict(z=(0, BF16), xbc=(1, F32), na=(3, BF16), gla_q=(4, BF16), gla_k=(5, BF16), gla_v=(6, BF16),
                    gla_r=(7, BF16), swa_q=(9, BF16), swa_k=(10, BF16), swa_v=(11, BF16), gates=(12, BF16))


def _split_w_in(w):
    o = IN_OFFS
    ws = {name: w[:, o[i]:o[i + 1]].astype(BF16) for name, (i, _) in _PROJ_GROUPS.items()}
    small = jnp.concatenate([w[:, o[2]:o[3]], w[:, o[8]:o[9]]], axis=1)
    ws['small'] = jnp.pad(small, ((0, 0), (0, V7X_LANES - small.shape[1]))).astype(BF16)
    return ws


def _project(h, ws):
    bsz, s, dm = h.shape
    hf = h.reshape(bsz * s, dm)
    out = {name: matmul(hf, ws[name], dt).reshape(bsz, s, -1) for name, (_, dt) in _PROJ_GROUPS.items()}
    sm = matmul(hf, ws['small'], F32).reshape(bsz, s, V7X_LANES)
    out['dt'] = sm[:, :, :2 * SSD_HEADS]
    out['glr'] = sm[:, :, 2 * SSD_HEADS:2 * SSD_HEADS + 2 * GLA_GATE_RANK]
    return out


def _flat_heads(t):
    return t.reshape((-1,) + t.shape[-2:])


def mixer_sublayer(hx, hc, p, rope, need_ctx):
    bsz, s, _ = hx.shape
    lc = hc.shape[1]
    ws = _split_w_in(p['w_in'])
    px = _project(hx, ws)
    pc = _project(hc, ws)
    zeros_h = jnp.zeros((2, bsz, SSD_HEADS, SSD_STATE, SSD_HEADDIM), F32)
    zero_s = jnp.zeros((bsz, GLA_HEADS, GLA_HEAD_K, GLA_HEAD_V), F32)
    zeros_s = (zero_s, zero_s)

    act_c = conv_silu(pc['xbc'], p['ssd_conv_w'], p['ssd_conv_b'])
    act_x = conv_silu(px['xbc'], p['ssd_conv_w'], p['ssd_conv_b'])
    y_c, h_c = ssd_scan(act_c, pc['dt'], p['ssd_a_log'], p['ssd_dt_bias'], zeros_h)
    y_x, _ = ssd_scan(act_x, px['dt'], p['ssd_a_log'], p['ssd_dt_bias'], h_c)
    a_x = ssd_finish(y_x, act_x, px['z'], p['ssd_d'], p['ssd_norm_g'])

    dh = NA_HEAD_DIM
    scale = dh ** -0.5

    def na_heads(t):
        t = _head_major(t, 3 * NA_HEADS)
        q = head_prep(_flat_heads(t[:, :NA_HEADS]), p['na_q_norm'], None, scale).reshape(t[:, :NA_HEADS].shape)
        k = head_prep(_flat_heads(t[:, NA_HEADS:2 * NA_HEADS]), p['na_k_norm'], None, 1.0).reshape(q.shape)
        return q, k, t[:, 2 * NA_HEADS:]

    nq, nk, nv = na_heads(px['na'])
    nqc, nkc, nvc = na_heads(pc['na'])
    b_x = na_attention(nq, nk, nv, nkc, nvc, na_bias_table(p['na_rpb']))
    b_x = b_x.transpose(0, 2, 1, 3).reshape(bsz, s, BRANCH_WIDTH)

    o_c, s_c = gla_scan(pc['gla_q'], pc['gla_k'], pc['gla_v'], pc['glr'], p['gla_w_gate'], p['gla_b_gate'], zeros_s)
    o_x, _ = gla_scan(px['gla_q'], px['gla_k'], px['gla_v'], px['glr'], p['gla_w_gate'], p['gla_b_gate'], s_c)
    c_x = gla_finish(o_x, px['gla_r'], p['gla_norm_g'])

    def swa_heads(q, k, v, tabs):
        qh = _head_major(q, SWA_HEADS)
        kh = _head_major(k, SWA_KV_HEADS)
        qn = head_prep(_flat_heads(qh), p['swa_q_norm'], tabs, SWA_HEAD_DIM ** -0.5).reshape(qh.shape)
        kn = head_prep(_flat_heads(kh), p['swa_k_norm'], tabs, 1.0).reshape(kh.shape)
        qn = qn.reshape(qh.shape[0], SWA_KV_HEADS, SWA_GROUP, qh.shape[2], SWA_HEAD_DIM)
        return qn, kn, _head_major(v, SWA_KV_HEADS)

    sq, sk, sv = swa_heads(px['swa_q'], px['swa_k'], px['swa_v'], rope)
    sqc, skc, svc = swa_heads(pc['swa_q'], pc['swa_k'], pc['swa_v'], None)
    d_x = swa_attention(sq, sk, sv, skc, svc, p['swa_sink'])
    d_x = d_x.reshape(bsz, SWA_HEADS, s, SWA_HEAD_DIM).transpose(0, 2, 1, 3).reshape(bsz, s, BRANCH_WIDTH)

    wb = p['w_branch'].astype(BF16)
    flat = lambda t: t.reshape(-1, t.shape[-1])
    m_x = merge_branches([flat(a_x), flat(b_x), flat(c_x), flat(d_x)], flat(px['gates']), wb)
    if not need_ctx:
        return m_x, None

    a_c = ssd_finish(y_c, act_c, pc['z'], p['ssd_d'], p['ssd_norm_g'])
    no_sink = jnp.full((NA_HEADS,), NEG, F32)
    b_c = ctx_attention(nqc[:, :, None], nkc, nvc, no_sink)[:, :, 0]
    b_c = b_c.transpose(0, 2, 1, 3).reshape(bsz, lc, BRANCH_WIDTH)
    c_c = gla_finish(o_c, pc['gla_r'], p['gla_norm_g'])
    d_c = ctx_attention(sqc, skc, svc, p['swa_sink'])
    d_c = d_c.reshape(bsz, SWA_HEADS, lc, SWA_HEAD_DIM).transpose(0, 2, 1, 3).reshape(bsz, lc, BRANCH_WIDTH)
    m_c = merge_branches([flat(a_c), flat(b_c), flat(c_c), flat(d_c)], flat(pc['gates']), wb)
    return m_x, m_c


def kernel(x, c, ctx, c_ctx, w_ada, b_ada, g_norm1, g_norm2, w_in, ssd_conv_w, ssd_conv_b, ssd_a_log, ssd_dt_bias, ssd_d, ssd_norm_g, na_q_norm, na_k_norm, na_rpb, gla_w_gate, gla_b_gate, gla_norm_g, swa_q_norm, swa_k_norm, swa_sink, w_branch, w_out, peer_wq, peer_k1, peer_k2, peer_u, peer_v):
    bsz, s, dm = x.shape
    rope = rope_tables(s)
    n_cond = 8
    cc = jnp.zeros((n_cond, dm), F32).at[:bsz].set(c).at[bsz].set(c_ctx)
    for l in range(DEPTH):
        need_ctx = l < DEPTH - 1
        mod = ada_mod(cc, w_ada[l], b_ada[l])
        mx = [mod[:bsz, i * dm:(i + 1) * dm].reshape(bsz, 1, dm) for i in range(6)]
        mc = [mod[bsz:bsz + 1, i * dm:(i + 1) * dm].reshape(1, 1, dm) for i in range(6)]
        p = dict(w_in=w_in[l], ssd_conv_w=ssd_conv_w[l], ssd_conv_b=ssd_conv_b[l], ssd_a_log=ssd_a_log[l],
                 ssd_dt_bias=ssd_dt_bias[l], ssd_d=ssd_d[l], ssd_norm_g=ssd_norm_g[l],
                 na_q_norm=na_q_norm[l], na_k_norm=na_k_norm[l], na_rpb=na_rpb[l],
                 gla_w_gate=gla_w_gate[l], gla_b_gate=gla_b_gate[l], gla_norm_g=gla_norm_g[l],
                 swa_q_norm=swa_q_norm[l], swa_k_norm=swa_k_norm[l], swa_sink=swa_sink[l],
                 w_branch=w_branch[l])
        (hx,) = norm_modulate(x, g_norm1[l], mx[0], mx[1], (BF16,))
        (hc,) = norm_modulate(ctx, g_norm1[l], mc[0], mc[1], (BF16,))
        m_x, m_c = mixer_sublayer(hx, hc, p, rope, need_ctx)
        wo = w_out[l].astype(BF16)
        wq = peer_wq[l].astype(BF16)
        uv = pack_expert_tables(peer_u[l], peer_v[l])
        x = matmul_gated_residual(m_x, wo, x, mx[2])
        x = peer_sublayer(x, g_norm2[l], mx[3], mx[4], mx[5], wq, peer_k1[l], peer_k2[l], uv)
        if need_ctx:
            ctx = matmul_gated_residual(m_c, wo, ctx, mc[2])
            ctx = peer_sublayer(ctx, g_norm2[l], mc[3], mc[4], mc[5], wq, peer_k1[l], peer_k2[l], uv)
    return x
```

```python
import functools
import math

import numpy as np
import jax
import jax.numpy as jnp
from jax import lax
from jax.experimental import pallas as pl
from jax.experimental.pallas import tpu as pltpu

D_MODEL = 2048
DEPTH = 2
GRID_W = 64
NORM_EPS = 1e-6
N_BRANCH = 4
BRANCH_WIDTH = D_MODEL // 2

SSD_HEADDIM = 64
SSD_D_INNER = BRANCH_WIDTH
SSD_HEADS = SSD_D_INNER // SSD_HEADDIM
SSD_GROUPS = 4
SSD_STATE = 128
SSD_CONV_DIM = SSD_D_INNER + 2 * SSD_GROUPS * SSD_STATE
SSD_CONV_K = 5
SSD_CHUNK = 128

NA_HEAD_DIM = 64
NA_HEADS = BRANCH_WIDTH // NA_HEAD_DIM
NA_WIN_R = 8
NA_WIN_C = 16

GLA_HEADS = 4
GLA_V_DIM = BRANCH_WIDTH
GLA_K_DIM = BRANCH_WIDTH // 2
GLA_HEAD_K = GLA_K_DIM // GLA_HEADS
GLA_HEAD_V = GLA_V_DIM // GLA_HEADS
GLA_GATE_RANK = 16
GLA_TAU = 16.0
GLA_CHUNK = 64

SWA_HEAD_DIM = 64
SWA_HEADS = BRANCH_WIDTH // SWA_HEAD_DIM
SWA_KV_HEADS = 4
SWA_GROUP = SWA_HEADS // SWA_KV_HEADS
SWA_WINDOW = 128
SWA_BLOCK = 128
ROPE_BASE = 10000.0
ROPE_AXIS_DIM = SWA_HEAD_DIM // 2

PEER_HEADS = 8
PEER_NKEYS = 128
PEER_QDIM = 256
PEER_TOPK = 16
PEER_NSEL = PEER_HEADS * PEER_TOPK

IN_SPLITS = (SSD_D_INNER, SSD_CONV_DIM, 2 * SSD_HEADS,
             3 * BRANCH_WIDTH,
             GLA_K_DIM, GLA_K_DIM, GLA_V_DIM, GLA_V_DIM, 2 * GLA_GATE_RANK,
             SWA_HEADS * SWA_HEAD_DIM, SWA_KV_HEADS * SWA_HEAD_DIM, SWA_KV_HEADS * SWA_HEAD_DIM,
             N_BRANCH * D_MODEL)
IN_OFFS = tuple(int(v) for v in np.cumsum((0,) + IN_SPLITS))

V7X_LANES = 128
V7X_VMEM_BYTES = 64 * 1024 * 1024
VMEM_LIMIT = 48 * 1024 * 1024

F32 = jnp.float32
BF16 = jnp.bfloat16
HI = lax.Precision.HIGHEST
NEG = -1e30


def _cparams(sem):
    return pltpu.CompilerParams(dimension_semantics=sem, vmem_limit_bytes=VMEM_LIMIT)


def _dot(a, b, precision=None):
    return jnp.dot(a, b, preferred_element_type=F32, precision=precision)


def _dot_nt(a, b, precision=None):
    return lax.dot_general(a, b, (((1,), (1,)), ((), ())), preferred_element_type=F32,
                           precision=precision)


def _silu(x):
    return x / (1.0 + jnp.exp(-x))


def _softplus(x):
    return jnp.maximum(x, 0.0) + jnp.log1p(jnp.exp(-jnp.abs(x)))


def _log_sigmoid(x):
    return jnp.minimum(x, 0.0) - jnp.log1p(jnp.exp(-jnp.abs(x)))


def _mm_kernel(a_ref, b_ref, o_ref):
    o_ref[...] = _dot(a_ref[...], b_ref[...]).astype(o_ref.dtype)


def matmul(a, b, out_dtype, tm=1024, tn=512):
    m, k = a.shape
    n = b.shape[1]
    tm = min(tm, m)
    tn = min(tn, n)
    assert m % tm == 0 and n % tn == 0, (m, n, tm, tn)
    return pl.pallas_call(
        _mm_kernel,
        grid=(m // tm, n // tn),
        in_specs=[pl.BlockSpec((tm, k), lambda i, j: (i, 0)),
                  pl.BlockSpec((k, tn), lambda i, j: (0, j))],
        out_specs=pl.BlockSpec((tm, tn), lambda i, j: (i, j)),
        out_shape=jax.ShapeDtypeStruct((m, n), out_dtype),
        compiler_params=_cparams(("parallel", "parallel")),
        name="matmul",
    )(a, b)


def _mod_kernel(c_ref, w_ref, b_ref, o_ref):
    a = _silu(c_ref[...]).astype(BF16)
    o_ref[...] = _dot(a, w_ref[...].astype(BF16)) + b_ref[...]


def ada_mod(cc, w, b, tn=1024):
    m, k = cc.shape
    n = w.shape[1]
    return pl.pallas_call(
        _mod_kernel,
        grid=(n // tn,),
        in_specs=[pl.BlockSpec((m, k), lambda j: (0, 0)),
                  pl.BlockSpec((k, tn), lambda j: (0, j)),
                  pl.BlockSpec((1, tn), lambda j: (0, j))],
        out_specs=pl.BlockSpec((m, tn), lambda j: (0, j)),
        out_shape=jax.ShapeDtypeStruct((m, n), F32),
        compiler_params=_cparams(("parallel",)),
        name="ada_mod",
    )(cc, w, b.reshape(1, n))


def _normmod_kernel(x_ref, g_ref, sh_ref, sc_ref, *o_refs):
    x = x_ref[0]
    var = jnp.mean(x * x, axis=-1, keepdims=True)
    y = x * lax.rsqrt(var + NORM_EPS) * g_ref[...]
    y = y * (1.0 + sc_ref[0]) + sh_ref[0]
    for o_ref in o_refs:
        o_ref[0] = y.astype(o_ref.dtype)


def norm_modulate(x, g, shift, scale, out_dtypes, ts=512):
    bsz, s, d = x.shape
    ts = min(ts, s)
    per_batch = shift.shape[0] == bsz and bsz > 1
    mod_map = (lambda b, i: (b, 0, 0)) if per_batch else (lambda b, i: (0, 0, 0))
    outs = pl.pallas_call(
        _normmod_kernel,
        grid=(bsz, s // ts),
        in_specs=[pl.BlockSpec((1, ts, d), lambda b, i: (b, i, 0)),
                  pl.BlockSpec((1, d), lambda b, i: (0, 0)),
                  pl.BlockSpec((1, 1, d), mod_map),
                  pl.BlockSpec((1, 1, d), mod_map)],
        out_specs=[pl.BlockSpec((1, ts, d), lambda b, i: (b, i, 0)) for _ in out_dtypes],
        out_shape=[jax.ShapeDtypeStruct((bsz, s, d), dt) for dt in out_dtypes],
        compiler_params=_cparams(("parallel", "parallel")),
        name="norm_modulate",
    )(x, g.reshape(1, d), shift, scale)
    return outs


def _mm_resid_kernel(a_ref, w_ref, x_ref, gt_ref, o_ref):
    y = _dot(a_ref[...], w_ref[...])
    o_ref[0] = x_ref[0] + gt_ref[0] * y


def matmul_gated_residual(a, w, x, gate, tm=1024, tn=512):
    bsz, s, n = x.shape
    k = a.shape[1]
    tm = min(tm, s)
    nt = s // tm
    per_batch = gate.shape[0] == bsz and bsz > 1
    g_map = (lambda b, i, j: (b, 0, j)) if per_batch else (lambda b, i, j: (0, 0, j))
    return pl.pallas_call(
        _mm_resid_kernel,
        grid=(bsz, nt, n // tn),
        in_specs=[pl.BlockSpec((tm, k), lambda b, i, j: (b * nt + i, 0)),
                  pl.BlockSpec((k, tn), lambda b, i, j: (0, j)),
                  pl.BlockSpec((1, tm, tn), lambda b, i, j: (b, i, j)),
                  pl.BlockSpec((1, 1, tn), g_map)],
        out_specs=pl.BlockSpec((1, tm, tn), lambda b, i, j: (b, i, j)),
        out_shape=jax.ShapeDtypeStruct((bsz, s, n), F32),
        compiler_params=_cparams(("parallel", "parallel", "parallel")),
        name="out_proj_residual",
    )(a, w, x, gate)


def _merge_kernel(a_ref, b_ref, c_ref, d_ref, g0_ref, g1_ref, g2_ref, g3_ref, w_ref, o_ref):
    acc = None
    for i, (br, gr) in enumerate(((a_ref, g0_ref), (b_ref, g1_ref), (c_ref, g2_ref), (d_ref, g3_ref))):
        y = _dot(br[...], w_ref[i])
        gate = 1.0 / (1.0 + jnp.exp(-gr[...].astype(F32)))
        acc = gate * y if acc is None else acc + gate * y
    o_ref[...] = acc.astype(o_ref.dtype)


def merge_branches(outs, gate_logits, w_branch, tm=512, tn=512):
    t, kb = outs[0].shape
    d = w_branch.shape[-1]
    tm = min(tm, t)
    nj = d // tn
    in_specs = [pl.BlockSpec((tm, kb), lambda i, j: (i, 0)) for _ in range(N_BRANCH)]
    in_specs += [pl.BlockSpec((tm, tn), functools.partial(lambda i, j, br: (i, br * nj + j), br=br))
                 for br in range(N_BRANCH)]
    in_specs += [pl.BlockSpec((N_BRANCH, kb, tn), lambda i, j: (0, 0, j))]
    return pl.pallas_call(
        _merge_kernel,
        grid=(t // tm, nj),
        in_specs=in_specs,
        out_specs=pl.BlockSpec((tm, tn), lambda i, j: (i, j)),
        out_shape=jax.ShapeDtypeStruct((t, d), BF16),
        compiler_params=_cparams(("parallel", "parallel")),
        name="merge_branches",
    )(*outs, gate_logits, gate_logits, gate_logits, gate_logits, w_branch)


def _headprep_kernel(x_ref, g_ref, cos_ref, sin_ref, rot_ref, o_ref, *, rope, scale):
    x = x_ref[0].astype(F32)
    var = jnp.mean(x * x, axis=-1, keepdims=True)
    y = x * lax.rsqrt(var + NORM_EPS) * g_ref[...]
    if rope:
        y = y * cos_ref[...] + _dot(y, rot_ref[...], HI) * sin_ref[...]
    o_ref[0] = (y * scale).astype(o_ref.dtype)


def head_prep(x, g, rope_tabs, scale, ts=1024):
    n, s, dh = x.shape
    ts = min(ts, s)
    rope = rope_tabs is not None
    if rope:
        cos, sin, rot = rope_tabs
    else:
        cos = sin = jnp.zeros((s, dh), F32)
        rot = jnp.zeros((dh, dh), F32)
    return pl.pallas_call(
        functools.partial(_headprep_kernel, rope=rope, scale=scale),
        grid=(n, s // ts),
        in_specs=[pl.BlockSpec((1, ts, dh), lambda h, i: (h, i, 0)),
                  pl.BlockSpec((1, dh), lambda h, i: (0, 0)),
                  pl.BlockSpec((ts, dh), lambda h, i: (i, 0)),
                  pl.BlockSpec((ts, dh), lambda h, i: (i, 0)),
                  pl.BlockSpec((dh, dh), lambda h, i: (0, 0))],
        out_specs=pl.BlockSpec((1, ts, dh), lambda h, i: (h, i, 0)),
        out_shape=jax.ShapeDtypeStruct((n, s, dh), BF16),
        compiler_params=_cparams(("parallel", "parallel")),
        name="head_prep",
    )(x, g.reshape(1, dh), cos, sin, rot)


def rope_tables(s):
    t = np.arange(s)
    row = (t // GRID_W).astype(np.float32)
    col = (t % GRID_W).astype(np.float32)
    nf = ROPE_AXIS_DIM // 2
    inv = jnp.asarray(ROPE_BASE, F32) ** (-jnp.arange(nf, dtype=F32) / nf)
    ar = jnp.asarray(row)[:, None] * inv
    ac = jnp.asarray(col)[:, None] * inv
    cos = jnp.concatenate([jnp.cos(ar), jnp.cos(ar), jnp.cos(ac), jnp.cos(ac)], axis=-1)
    sin = jnp.concatenate([jnp.sin(ar), jnp.sin(ar), jnp.sin(ac), jnp.sin(ac)], axis=-1)
    rot = np.zeros((SWA_HEAD_DIM, SWA_HEAD_DIM), np.float32)
    for d in range(SWA_HEAD_DIM):
        if d % ROPE_AXIS_DIM < nf:
            rot[d + nf, d] = -1.0
        else:
            rot[d - nf, d] = 1.0
    return cos, sin, jnp.asarray(rot)


NA_ROWS_PER_STEP = 8


def _pair_head_norm(x, g, lane_lo):
    sq = x * x
    lo = jnp.sum(jnp.where(lane_lo, sq, 0.0), axis=-1, keepdims=True)
    hi = jnp.sum(sq, axis=-1, keepdims=True) - lo
    inv = jnp.where(lane_lo, lax.rsqrt(lo * (1.0 / NA_HEAD_DIM) + NORM_EPS),
                    lax.rsqrt(hi * (1.0 / NA_HEAD_DIM) + NORM_EPS))
    return x * inv * g


def _na_kernel(q_ref, k_ref, v_ref, kc_ref, vc_ref, qg_ref, kg_ref, bias_ref, o_ref, kn_ref, kcn_ref, *, n_rows, seq):
    i = pl.program_id(2)
    n_nb = NA_WIN_R * GRID_W
    width = 2 * NA_HEAD_DIM
    norm_rows = min(seq, 512)

    @pl.when(i == 0)
    def _():
        lane_lo = lax.broadcasted_iota(jnp.int32, (norm_rows, width), 1) < NA_HEAD_DIM

        def nbody(t, carry):
            off = pl.multiple_of(t * norm_rows, norm_rows)
            x = k_ref[0, pl.ds(off, norm_rows), :].astype(F32)
            kn_ref[pl.ds(off, norm_rows), :] = _pair_head_norm(x, kg_ref[...], lane_lo).astype(BF16)
            return carry

        lax.fori_loop(0, seq // norm_rows, nbody, 0)
        lane_lo_c = lax.broadcasted_iota(jnp.int32, kcn_ref.shape, 1) < NA_HEAD_DIM
        kcn_ref[...] = _pair_head_norm(kc_ref[0].astype(F32), kg_ref[...], lane_lo_c).astype(BF16)

    kc = kcn_ref[...]
    vc = vc_ref[0]
    lane_lo = lax.broadcasted_iota(jnp.int32, (GRID_W, width), 1) < NA_HEAD_DIM
    scale = NA_HEAD_DIM ** -0.5

    def body(rr, carry):
        r = i * NA_ROWS_PER_STEP + rr
        r0 = jnp.clip(r - NA_WIN_R // 2, 0, n_rows - NA_WIN_R)
        qoff = pl.multiple_of(rr * GRID_W, GRID_W)
        koff = pl.multiple_of(r0 * GRID_W, GRID_W)
        q = _pair_head_norm(q_ref[0, pl.ds(qoff, GRID_W), :].astype(F32), qg_ref[...], lane_lo) * scale
        kn = kn_ref[pl.ds(koff, n_nb), :]
        vn = v_ref[0, pl.ds(koff, n_nb), :]
        outs = []
        for e in range(2):
            qe = jnp.where(lane_lo if e == 0 else jnp.logical_not(lane_lo), q, 0.0).astype(BF16)
            s_nb = _dot_nt(qe, kn) + bias_ref[r - r0, e]
            s_cx = _dot_nt(qe, kc)
            m = jnp.maximum(jnp.max(s_nb, axis=-1, keepdims=True), jnp.max(s_cx, axis=-1, keepdims=True))
            p_nb = jnp.exp(s_nb - m)
            p_cx = jnp.exp(s_cx - m)
            den = jnp.sum(p_nb, axis=-1, keepdims=True) + jnp.sum(p_cx, axis=-1, keepdims=True)
            outs.append((_dot(p_nb.astype(BF16), vn) + _dot(p_cx.astype(BF16), vc)) / den)
        o_ref[0, pl.ds(qoff, GRID_W), :] = jnp.where(lane_lo, outs[0], outs[1]).astype(o_ref.dtype)
        return carry

    lax.fori_loop(0, NA_ROWS_PER_STEP, body, 0, unroll=True)


def na_bias_table(rpb):
    nh = rpb.shape[0]
    wr, wc, gw = NA_WIN_R, NA_WIN_C, GRID_W
    rpb = rpb.astype(F32)

    def toeplitz(vec, n, period):
        tiled = jnp.broadcast_to(vec[..., None, :], vec.shape[:-1] + (n, period))
        flat = tiled.reshape(vec.shape[:-1] + (n * period,))[..., :n * (period - 1)]
        return flat.reshape(vec.shape[:-1] + (n, period - 1))[..., :n]

    rows = jnp.concatenate([rpb[:, wr - 1:], jnp.zeros((nh, 1, 2 * wc - 1), F32), rpb[:, :wr - 1]], axis=1)
    tab = toeplitz(jnp.moveaxis(rows, 1, -1), wr, 2 * wr)
    tab = jnp.moveaxis(tab, 1, -1)
    fill = jnp.zeros(tab.shape[:-1] + (2 * gw - (2 * wc - 1),), F32)
    cols_ext = jnp.concatenate([tab[..., wc - 1:], fill, tab[..., :wc - 1]], axis=-1)
    tab = toeplitz(cols_ext, gw, 2 * gw)
    cols = np.arange(gw)
    c_start = np.clip(cols - wc // 2, 0, gw - wc)
    valid = (cols[None, :] >= c_start[:, None]) & (cols[None, :] < c_start[:, None] + wc)
    tab = jnp.where(jnp.asarray(valid)[None, None, None], tab, NEG)
    tab = tab.transpose(1, 0, 3, 2, 4)
    return tab.reshape(wr, nh, gw, wr * gw)


def na_attention(qkv, qkv_c, q_norm, k_norm, bias):
    bsz, s, _ = qkv.shape
    lc = qkv_c.shape[1]
    n_rows = s // GRID_W
    tq = NA_ROWS_PER_STEP * GRID_W
    n_nb = NA_WIN_R * GRID_W
    width = 2 * NA_HEAD_DIM
    n_pairs = NA_HEADS // 2
    gq = jnp.tile(q_norm.astype(F32), 2).reshape(1, width)
    gk = jnp.tile(k_norm.astype(F32), 2).reshape(1, width)
    return pl.pallas_call(
        functools.partial(_na_kernel, n_rows=n_rows, seq=s),
        grid=(bsz, n_pairs, s // tq),
        in_specs=[pl.BlockSpec((1, tq, width), lambda b, p, i: (b, i, p)),
                  pl.BlockSpec((1, s, width), lambda b, p, i: (b, 0, n_pairs + p)),
                  pl.BlockSpec((1, s, width), lambda b, p, i: (b, 0, 2 * n_pairs + p)),
                  pl.BlockSpec((1, lc, width), lambda b, p, i: (b, 0, n_pairs + p)),
                  pl.BlockSpec((1, lc, width), lambda b, p, i: (b, 0, 2 * n_pairs + p)),
                  pl.BlockSpec((1, width), lambda b, p, i: (0, 0)),
                  pl.BlockSpec((1, width), lambda b, p, i: (0, 0)),
                  pl.BlockSpec((NA_WIN_R, 2, GRID_W, n_nb), lambda b, p, i: (0, p, 0, 0))],
        out_specs=pl.BlockSpec((1, tq, width), lambda b, p, i: (b, i, p)),
        out_shape=jax.ShapeDtypeStruct((bsz, s, NA_HEADS * NA_HEAD_DIM), BF16),
        scratch_shapes=[pltpu.VMEM((s, width), BF16), pltpu.VMEM((lc, width), BF16)],
        compiler_params=_cparams(("parallel", "parallel", "arbitrary")),
        name="na_attention",
    )(qkv, qkv, qkv, qkv_c, qkv_c, gq, gk, bias)


def _swa_kernel(sink_ref, q_ref, k_ref, v_ref, kc_ref, vc_ref, o_ref, *, seq):
    kh = pl.program_id(1)
    n = pl.program_id(2)
    span = 3 * SWA_BLOCK
    start = pl.multiple_of(jnp.clip((n - 1) * SWA_BLOCK, 0, seq - span), SWA_BLOCK)
    kw = k_ref[0, 0, pl.ds(start, span), :]
    vw = v_ref[0, 0, pl.ds(start, span), :]
    kc = kc_ref[0, 0]
    vc = vc_ref[0, 0]
    qpos = n * SWA_BLOCK + lax.broadcasted_iota(jnp.int32, (SWA_BLOCK, span), 0)
    kpos = start + lax.broadcasted_iota(jnp.int32, (SWA_BLOCK, span), 1)
    valid = jnp.abs(qpos - kpos) <= SWA_WINDOW
    for g in range(SWA_GROUP):
        q = q_ref[0, 0, g]
        s_loc = jnp.where(valid, _dot_nt(q, kw), NEG)
        s_ctx = _dot_nt(q, kc)
        sink = sink_ref[kh * SWA_GROUP + g]
        m = jnp.maximum(jnp.max(s_loc, axis=-1, keepdims=True), jnp.max(s_ctx, axis=-1, keepdims=True))
        m = jnp.maximum(m, sink)
        p_loc = jnp.exp(s_loc - m)
        p_ctx = jnp.exp(s_ctx - m)
        den = (jnp.sum(p_loc, axis=-1, keepdims=True) + jnp.sum(p_ctx, axis=-1, keepdims=True)
               + jnp.exp(sink - m))
        o = _dot(p_loc.astype(BF16), vw) + _dot(p_ctx.astype(BF16), vc)
        o_ref[0, 0, g] = (o / den).astype(o_ref.dtype)


def swa_attention(q, k, v, kc, vc, sink):
    bsz, hk, grp, s, dh = q.shape
    lc = kc.shape[2]
    assert s >= 3 * SWA_BLOCK
    return pl.pallas_call(
        functools.partial(_swa_kernel, seq=s),
        grid=(bsz, hk, s // SWA_BLOCK),
        in_specs=[pl.BlockSpec(memory_space=pltpu.SMEM),
                  pl.BlockSpec((1, 1, grp, SWA_BLOCK, dh), lambda b, h, n: (b, h, 0, n, 0)),
                  pl.BlockSpec((1, 1, s, dh), lambda b, h, n: (b, h, 0, 0)),
                  pl.BlockSpec((1, 1, s, dh), lambda b, h, n: (b, h, 0, 0)),
                  pl.BlockSpec((1, 1, lc, dh), lambda b, h, n: (b, h, 0, 0)),
                  pl.BlockSpec((1, 1, lc, dh), lambda b, h, n: (b, h, 0, 0))],
        out_specs=pl.BlockSpec((1, 1, grp, SWA_BLOCK, dh), lambda b, h, n: (b, h, 0, n, 0)),
        out_shape=jax.ShapeDtypeStruct((bsz, hk, grp, s, dh), BF16),
        compiler_params=_cparams(("parallel", "parallel", "arbitrary")),
        name="swa_attention",
    )(sink.astype(F32), q, k, v, kc, vc)


def _ctx_attn_kernel(sink_ref, q_ref, k_ref, v_ref, o_ref, *, grp):
    kh = pl.program_id(1)
    k = k_ref[0, 0]
    v = v_ref[0, 0]
    for g in range(grp):
        q = q_ref[0, 0, g]
        s = _dot_nt(q, k)
        sink = sink_ref[kh * grp + g]
        m = jnp.maximum(jnp.max(s, axis=-1, keepdims=True), sink)
        p = jnp.exp(s - m)
        den = jnp.sum(p, axis=-1, keepdims=True) + jnp.exp(sink - m)
        o_ref[0, 0, g] = (_dot(p.astype(BF16), v) / den).astype(o_ref.dtype)


def ctx_attention(q, k, v, sink):
    bsz, hk, grp, n, dh = q.shape
    return pl.pallas_call(
        functools.partial(_ctx_attn_kernel, grp=grp),
        grid=(bsz, hk),
        in_specs=[pl.BlockSpec(memory_space=pltpu.SMEM),
                  pl.BlockSpec((1, 1, grp, n, dh), lambda b, h: (b, h, 0, 0, 0)),
                  pl.BlockSpec((1, 1, n, dh), lambda b, h: (b, h, 0, 0)),
                  pl.BlockSpec((1, 1, n, dh), lambda b, h: (b, h, 0, 0))],
        out_specs=pl.BlockSpec((1, 1, grp, n, dh), lambda b, h: (b, h, 0, 0, 0)),
        out_shape=jax.ShapeDtypeStruct((bsz, hk, grp, n, dh), BF16),
        compiler_params=_cparams(("parallel", "parallel")),
        name="ctx_attention",
    )(sink.astype(F32), q, k, v)


CONV_PAD = 8
CONV_ROWS = 256
CONV_COLS = 256


def _conv_kernel(x_ref, w_ref, b_ref, o_ref, *, seq):
    rows = min(CONV_ROWS, seq)
    n = seq // rows

    def body(i, carry):
        base = pl.multiple_of(i * rows, rows)
        acc = jnp.zeros((rows, CONV_COLS), F32) + b_ref[...]
        prev_off = pl.multiple_of(jnp.maximum(base - CONV_PAD, 0), CONV_PAD)
        next_off = pl.multiple_of(jnp.minimum(base + rows, seq - CONV_PAD), CONV_PAD)
        prev = jnp.where(i > 0, x_ref[0, pl.ds(prev_off, CONV_PAD), :], 0.0)
        nxt = jnp.where(i < n - 1, x_ref[0, pl.ds(next_off, CONV_PAD), :], 0.0)
        halo = jnp.concatenate([prev, x_ref[0, pl.ds(base, rows), :], nxt], axis=0)
        for k in range(SSD_CONV_K):
            off = CONV_PAD + k - SSD_CONV_K // 2
            acc = acc + w_ref[k:k + 1, :] * halo[off:off + rows, :]
        o_ref[0, pl.ds(base, rows), :] = _silu(acc).astype(o_ref.dtype)
        return carry

    lax.fori_loop(0, seq // rows, body, 0)


def conv_silu(xbc, w, b):
    bsz, s, c = xbc.shape
    return pl.pallas_call(
        functools.partial(_conv_kernel, seq=s),
        grid=(bsz, c // CONV_COLS),
        in_specs=[pl.BlockSpec((1, s, CONV_COLS), lambda bb, j: (bb, 0, j)),
                  pl.BlockSpec((SSD_CONV_K, CONV_COLS), lambda bb, j: (0, j)),
                  pl.BlockSpec((1, CONV_COLS), lambda bb, j: (0, j))],
        out_specs=pl.BlockSpec((1, s, CONV_COLS), lambda bb, j: (bb, 0, j)),
        out_shape=jax.ShapeDtypeStruct((bsz, s, c), BF16),
        compiler_params=_cparams(("parallel", "parallel")),
        name="conv_silu",
    )(xbc, w.astype(F32), b.reshape(1, c).astype(F32))


def _scan_masks(length, d):
    row = lax.broadcasted_iota(jnp.int32, (length, length), 0)
    col = lax.broadcasted_iota(jnp.int32, (length, length), 1)
    return (row - col) * (1 - 2 * d) >= 0


def _ssd_kernel(xs_ref, bm_ref, cm_ref, bmt_ref, dt_ref, dtt_ref, alog_ref, alogt_ref, dtb_ref, dtbt_ref,
                h0_ref, y_ref, hout_ref, state_ref, *, n_chunks):
    d = pl.program_id(0)
    c = pl.program_id(2)

    @pl.when(c == 0)
    def _():
        state_ref[...] = h0_ref[0, 0]

    length = SSD_CHUNK
    incl = _scan_masks(length, d)
    tri = incl.astype(F32)
    dt = _softplus(dt_ref[0, 0] + dtb_ref[0])
    dtt = _softplus(dtt_ref[0, 0] + dtbt_ref[0])
    ad = dt * (-jnp.exp(alog_ref[0]))
    adt = dtt * (-jnp.exp(alogt_ref[0]))
    acs = _dot(tri, ad, HI)
    acst = _dot_nt(adt, tri, HI)
    tot = jnp.sum(ad, axis=0, keepdims=True)
    tott = jnp.sum(adt, axis=1, keepdims=True)
    rep = SSD_HEADS // SSD_GROUPS
    for gi in range(SSD_GROUPS):
        gsl = slice(gi * SSD_STATE, (gi + 1) * SSD_STATE)
        cg = cm_ref[0, :, gsl]
        bg = bm_ref[0, :, gsl]
        bgt = bmt_ref[0, gsl, :].astype(F32)
        cb = _dot_nt(cg, bg)
        for hh in range(rep):
            h = gi * rep + hh
            psl = slice(h * SSD_HEADDIM, (h + 1) * SSD_HEADDIM)
            a_col = acs[:, h:h + 1]
            a_row = acst[h:h + 1, :]
            lmat = jnp.exp(jnp.where(incl, a_col - a_row, NEG))
            xdt = (xs_ref[0, :, psl].astype(F32) * dt[:, h:h + 1]).astype(BF16)
            st = state_ref[h]
            y = _dot((cb * lmat).astype(BF16), xdt)
            y = y + jnp.exp(a_col) * _dot(cg, st.astype(BF16))
            dec = jnp.exp(tott[h:h + 1, :] - a_row)
            state_ref[h] = jnp.exp(tot[:, h:h + 1]) * st + _dot((bgt * dec).astype(BF16), xdt)
            y_ref[0, 0, :, psl] = y.astype(y_ref.dtype)

    @pl.when(c == n_chunks - 1)
    def _():
        hout_ref[0, 0] = state_ref[...]


def _chunk_index(d, c, n_chunks):
    return c + d * (n_chunks - 1 - 2 * c)


def ssd_scan(xbc_act, dt_raw, a_log, dt_bias, h0):
    bsz, s, _ = xbc_act.shape
    nc = s // SSD_CHUNK
    hh = SSD_HEADS
    ng = SSD_GROUPS * SSD_STATE
    bmt = jnp.swapaxes(xbc_act[:, :, SSD_D_INNER:SSD_D_INNER + ng], 1, 2)
    dt2 = dt_raw.reshape(bsz, s, 2, hh).transpose(2, 0, 1, 3)
    dtt = dt2.transpose(0, 1, 3, 2)
    a_log = a_log.astype(F32)
    dt_bias = dt_bias.astype(F32)
    cmap = functools.partial(_chunk_index, n_chunks=nc)
    nb_x = SSD_D_INNER // ng
    return pl.pallas_call(
        functools.partial(_ssd_kernel, n_chunks=nc),
        grid=(2, bsz, nc),
        in_specs=[pl.BlockSpec((1, SSD_CHUNK, SSD_D_INNER), lambda d, b, c: (b, cmap(d, c), 0)),
                  pl.BlockSpec((1, SSD_CHUNK, ng), lambda d, b, c: (b, cmap(d, c), nb_x)),
                  pl.BlockSpec((1, SSD_CHUNK, ng), lambda d, b, c: (b, cmap(d, c), nb_x + 1)),
                  pl.BlockSpec((1, ng, SSD_CHUNK), lambda d, b, c: (b, 0, cmap(d, c))),
                  pl.BlockSpec((1, 1, SSD_CHUNK, hh), lambda d, b, c: (d, b, cmap(d, c), 0)),
                  pl.BlockSpec((1, 1, hh, SSD_CHUNK), lambda d, b, c: (d, b, 0, cmap(d, c))),
                  pl.BlockSpec((1, 1, hh), lambda d, b, c: (d, 0, 0)),
                  pl.BlockSpec((1, hh, 1), lambda d, b, c: (d, 0, 0)),
                  pl.BlockSpec((1, 1, hh), lambda d, b, c: (d, 0, 0)),
                  pl.BlockSpec((1, hh, 1), lambda d, b, c: (d, 0, 0)),
                  pl.BlockSpec((1, 1, hh, SSD_STATE, SSD_HEADDIM), lambda d, b, c: (d, b, 0, 0, 0))],
        out_specs=[pl.BlockSpec((1, 1, SSD_CHUNK, SSD_D_INNER), lambda d, b, c: (d, b, cmap(d, c), 0)),
                   pl.BlockSpec((1, 1, hh, SSD_STATE, SSD_HEADDIM), lambda d, b, c: (d, b, 0, 0, 0))],
        out_shape=[jax.ShapeDtypeStruct((2, bsz, s, SSD_D_INNER), BF16),
                   jax.ShapeDtypeStruct((2, bsz, hh, SSD_STATE, SSD_HEADDIM), F32)],
        scratch_shapes=[pltpu.VMEM((hh, SSD_STATE, SSD_HEADDIM), F32)],
        compiler_params=_cparams(("parallel", "parallel", "arbitrary")),
        name="ssd_scan",
    )(xbc_act, xbc_act, xbc_act, bmt, dt2, dtt,
      a_log.reshape(2, 1, hh), a_log.reshape(2, hh, 1), dt_bias.reshape(2, 1, hh), dt_bias.reshape(2, hh, 1), h0)


def _ssd_finish_kernel(y_ref, xs_ref, z_ref, d_ref, g_ref, o_ref):
    y = y_ref[0, 0].astype(F32) + y_ref[1, 0].astype(F32) + d_ref[...] * xs_ref[0].astype(F32)
    z = z_ref[0].astype(F32)
    u = y * _silu(z)
    var = jnp.mean(u * u, axis=-1, keepdims=True)
    o_ref[0] = (u * lax.rsqrt(var + NORM_EPS) * g_ref[...]).astype(o_ref.dtype)


def ssd_finish(y, xbc_act, z, d_skip, norm_g, ts=512):
    _, bsz, s, di = y.shape
    ts = min(ts, s)
    dvec = jnp.repeat(d_skip.astype(F32), SSD_HEADDIM).reshape(1, di)
    return pl.pallas_call(
        _ssd_finish_kernel,
        grid=(bsz, s // ts),
        in_specs=[pl.BlockSpec((2, 1, ts, di), lambda b, i: (0, b, i, 0)),
                  pl.BlockSpec((1, ts, di), lambda b, i: (b, i, 0)),
                  pl.BlockSpec((1, ts, di), lambda b, i: (b, i, 0)),
                  pl.BlockSpec((1, di), lambda b, i: (0, 0)),
                  pl.BlockSpec((1, di), lambda b, i: (0, 0))],
        out_specs=pl.BlockSpec((1, ts, di), lambda b, i: (b, i, 0)),
        out_shape=jax.ShapeDtypeStruct((bsz, s, di), BF16),
        compiler_params=_cparams(("parallel", "parallel")),
        name="ssd_finish",
    )(y, xbc_act, z, dvec, norm_g.reshape(1, di).astype(F32))


GLA_STEP = 128


def _gla_kernel(q_ref, k_ref, kt_ref, v_ref, glr_ref, glrt_ref, wg_ref, wgt_ref, bg_ref, bgt_ref,
                s0_ref, o_ref, sout_ref, state_ref, *, n_steps, reverse):
    c = pl.program_id(1)

    @pl.when(c == 0)
    def _():
        state_ref[...] = s0_ref[0]

    length = GLA_CHUNK
    incl = _scan_masks(length, int(reverse))
    tri = incl.astype(F32)
    subs = range(GLA_STEP // length)
    for sub in (reversed(subs) if reverse else subs):
        tsl = slice(sub * length, (sub + 1) * length)
        g = _log_sigmoid(_dot(glr_ref[0, tsl, :], wg_ref[...], HI) + bg_ref[...]) / GLA_TAU
        gt = _log_sigmoid(_dot(wgt_ref[...], glrt_ref[0, :, tsl], HI) + bgt_ref[...]) / GLA_TAU
        gc = _dot(tri, g, HI)
        gct = _dot_nt(gt, tri, HI)
        tott = jnp.sum(gt, axis=1, keepdims=True)
        q_in = (q_ref[0, tsl, :].astype(F32) * (GLA_HEAD_K ** -0.5) * jnp.exp(gc)).astype(BF16)
        k_in = (k_ref[0, tsl, :].astype(F32) * jnp.exp(-gc)).astype(BF16)
        k_out_t = (kt_ref[0, :, tsl].astype(F32) * jnp.exp(tott - gct)).astype(BF16)
        dec_t = jnp.exp(tott)
        for h in range(GLA_HEADS):
            ksl = slice(h * GLA_HEAD_K, (h + 1) * GLA_HEAD_K)
            vsl = slice(h * GLA_HEAD_V, (h + 1) * GLA_HEAD_V)
            qh = q_in[:, ksl]
            vh = v_ref[0, tsl, vsl]
            att = jnp.where(incl, _dot_nt(qh, k_in[:, ksl]), 0.0)
            st = state_ref[h]
            o = _dot(att.astype(BF16), vh) + _dot(qh, st.astype(BF16))
            state_ref[h] = st * dec_t[ksl, :] + _dot(k_out_t[ksl, :], vh)
            o_ref[0, tsl, vsl] = o.astype(o_ref.dtype)

    @pl.when(c == n_steps - 1)
    def _():
        sout_ref[0] = state_ref[...]


def gla_scan(q, k, v, glr, w_gate, b_gate, s0):
    bsz, s, kd = q.shape
    vd = v.shape[-1]
    ns = s // GLA_STEP
    r = GLA_GATE_RANK
    kt = jnp.swapaxes(k, 1, 2)
    w_gate = w_gate.astype(F32)
    b_gate = b_gate.astype(F32)
    outs, states = [], []
    for d in range(2):
        cmap = (lambda c: ns - 1 - c) if d else (lambda c: c)
        glr_d = glr[:, :, d * r:(d + 1) * r]
        o, st = pl.pallas_call(
            functools.partial(_gla_kernel, n_steps=ns, reverse=bool(d)),
            grid=(bsz, ns),
            in_specs=[pl.BlockSpec((1, GLA_STEP, kd), lambda b, c, cmap=cmap: (b, cmap(c), 0)),
                      pl.BlockSpec((1, GLA_STEP, kd), lambda b, c, cmap=cmap: (b, cmap(c), 0)),
                      pl.BlockSpec((1, kd, GLA_STEP), lambda b, c, cmap=cmap: (b, 0, cmap(c))),
                      pl.BlockSpec((1, GLA_STEP, vd), lambda b, c, cmap=cmap: (b, cmap(c), 0)),
                      pl.BlockSpec((1, GLA_STEP, r), lambda b, c, cmap=cmap: (b, cmap(c), 0)),
                      pl.BlockSpec((1, r, GLA_STEP), lambda b, c, cmap=cmap: (b, 0, cmap(c))),
                      pl.BlockSpec((r, kd), lambda b, c: (0, 0)),
                      pl.BlockSpec((kd, r), lambda b, c: (0, 0)),
                      pl.BlockSpec((1, kd), lambda b, c: (0, 0)),
                      pl.BlockSpec((kd, 1), lambda b, c: (0, 0)),
                      pl.BlockSpec((1, GLA_HEADS, GLA_HEAD_K, GLA_HEAD_V), lambda b, c: (b, 0, 0, 0))],
            out_specs=[pl.BlockSpec((1, GLA_STEP, vd), lambda b, c, cmap=cmap: (b, cmap(c), 0)),
                       pl.BlockSpec((1, GLA_HEADS, GLA_HEAD_K, GLA_HEAD_V), lambda b, c: (b, 0, 0, 0))],
            out_shape=[jax.ShapeDtypeStruct((bsz, s, vd), BF16),
                       jax.ShapeDtypeStruct((bsz, GLA_HEADS, GLA_HEAD_K, GLA_HEAD_V), F32)],
            scratch_shapes=[pltpu.VMEM((GLA_HEADS, GLA_HEAD_K, GLA_HEAD_V), F32)],
            compiler_params=_cparams(("parallel", "arbitrary")),
            name="gla_scan_bwd" if d else "gla_scan_fwd",
        )(q, k, kt, v, glr_d, jnp.swapaxes(glr_d, 1, 2), w_gate[d], w_gate[d].T,
          b_gate[d].reshape(1, kd), b_gate[d].reshape(kd, 1), s0[d])
        outs.append(o)
        states.append(st)
    return outs, states


def _gla_finish_kernel(of_ref, ob_ref, r_ref, g_ref, out_ref):
    o = of_ref[0].astype(F32) + ob_ref[0].astype(F32)
    r = r_ref[0].astype(F32)
    for h in range(GLA_HEADS):
        vsl = slice(h * GLA_HEAD_V, (h + 1) * GLA_HEAD_V)
        oh = o[:, vsl]
        var = jnp.mean(oh * oh, axis=-1, keepdims=True)
        y = oh * lax.rsqrt(var + NORM_EPS) * g_ref[...]
        out_ref[0, :, vsl] = (y * _silu(r[:, vsl])).astype(out_ref.dtype)


def gla_finish(o, r, norm_g, ts=512):
    bsz, s, vd = o[0].shape
    ts = min(ts, s)
    return pl.pallas_call(
        _gla_finish_kernel,
        grid=(bsz, s // ts),
        in_specs=[pl.BlockSpec((1, ts, vd), lambda b, i: (b, i, 0)),
                  pl.BlockSpec((1, ts, vd), lambda b, i: (b, i, 0)),
                  pl.BlockSpec((1, ts, vd), lambda b, i: (b, i, 0)),
                  pl.BlockSpec((1, GLA_HEAD_V), lambda b, i: (0, 0))],
        out_specs=pl.BlockSpec((1, ts, vd), lambda b, i: (b, i, 0)),
        out_shape=jax.ShapeDtypeStruct((bsz, s, vd), BF16),
        compiler_params=_cparams(("parallel", "parallel")),
        name="gla_finish",
    )(o[0], o[1], r, norm_g.reshape(1, GLA_HEAD_V).astype(F32))


PEER_SCORE_TOKENS = 128
PEER_TOKENS = 128
PEER_SLOTS = 6
PEER_CAND_ROWS = -(-sum(PEER_TOPK // (a + 1) for a in range(PEER_TOPK)) // 8) * 8
PEER_SLAB = 2 * (D_MODEL // 2) // V7X_LANES
PEER_PITCH = PEER_SLAB + 4


def _extract_topk(s, rows, vals_ref, pos_ref):
    iota = lax.broadcasted_iota(jnp.int32, s.shape, 0).astype(F32)

    def body(j, cur):
        m = jnp.max(cur, axis=0, keepdims=True)
        pos = jnp.min(jnp.where(cur == m, iota, float(rows)), axis=0, keepdims=True)
        vals_ref[pl.ds(j, 1), :] = m
        pos_ref[pl.ds(j, 1), :] = pos
        return jnp.where(iota == pos, -jnp.inf, cur)

    lax.fori_loop(0, PEER_TOPK, body, s)


def _peer_score_kernel(q_ref, k1_ref, k2_ref, idx_ref, gate_ref, v12_ref, p12_ref, vt_ref, pt_ref, cand_ref):
    half = PEER_QDIM // 2
    tt = q_ref.shape[0]
    q = q_ref[...]
    s12 = jnp.concatenate([_dot_nt(k1_ref[...], q[:, :half], HI), _dot_nt(k2_ref[...], q[:, half:], HI)], axis=1)
    _extract_topk(s12, PEER_NKEYS, v12_ref, p12_ref)
    v1 = v12_ref[:, 0:tt]
    v2 = v12_ref[:, tt:2 * tt]
    p1 = p12_ref[:, 0:tt]
    p2 = p12_ref[:, tt:2 * tt]
    widths = [PEER_TOPK // (a + 1) for a in range(PEER_TOPK)]
    starts = [sum(widths[:a]) for a in range(PEER_TOPK)]
    n_cand = sum(widths)
    for a in range(PEER_TOPK):
        cand_ref[starts[a]:starts[a] + widths[a], :] = v1[a:a + 1, :] + v2[0:widths[a], :]
    cand_ref[n_cand:, :] = jnp.full((cand_ref.shape[0] - n_cand, tt), -jnp.inf, F32)
    _extract_topk(cand_ref[...], cand_ref.shape[0], vt_ref, pt_ref)
    pos = pt_ref[...]
    ia = jnp.zeros_like(pos)
    ib = pos
    for a in range(1, PEER_TOPK):
        past = pos >= float(starts[a])
        ia = ia + jnp.where(past, 1.0, 0.0)
        ib = ib - jnp.where(past, float(widths[a - 1]), 0.0)
    e1 = jnp.zeros_like(pos)
    e2 = jnp.zeros_like(pos)
    for a in range(PEER_TOPK):
        e1 = e1 + jnp.where(ia == float(a), p1[a:a + 1, :], 0.0)
        e2 = e2 + jnp.where(ib == float(a), p2[a:a + 1, :], 0.0)
    idx_ref[...] = (e1 * float(PEER_NKEYS) + e2).astype(jnp.int32)
    top = vt_ref[...]
    p = jnp.exp(top - top[0:1, :])
    gate_ref[...] = p / jnp.sum(p, axis=0, keepdims=True)


def peer_retrieve(q, k1, k2):
    t = q.shape[0]
    tt = min(PEER_SCORE_TOKENS, t)
    kk = PEER_TOPK
    return pl.pallas_call(
        _peer_score_kernel,
        grid=(t // tt, PEER_HEADS),
        in_specs=[pl.BlockSpec((tt, PEER_QDIM), lambda i, h: (i, h)),
                  pl.BlockSpec((PEER_NKEYS, PEER_QDIM // 2), lambda i, h: (0, 0)),
                  pl.BlockSpec((PEER_NKEYS, PEER_QDIM // 2), lambda i, h: (0, 0))],
        out_specs=[pl.BlockSpec((kk, tt), lambda i, h: (h, i)),
                   pl.BlockSpec((kk, tt), lambda i, h: (h, i))],
        out_shape=[jax.ShapeDtypeStruct((PEER_NSEL, t), jnp.int32),
                   jax.ShapeDtypeStruct((PEER_NSEL, t), F32)],
        scratch_shapes=[pltpu.VMEM((kk, 2 * tt), F32), pltpu.VMEM((kk, 2 * tt), F32),
                        pltpu.VMEM((kk, tt), F32), pltpu.VMEM((kk, tt), F32),
                        pltpu.VMEM((PEER_CAND_ROWS, tt), F32)],
        compiler_params=_cparams(("parallel", "parallel")),
        name="peer_retrieve",
    )(q, k1.astype(F32), k2.astype(F32))


def _gelu(x):
    return 0.5 * x * (1.0 + lax.erf(x * (1.0 / math.sqrt(2.0))))


def pack_expert_tables(u, v):
    def pack(tab):
        bits = lax.bitcast_convert_type(tab.astype(BF16), jnp.uint16).astype(jnp.uint32)
        half = tab.shape[1] // 2
        return bits[:, :half] | (bits[:, half:] << 16)

    n_exp = u.shape[0]
    slabs = jnp.concatenate([pack(u).reshape(n_exp, -1, V7X_LANES), pack(v).reshape(n_exp, -1, V7X_LANES)], axis=1)
    return slabs.reshape(n_exp * PEER_SLAB, V7X_LANES)


def _unpack_pair(words):
    lo = pltpu.bitcast(words << 16, F32)
    hi = pltpu.bitcast(words & jnp.uint32(0xFFFF0000), F32)
    return lo, hi


def _peer_expert_kernel(idx_ref, gate_ref, h_ref, x_ref, gt_ref, uv_hbm, o_ref, *scratch, tokens):
    bufs, sem = scratch[:PEER_SLOTS], scratch[PEER_SLOTS]
    half = D_MODEL // 2
    n_chunks = PEER_SLAB // 2
    per_step = PEER_NSEL // PEER_SLAB
    depth = PEER_SLOTS - 1

    def row_copy(e, slot, k):
        src = uv_hbm.at[pl.ds(pl.multiple_of(e * PEER_SLAB, PEER_SLAB), PEER_SLAB), :]
        return pltpu.make_async_copy(src, bufs[slot].at[pl.ds(k * PEER_PITCH, PEER_SLAB), :], sem.at[slot])

    def issue(tok, slot, ks):
        for k in ks:
            row_copy(idx_ref[k, tok], slot, k).start(priority=k % 2)

    def wait(slot):
        n = PEER_NSEL * PEER_SLAB
        pltpu.make_async_copy(uv_hbm.at[pl.ds(0, n), :], bufs[slot].at[pl.ds(0, n), :], sem.at[slot]).wait()

    lane = lax.broadcasted_iota(jnp.int32, (PEER_NSEL, tokens), 1)

    def chunk(slot, row):
        return _unpack_pair(bufs[slot][pl.ds(row, PEER_NSEL, stride=PEER_PITCH), :])

    def compute(j, slot, prefetch):
        def maybe_issue(step):
            if prefetch:
                issue(j + depth, (slot + depth) % PEER_SLOTS, range(step * per_step, (step + 1) * per_step))

        gcol = jnp.sum(jnp.where(lane == j, gate_ref[...], 0.0), axis=1, keepdims=True)
        p = jnp.zeros((PEER_NSEL, V7X_LANES), F32)
        t_lo = h_ref[pl.ds(j, 1), 0:half]
        t_hi = h_ref[pl.ds(j, 1), half:2 * half]
        for c in range(n_chunks):
            maybe_issue(c)
            ua, ub = chunk(slot, c)
            cols = slice(c * V7X_LANES, (c + 1) * V7X_LANES)
            p = p + ua * t_lo[:, cols] + ub * t_hi[:, cols]
        act = jnp.sum(p, axis=1, keepdims=True)
        a = jnp.broadcast_to(_gelu(act) * gcol, (PEER_NSEL, V7X_LANES))
        out_lo, out_hi = [], []
        for c in range(n_chunks):
            maybe_issue(n_chunks + c)
            va, vb = chunk(slot, n_chunks + c)
            out_lo.append(jnp.sum(a * va, axis=0, keepdims=True))
            out_hi.append(jnp.sum(a * vb, axis=0, keepdims=True))
        gt = gt_ref[0]
        for cols, parts in ((slice(0, half), out_lo), (slice(half, 2 * half), out_hi)):
            o_ref[0, pl.ds(j, 1), cols] = (x_ref[0, pl.ds(j, 1), cols]
                                           + gt[:, cols] * jnp.concatenate(parts, axis=1))

    for j0 in range(depth):
        issue(j0, j0 % PEER_SLOTS, range(PEER_NSEL))

    def body(it, carry):
        for s in range(PEER_SLOTS):
            wait(s)
            compute(it * PEER_SLOTS + s, s, True)
        return carry

    n_main = (tokens - depth) // PEER_SLOTS
    lax.fori_loop(0, n_main, body, 0)
    for j in range(n_main * PEER_SLOTS, tokens):
        wait(j % PEER_SLOTS)
        compute(j, j % PEER_SLOTS, j + depth < tokens)


def peer_experts(idx, gate, h, x, gt, uv):
    bsz, s, dm = x.shape
    t = bsz * s
    tt = min(PEER_TOKENS, s)
    nt = s // tt
    per_batch = gt.shape[0] == bsz and bsz > 1
    g_map = (lambda b, i: (b, 0, 0)) if per_batch else (lambda b, i: (0, 0, 0))
    out = pl.pallas_call(
        functools.partial(_peer_expert_kernel, tokens=tt),
        grid=(bsz, nt),
        in_specs=[pl.BlockSpec((PEER_NSEL, tt), lambda b, i: (0, b * nt + i), memory_space=pltpu.SMEM),
                  pl.BlockSpec((PEER_NSEL, tt), lambda b, i: (0, b * nt + i)),
                  pl.BlockSpec((tt, dm), lambda b, i: (b * nt + i, 0)),
                  pl.BlockSpec((1, tt, dm), lambda b, i: (b, i, 0)),
                  pl.BlockSpec((1, 1, dm), g_map),
                  pl.BlockSpec(memory_space=pl.ANY)],
        out_specs=pl.BlockSpec((1, tt, dm), lambda b, i: (b, i, 0)),
        out_shape=jax.ShapeDtypeStruct((bsz, s, dm), F32),
        scratch_shapes=[pltpu.VMEM((PEER_NSEL * PEER_PITCH, V7X_LANES), jnp.uint32) for _ in range(PEER_SLOTS)]
        + [pltpu.SemaphoreType.DMA((PEER_SLOTS,))],
        compiler_params=pltpu.CompilerParams(dimension_semantics=("arbitrary", "arbitrary"),
                                             vmem_limit_bytes=VMEM_LIMIT),
        name="peer_experts",
    )(idx, gate, h, x.reshape(bsz, s, dm), gt, uv)
    return out


def peer_sublayer(x, g_norm, shift, scale, gt, wq, k1, k2, uv):
    bsz, s, dm = x.shape
    h_bf, h_f32 = norm_modulate(x, g_norm, shift, scale, (BF16, F32))
    q = matmul(h_bf.reshape(bsz * s, dm), wq, F32)
    idx, gate = peer_retrieve(q, k1, k2)
    return peer_experts(idx, gate, h_f32.reshape(bsz * s, dm), x, gt, uv)


def _head_major(t, n_heads):
    bsz, s, _ = t.shape
    return t.reshape(bsz, s, n_heads, -1).transpose(0, 2, 1, 3)


_PROJ_GROUPS = dict(z=(0, BF16), xbc=(1, F32), na=(3, BF16), gla_q=(4, BF16), gla_k=(5, BF16), gla_v=(6, BF16),
                    gla_r=(7, BF16), swa_q=(9, BF16), swa_k=(10, BF16), swa_v=(11, BF16), gates=(12, BF16))


def _split_w_in(w):
    o = IN_OFFS
    ws = {name: w[:, o[i]:o[i + 1]].astype(BF16) for name, (i, _) in _PROJ_GROUPS.items()}
    small = jnp.concatenate([w[:, o[2]:o[3]], w[:, o[8]:o[9]]], axis=1)
    ws['small'] = jnp.pad(small, ((0, 0), (0, V7X_LANES - small.shape[1]))).astype(BF16)
    return ws


def _project(h, ws):
    bsz, s, dm = h.shape
    hf = h.reshape(bsz * s, dm)
    out = {name: matmul(hf, ws[name], dt).reshape(bsz, s, -1) for name, (_, dt) in _PROJ_GROUPS.items()}
    sm = matmul(hf, ws['small'], F32).reshape(bsz, s, V7X_LANES)
    out['dt'] = sm[:, :, :2 * SSD_HEADS]
    out['glr'] = sm[:, :, 2 * SSD_HEADS:2 * SSD_HEADS + 2 * GLA_GATE_RANK]
    return out


def _flat_heads(t):
    return t.reshape((-1,) + t.shape[-2:])


def mixer_sublayer(hx, hc, p, rope, need_ctx):
    bsz, s, _ = hx.shape
    lc = hc.shape[1]
    ws = _split_w_in(p['w_in'])
    px = _project(hx, ws)
    pc = _project(hc, ws)
    zeros_h = jnp.zeros((2, bsz, SSD_HEADS, SSD_STATE, SSD_HEADDIM), F32)
    zero_s = jnp.zeros((bsz, GLA_HEADS, GLA_HEAD_K, GLA_HEAD_V), F32)
    zeros_s = (zero_s, zero_s)

    act_c = conv_silu(pc['xbc'], p['ssd_conv_w'], p['ssd_conv_b'])
    act_x = conv_silu(px['xbc'], p['ssd_conv_w'], p['ssd_conv_b'])
    y_c, h_c = ssd_scan(act_c, pc['dt'], p['ssd_a_log'], p['ssd_dt_bias'], zeros_h)
    y_x, _ = ssd_scan(act_x, px['dt'], p['ssd_a_log'], p['ssd_dt_bias'], h_c)
    a_x = ssd_finish(y_x, act_x, px['z'], p['ssd_d'], p['ssd_norm_g'])

    dh = NA_HEAD_DIM
    scale = dh ** -0.5

    def na_heads(t):
        t = _head_major(t, 3 * NA_HEADS)
        q = head_prep(_flat_heads(t[:, :NA_HEADS]), p['na_q_norm'], None, scale).reshape(t[:, :NA_HEADS].shape)
        k = head_prep(_flat_heads(t[:, NA_HEADS:2 * NA_HEADS]), p['na_k_norm'], None, 1.0).reshape(q.shape)
        return q, k, t[:, 2 * NA_HEADS:]

    b_x = na_attention(px['na'], pc['na'], p['na_q_norm'], p['na_k_norm'], na_bias_table(p['na_rpb']))

    o_c, s_c = gla_scan(pc['gla_q'], pc['gla_k'], pc['gla_v'], pc['glr'], p['gla_w_gate'], p['gla_b_gate'], zeros_s)
    o_x, _ = gla_scan(px['gla_q'], px['gla_k'], px['gla_v'], px['glr'], p['gla_w_gate'], p['gla_b_gate'], s_c)
    c_x = gla_finish(o_x, px['gla_r'], p['gla_norm_g'])

    def swa_heads(q, k, v, tabs):
        qh = _head_major(q, SWA_HEADS)
        kh = _head_major(k, SWA_KV_HEADS)
        qn = head_prep(_flat_heads(qh), p['swa_q_norm'], tabs, SWA_HEAD_DIM ** -0.5).reshape(qh.shape)
        kn = head_prep(_flat_heads(kh), p['swa_k_norm'], tabs, 1.0).reshape(kh.shape)
        qn = qn.reshape(qh.shape[0], SWA_KV_HEADS, SWA_GROUP, qh.shape[2], SWA_HEAD_DIM)
        return qn, kn, _head_major(v, SWA_KV_HEADS)

    sq, sk, sv = swa_heads(px['swa_q'], px['swa_k'], px['swa_v'], rope)
    sqc, skc, svc = swa_heads(pc['swa_q'], pc['swa_k'], pc['swa_v'], None)
    d_x = swa_attention(sq, sk, sv, skc, svc, p['swa_sink'])
    d_x = d_x.reshape(bsz, SWA_HEADS, s, SWA_HEAD_DIM).transpose(0, 2, 1, 3).reshape(bsz, s, BRANCH_WIDTH)

    wb = p['w_branch'].astype(BF16)
    flat = lambda t: t.reshape(-1, t.shape[-1])
    m_x = merge_branches([flat(a_x), flat(b_x), flat(c_x), flat(d_x)], flat(px['gates']), wb)
    if not need_ctx:
        return m_x, None

    a_c = ssd_finish(y_c, act_c, pc['z'], p['ssd_d'], p['ssd_norm_g'])
    no_sink = jnp.full((NA_HEADS,), NEG, F32)
    nqc, nkc, nvc = na_heads(pc['na'])
    b_c = ctx_attention(nqc[:, :, None], nkc, nvc, no_sink)[:, :, 0]
    b_c = b_c.transpose(0, 2, 1, 3).reshape(bsz, lc, BRANCH_WIDTH)
    c_c = gla_finish(o_c, pc['gla_r'], p['gla_norm_g'])
    d_c = ctx_attention(sqc, skc, svc, p['swa_sink'])
    d_c = d_c.reshape(bsz, SWA_HEADS, lc, SWA_HEAD_DIM).transpose(0, 2, 1, 3).reshape(bsz, lc, BRANCH_WIDTH)
    m_c = merge_branches([flat(a_c), flat(b_c), flat(c_c), flat(d_c)], flat(pc['gates']), wb)
    return m_x, m_c


def kernel(x, c, ctx, c_ctx, w_ada, b_ada, g_norm1, g_norm2, w_in, ssd_conv_w, ssd_conv_b, ssd_a_log, ssd_dt_bias, ssd_d, ssd_norm_g, na_q_norm, na_k_norm, na_rpb, gla_w_gate, gla_b_gate, gla_norm_g, swa_q_norm, swa_k_norm, swa_sink, w_branch, w_out, peer_wq, peer_k1, peer_k2, peer_u, peer_v):
    bsz, s, dm = x.shape
    rope = rope_tables(s)
    n_cond = 8
    cc = jnp.zeros((n_cond, dm), F32).at[:bsz].set(c).at[bsz].set(c_ctx)
    for l in range(DEPTH):
        need_ctx = l < DEPTH - 1
        mod = ada_mod(cc, w_ada[l], b_ada[l])
        mx = [mod[:bsz, i * dm:(i + 1) * dm].reshape(bsz, 1, dm) for i in range(6)]
        mc = [mod[bsz:bsz + 1, i * dm:(i + 1) * dm].reshape(1, 1, dm) for i in range(6)]
        p = dict(w_in=w_in[l], ssd_conv_w=ssd_conv_w[l], ssd_conv_b=ssd_conv_b[l], ssd_a_log=ssd_a_log[l],
                 ssd_dt_bias=ssd_dt_bias[l], ssd_d=ssd_d[l], ssd_norm_g=ssd_norm_g[l],
                 na_q_norm=na_q_norm[l], na_k_norm=na_k_norm[l], na_rpb=na_rpb[l],
                 gla_w_gate=gla_w_gate[l], gla_b_gate=gla_b_gate[l], gla_norm_g=gla_norm_g[l],
                 swa_q_norm=swa_q_norm[l], swa_k_norm=swa_k_norm[l], swa_sink=swa_sink[l],
                 w_branch=w_branch[l])
        (hx,) = norm_modulate(x, g_norm1[l], mx[0], mx[1], (BF16,))
        (hc,) = norm_modulate(ctx, g_norm1[l], mc[0], mc[1], (BF16,))
        m_x, m_c = mixer_sublayer(hx, hc, p, rope, need_ctx)
        wo = w_out[l].astype(BF16)
        wq = peer_wq[l].astype(BF16)
        uv = pack_expert_tables(peer_u[l], peer_v[l])
        x = matmul_gated_residual(m_x, wo, x, mx[2])
        x = peer_sublayer(x, g_norm2[l], mx[3], mx[4], mx[5], wq, peer_k1[l], peer_k2[l], uv)
        if need_ctx:
            ctx = matmul_gated_residual(m_c, wo, ctx, mc[2])
            ctx = peer_sublayer(ctx, g_norm2[l], mc[3], mc[4], mc[5], wq, peer_k1[l], peer_k2[l], uv)
    return x
```

```python
import functools
import math

import numpy as np
import jax
import jax.numpy as jnp
from jax import lax
from jax.experimental import pallas as pl
from jax.experimental.pallas import tpu as pltpu

D_MODEL = 2048
DEPTH = 2
GRID_W = 64
NORM_EPS = 1e-6
N_BRANCH = 4
BRANCH_WIDTH = D_MODEL // 2

SSD_HEADDIM = 64
SSD_D_INNER = BRANCH_WIDTH
SSD_HEADS = SSD_D_INNER // SSD_HEADDIM
SSD_GROUPS = 4
SSD_STATE = 128
SSD_CONV_DIM = SSD_D_INNER + 2 * SSD_GROUPS * SSD_STATE
SSD_CONV_K = 5
SSD_CHUNK = 128

NA_HEAD_DIM = 64
NA_HEADS = BRANCH_WIDTH // NA_HEAD_DIM
NA_WIN_R = 8
NA_WIN_C = 16

GLA_HEADS = 4
GLA_V_DIM = BRANCH_WIDTH
GLA_K_DIM = BRANCH_WIDTH // 2
GLA_HEAD_K = GLA_K_DIM // GLA_HEADS
GLA_HEAD_V = GLA_V_DIM // GLA_HEADS
GLA_GATE_RANK = 16
GLA_TAU = 16.0
GLA_CHUNK = 64

SWA_HEAD_DIM = 64
SWA_HEADS = BRANCH_WIDTH // SWA_HEAD_DIM
SWA_KV_HEADS = 4
SWA_GROUP = SWA_HEADS // SWA_KV_HEADS
SWA_WINDOW = 128
SWA_BLOCK = 128
ROPE_BASE = 10000.0
ROPE_AXIS_DIM = SWA_HEAD_DIM // 2

PEER_HEADS = 8
PEER_NKEYS = 128
PEER_QDIM = 256
PEER_TOPK = 16
PEER_NSEL = PEER_HEADS * PEER_TOPK

IN_SPLITS = (SSD_D_INNER, SSD_CONV_DIM, 2 * SSD_HEADS,
             3 * BRANCH_WIDTH,
             GLA_K_DIM, GLA_K_DIM, GLA_V_DIM, GLA_V_DIM, 2 * GLA_GATE_RANK,
             SWA_HEADS * SWA_HEAD_DIM, SWA_KV_HEADS * SWA_HEAD_DIM, SWA_KV_HEADS * SWA_HEAD_DIM,
             N_BRANCH * D_MODEL)
IN_OFFS = tuple(int(v) for v in np.cumsum((0,) + IN_SPLITS))

V7X_LANES = 128
V7X_VMEM_BYTES = 64 * 1024 * 1024
VMEM_LIMIT = 48 * 1024 * 1024

F32 = jnp.float32
BF16 = jnp.bfloat16
HI = lax.Precision.HIGHEST
NEG = -1e30


def _cparams(sem):
    return pltpu.CompilerParams(dimension_semantics=sem, vmem_limit_bytes=VMEM_LIMIT)


def _dot(a, b, precision=None):
    return jnp.dot(a, b, preferred_element_type=F32, precision=precision)


def _dot_nt(a, b, precision=None):
    return lax.dot_general(a, b, (((1,), (1,)), ((), ())), preferred_element_type=F32,
                           precision=precision)


def _silu(x):
    return x / (1.0 + jnp.exp(-x))


def _softplus(x):
    return jnp.maximum(x, 0.0) + jnp.log1p(jnp.exp(-jnp.abs(x)))


def _log_sigmoid(x):
    return jnp.minimum(x, 0.0) - jnp.log1p(jnp.exp(-jnp.abs(x)))


def _mm_kernel(a_ref, b_ref, o_ref):
    o_ref[...] = _dot(a_ref[...], b_ref[...]).astype(o_ref.dtype)


def matmul(a, b, out_dtype, tm=1024, tn=512):
    m, k = a.shape
    n = b.shape[1]
    tm = min(tm, m)
    if n % (2 * tn) == 0:
        tn = 2 * tn
    tn = min(tn, n)
    assert m % tm == 0 and n % tn == 0, (m, n, tm, tn)
    return pl.pallas_call(
        _mm_kernel,
        grid=(m // tm, n // tn),
        in_specs=[pl.BlockSpec((tm, k), lambda i, j: (i, 0)),
                  pl.BlockSpec((k, tn), lambda i, j: (0, j))],
        out_specs=pl.BlockSpec((tm, tn), lambda i, j: (i, j)),
        out_shape=jax.ShapeDtypeStruct((m, n), out_dtype),
        compiler_params=_cparams(("parallel", "parallel")),
        name="matmul",
    )(a, b)


def _mod_kernel(c_ref, w_ref, b_ref, o_ref):
    a = _silu(c_ref[...]).astype(BF16)
    o_ref[...] = _dot(a, w_ref[...].astype(BF16)) + b_ref[...]


def ada_mod(cc, w, b, tn=1024):
    m, k = cc.shape
    n = w.shape[1]
    return pl.pallas_call(
        _mod_kernel,
        grid=(n // tn,),
        in_specs=[pl.BlockSpec((m, k), lambda j: (0, 0)),
                  pl.BlockSpec((k, tn), lambda j: (0, j)),
                  pl.BlockSpec((1, tn), lambda j: (0, j))],
        out_specs=pl.BlockSpec((m, tn), lambda j: (0, j)),
        out_shape=jax.ShapeDtypeStruct((m, n), F32),
        compiler_params=_cparams(("parallel",)),
        name="ada_mod",
    )(cc, w, b.reshape(1, n))


def _normmod_kernel(x_ref, g_ref, sh_ref, sc_ref, *o_refs):
    x = x_ref[0]
    var = jnp.mean(x * x, axis=-1, keepdims=True)
    y = x * lax.rsqrt(var + NORM_EPS) * g_ref[...]
    y = y * (1.0 + sc_ref[0]) + sh_ref[0]
    for o_ref in o_refs:
        o_ref[0] = y.astype(o_ref.dtype)


def norm_modulate(x, g, shift, scale, out_dtypes, ts=512):
    bsz, s, d = x.shape
    ts = min(ts, s)
    per_batch = shift.shape[0] == bsz and bsz > 1
    mod_map = (lambda b, i: (b, 0, 0)) if per_batch else (lambda b, i: (0, 0, 0))
    outs = pl.pallas_call(
        _normmod_kernel,
        grid=(bsz, s // ts),
        in_specs=[pl.BlockSpec((1, ts, d), lambda b, i: (b, i, 0)),
                  pl.BlockSpec((1, d), lambda b, i: (0, 0)),
                  pl.BlockSpec((1, 1, d), mod_map),
                  pl.BlockSpec((1, 1, d), mod_map)],
        out_specs=[pl.BlockSpec((1, ts, d), lambda b, i: (b, i, 0)) for _ in out_dtypes],
        out_shape=[jax.ShapeDtypeStruct((bsz, s, d), dt) for dt in out_dtypes],
        compiler_params=_cparams(("parallel", "parallel")),
        name="norm_modulate",
    )(x, g.reshape(1, d), shift, scale)
    return outs


def _mm_resid_kernel(a_ref, w_ref, x_ref, gt_ref, o_ref):
    y = _dot(a_ref[...], w_ref[...])
    o_ref[0] = x_ref[0] + gt_ref[0] * y


def matmul_gated_residual(a, w, x, gate, tm=1024, tn=512):
    bsz, s, n = x.shape
    k = a.shape[1]
    tm = min(tm, s)
    nt = s // tm
    per_batch = gate.shape[0] == bsz and bsz > 1
    g_map = (lambda b, i, j: (b, 0, j)) if per_batch else (lambda b, i, j: (0, 0, j))
    return pl.pallas_call(
        _mm_resid_kernel,
        grid=(bsz, nt, n // tn),
        in_specs=[pl.BlockSpec((tm, k), lambda b, i, j: (b * nt + i, 0)),
                  pl.BlockSpec((k, tn), lambda b, i, j: (0, j)),
                  pl.BlockSpec((1, tm, tn), lambda b, i, j: (b, i, j)),
                  pl.BlockSpec((1, 1, tn), g_map)],
        out_specs=pl.BlockSpec((1, tm, tn), lambda b, i, j: (b, i, j)),
        out_shape=jax.ShapeDtypeStruct((bsz, s, n), F32),
        compiler_params=_cparams(("parallel", "parallel", "parallel")),
        name="out_proj_residual",
    )(a, w, x, gate)


def _merge_kernel(a_ref, b_ref, c_ref, d_ref, g0_ref, g1_ref, g2_ref, g3_ref, w_ref, o_ref):
    acc = None
    for i, (br, gr) in enumerate(((a_ref, g0_ref), (b_ref, g1_ref), (c_ref, g2_ref), (d_ref, g3_ref))):
        y = _dot(br[...], w_ref[i])
        gate = 1.0 / (1.0 + jnp.exp(-gr[...].astype(F32)))
        acc = gate * y if acc is None else acc + gate * y
    o_ref[...] = acc.astype(o_ref.dtype)


def merge_branches(outs, gate_logits, w_branch, tm=512, tn=512):
    t, kb = outs[0].shape
    d = w_branch.shape[-1]
    tm = min(tm, t)
    nj = d // tn
    in_specs = [pl.BlockSpec((tm, kb), lambda i, j: (i, 0)) for _ in range(N_BRANCH)]
    in_specs += [pl.BlockSpec((tm, tn), functools.partial(lambda i, j, br: (i, br * nj + j), br=br))
                 for br in range(N_BRANCH)]
    in_specs += [pl.BlockSpec((N_BRANCH, kb, tn), lambda i, j: (0, 0, j))]
    return pl.pallas_call(
        _merge_kernel,
        grid=(t // tm, nj),
        in_specs=in_specs,
        out_specs=pl.BlockSpec((tm, tn), lambda i, j: (i, j)),
        out_shape=jax.ShapeDtypeStruct((t, d), BF16),
        compiler_params=_cparams(("parallel", "parallel")),
        name="merge_branches",
    )(*outs, gate_logits, gate_logits, gate_logits, gate_logits, w_branch)


def _headprep_kernel(x_ref, g_ref, cos_ref, sin_ref, rot_ref, o_ref, *, rope, scale):
    x = x_ref[0].astype(F32)
    var = jnp.mean(x * x, axis=-1, keepdims=True)
    y = x * lax.rsqrt(var + NORM_EPS) * g_ref[...]
    if rope:
        y = y * cos_ref[...] + _dot(y, rot_ref[...], HI) * sin_ref[...]
    o_ref[0] = (y * scale).astype(o_ref.dtype)


def head_prep(x, g, rope_tabs, scale, ts=1024):
    n, s, dh = x.shape
    ts = min(ts, s)
    rope = rope_tabs is not None
    if rope:
        cos, sin, rot = rope_tabs
    else:
        cos = sin = jnp.zeros((s, dh), F32)
        rot = jnp.zeros((dh, dh), F32)
    return pl.pallas_call(
        functools.partial(_headprep_kernel, rope=rope, scale=scale),
        grid=(n, s // ts),
        in_specs=[pl.BlockSpec((1, ts, dh), lambda h, i: (h, i, 0)),
                  pl.BlockSpec((1, dh), lambda h, i: (0, 0)),
                  pl.BlockSpec((ts, dh), lambda h, i: (i, 0)),
                  pl.BlockSpec((ts, dh), lambda h, i: (i, 0)),
                  pl.BlockSpec((dh, dh), lambda h, i: (0, 0))],
        out_specs=pl.BlockSpec((1, ts, dh), lambda h, i: (h, i, 0)),
        out_shape=jax.ShapeDtypeStruct((n, s, dh), BF16),
        compiler_params=_cparams(("parallel", "parallel")),
        name="head_prep",
    )(x, g.reshape(1, dh), cos, sin, rot)


def rope_tables(s):
    t = np.arange(s)
    row = (t // GRID_W).astype(np.float32)
    col = (t % GRID_W).astype(np.float32)
    nf = ROPE_AXIS_DIM // 2
    inv = jnp.asarray(ROPE_BASE, F32) ** (-jnp.arange(nf, dtype=F32) / nf)
    ar = jnp.asarray(row)[:, None] * inv
    ac = jnp.asarray(col)[:, None] * inv
    cos = jnp.concatenate([jnp.cos(ar), jnp.cos(ar), jnp.cos(ac), jnp.cos(ac)], axis=-1)
    sin = jnp.concatenate([jnp.sin(ar), jnp.sin(ar), jnp.sin(ac), jnp.sin(ac)], axis=-1)
    rot = np.zeros((SWA_HEAD_DIM, SWA_HEAD_DIM), np.float32)
    for d in range(SWA_HEAD_DIM):
        if d % ROPE_AXIS_DIM < nf:
            rot[d + nf, d] = -1.0
        else:
            rot[d - nf, d] = 1.0
    return cos, sin, jnp.asarray(rot)


NA_ROWS_PER_STEP = 8


def _pair_head_norm(x, g, lane_lo):
    sq = x * x
    lo = jnp.sum(jnp.where(lane_lo, sq, 0.0), axis=-1, keepdims=True)
    hi = jnp.sum(sq, axis=-1, keepdims=True) - lo
    inv = jnp.where(lane_lo, lax.rsqrt(lo * (1.0 / NA_HEAD_DIM) + NORM_EPS),
                    lax.rsqrt(hi * (1.0 / NA_HEAD_DIM) + NORM_EPS))
    return x * inv * g


def _na_kernel(q_ref, k_ref, v_ref, kc_ref, vc_ref, qg_ref, kg_ref, bias_ref, o_ref, kn_ref, kcn_ref, *, n_rows, seq):
    i = pl.program_id(2)
    n_nb = NA_WIN_R * GRID_W
    width = 2 * NA_HEAD_DIM
    norm_rows = min(seq, 512)

    @pl.when(i == 0)
    def _():
        lane_lo = lax.broadcasted_iota(jnp.int32, (norm_rows, width), 1) < NA_HEAD_DIM

        def nbody(t, carry):
            off = pl.multiple_of(t * norm_rows, norm_rows)
            x = k_ref[0, pl.ds(off, norm_rows), :].astype(F32)
            kn_ref[pl.ds(off, norm_rows), :] = _pair_head_norm(x, kg_ref[...], lane_lo).astype(BF16)
            return carry

        lax.fori_loop(0, seq // norm_rows, nbody, 0)
        lane_lo_c = lax.broadcasted_iota(jnp.int32, kcn_ref.shape, 1) < NA_HEAD_DIM
        kcn_ref[...] = _pair_head_norm(kc_ref[0].astype(F32), kg_ref[...], lane_lo_c).astype(BF16)

    kc = kcn_ref[...]
    vc = vc_ref[0]
    lane_lo = lax.broadcasted_iota(jnp.int32, (GRID_W, width), 1) < NA_HEAD_DIM
    scale = NA_HEAD_DIM ** -0.5

    def body(rr, carry):
        r = i * NA_ROWS_PER_STEP + rr
        r0 = jnp.clip(r - NA_WIN_R // 2, 0, n_rows - NA_WIN_R)
        qoff = pl.multiple_of(rr * GRID_W, GRID_W)
        koff = pl.multiple_of(r0 * GRID_W, GRID_W)
        q = _pair_head_norm(q_ref[0, pl.ds(qoff, GRID_W), :].astype(F32), qg_ref[...], lane_lo) * scale
        kn = kn_ref[pl.ds(koff, n_nb), :]
        vn = v_ref[0, pl.ds(koff, n_nb), :]
        outs = []
        for e in range(2):
            qe = jnp.where(lane_lo if e == 0 else jnp.logical_not(lane_lo), q, 0.0).astype(BF16)
            s_nb = _dot_nt(qe, kn) + bias_ref[r - r0, e]
            s_cx = _dot_nt(qe, kc)
            m = jnp.maximum(jnp.max(s_nb, axis=-1, keepdims=True), jnp.max(s_cx, axis=-1, keepdims=True))
            p_nb = jnp.exp(s_nb - m)
            p_cx = jnp.exp(s_cx - m)
            den = jnp.sum(p_nb, axis=-1, keepdims=True) + jnp.sum(p_cx, axis=-1, keepdims=True)
            outs.append((_dot(p_nb.astype(BF16), vn) + _dot(p_cx.astype(BF16), vc)) / den)
        o_ref[0, pl.ds(qoff, GRID_W), :] = jnp.where(lane_lo, outs[0], outs[1]).astype(o_ref.dtype)
        return carry

    lax.fori_loop(0, NA_ROWS_PER_STEP, body, 0, unroll=True)


def na_bias_table(rpb):
    nh = rpb.shape[0]
    wr, wc, gw = NA_WIN_R, NA_WIN_C, GRID_W
    rpb = rpb.astype(F32)

    def toeplitz(vec, n, period):
        tiled = jnp.broadcast_to(vec[..., None, :], vec.shape[:-1] + (n, period))
        flat = tiled.reshape(vec.shape[:-1] + (n * period,))[..., :n * (period - 1)]
        return flat.reshape(vec.shape[:-1] + (n, period - 1))[..., :n]

    rows = jnp.concatenate([rpb[:, wr - 1:], jnp.zeros((nh, 1, 2 * wc - 1), F32), rpb[:, :wr - 1]], axis=1)
    tab = toeplitz(jnp.moveaxis(rows, 1, -1), wr, 2 * wr)
    tab = jnp.moveaxis(tab, 1, -1)
    fill = jnp.zeros(tab.shape[:-1] + (2 * gw - (2 * wc - 1),), F32)
    cols_ext = jnp.concatenate([tab[..., wc - 1:], fill, tab[..., :wc - 1]], axis=-1)
    tab = toeplitz(cols_ext, gw, 2 * gw)
    cols = np.arange(gw)
    c_start = np.clip(cols - wc // 2, 0, gw - wc)
    valid = (cols[None, :] >= c_start[:, None]) & (cols[None, :] < c_start[:, None] + wc)
    tab = jnp.where(jnp.asarray(valid)[None, None, None], tab, NEG)
    tab = tab.transpose(1, 0, 3, 2, 4)
    return tab.reshape(wr, nh, gw, wr * gw)


def na_attention(qkv, qkv_c, q_norm, k_norm, bias):
    bsz, s, _ = qkv.shape
    lc = qkv_c.shape[1]
    n_rows = s // GRID_W
    tq = NA_ROWS_PER_STEP * GRID_W
    n_nb = NA_WIN_R * GRID_W
    width = 2 * NA_HEAD_DIM
    n_pairs = NA_HEADS // 2
    gq = jnp.tile(q_norm.astype(F32), 2).reshape(1, width)
    gk = jnp.tile(k_norm.astype(F32), 2).reshape(1, width)
    return pl.pallas_call(
        functools.partial(_na_kernel, n_rows=n_rows, seq=s),
        grid=(bsz, n_pairs, s // tq),
        in_specs=[pl.BlockSpec((1, tq, width), lambda b, p, i: (b, i, p)),
                  pl.BlockSpec((1, s, width), lambda b, p, i: (b, 0, n_pairs + p)),
                  pl.BlockSpec((1, s, width), lambda b, p, i: (b, 0, 2 * n_pairs + p)),
                  pl.BlockSpec((1, lc, width), lambda b, p, i: (b, 0, n_pairs + p)),
                  pl.BlockSpec((1, lc, width), lambda b, p, i: (b, 0, 2 * n_pairs + p)),
                  pl.BlockSpec((1, width), lambda b, p, i: (0, 0)),
                  pl.BlockSpec((1, width), lambda b, p, i: (0, 0)),
                  pl.BlockSpec((NA_WIN_R, 2, GRID_W, n_nb), lambda b, p, i: (0, p, 0, 0))],
        out_specs=pl.BlockSpec((1, tq, width), lambda b, p, i: (b, i, p)),
        out_shape=jax.ShapeDtypeStruct((bsz, s, NA_HEADS * NA_HEAD_DIM), BF16),
        scratch_shapes=[pltpu.VMEM((s, width), BF16), pltpu.VMEM((lc, width), BF16)],
        compiler_params=_cparams(("parallel", "parallel", "arbitrary")),
        name="na_attention",
    )(qkv, qkv, qkv, qkv_c, qkv_c, gq, gk, bias)


def _swa_kernel(sink_ref, q_ref, k_ref, v_ref, kc_ref, vc_ref, o_ref, *, seq):
    kh = pl.program_id(1)
    n = pl.program_id(2)
    span = 3 * SWA_BLOCK
    start = pl.multiple_of(jnp.clip((n - 1) * SWA_BLOCK, 0, seq - span), SWA_BLOCK)
    kw = k_ref[0, 0, pl.ds(start, span), :]
    vw = v_ref[0, 0, pl.ds(start, span), :]
    kc = kc_ref[0, 0]
    vc = vc_ref[0, 0]
    qpos = n * SWA_BLOCK + lax.broadcasted_iota(jnp.int32, (SWA_BLOCK, span), 0)
    kpos = start + lax.broadcasted_iota(jnp.int32, (SWA_BLOCK, span), 1)
    valid = jnp.abs(qpos - kpos) <= SWA_WINDOW
    for g in range(SWA_GROUP):
        q = q_ref[0, 0, g]
        s_loc = jnp.where(valid, _dot_nt(q, kw), NEG)
        s_ctx = _dot_nt(q, kc)
        sink = sink_ref[kh * SWA_GROUP + g]
        m = jnp.maximum(jnp.max(s_loc, axis=-1, keepdims=True), jnp.max(s_ctx, axis=-1, keepdims=True))
        m = jnp.maximum(m, sink)
        p_loc = jnp.exp(s_loc - m)
        p_ctx = jnp.exp(s_ctx - m)
        den = (jnp.sum(p_loc, axis=-1, keepdims=True) + jnp.sum(p_ctx, axis=-1, keepdims=True)
               + jnp.exp(sink - m))
        o = _dot(p_loc.astype(BF16), vw) + _dot(p_ctx.astype(BF16), vc)
        o_ref[0, 0, g] = (o / den).astype(o_ref.dtype)


def swa_attention(q, k, v, kc, vc, sink):
    bsz, hk, grp, s, dh = q.shape
    lc = kc.shape[2]
    assert s >= 3 * SWA_BLOCK
    return pl.pallas_call(
        functools.partial(_swa_kernel, seq=s),
        grid=(bsz, hk, s // SWA_BLOCK),
        in_specs=[pl.BlockSpec(memory_space=pltpu.SMEM),
                  pl.BlockSpec((1, 1, grp, SWA_BLOCK, dh), lambda b, h, n: (b, h, 0, n, 0)),
                  pl.BlockSpec((1, 1, s, dh), lambda b, h, n: (b, h, 0, 0)),
                  pl.BlockSpec((1, 1, s, dh), lambda b, h, n: (b, h, 0, 0)),
                  pl.BlockSpec((1, 1, lc, dh), lambda b, h, n: (b, h, 0, 0)),
                  pl.BlockSpec((1, 1, lc, dh), lambda b, h, n: (b, h, 0, 0))],
        out_specs=pl.BlockSpec((1, 1, grp, SWA_BLOCK, dh), lambda b, h, n: (b, h, 0, n, 0)),
        out_shape=jax.ShapeDtypeStruct((bsz, hk, grp, s, dh), BF16),
        compiler_params=_cparams(("parallel", "parallel", "arbitrary")),
        name="swa_attention",
    )(sink.astype(F32), q, k, v, kc, vc)


def _ctx_attn_kernel(sink_ref, q_ref, k_ref, v_ref, o_ref, *, grp):
    kh = pl.program_id(1)
    k = k_ref[0, 0]
    v = v_ref[0, 0]
    for g in range(grp):
        q = q_ref[0, 0, g]
        s = _dot_nt(q, k)
        sink = sink_ref[kh * grp + g]
        m = jnp.maximum(jnp.max(s, axis=-1, keepdims=True), sink)
        p = jnp.exp(s - m)
        den = jnp.sum(p, axis=-1, keepdims=True) + jnp.exp(sink - m)
        o_ref[0, 0, g] = (_dot(p.astype(BF16), v) / den).astype(o_ref.dtype)


def ctx_attention(q, k, v, sink):
    bsz, hk, grp, n, dh = q.shape
    return pl.pallas_call(
        functools.partial(_ctx_attn_kernel, grp=grp),
        grid=(bsz, hk),
        in_specs=[pl.BlockSpec(memory_space=pltpu.SMEM),
                  pl.BlockSpec((1, 1, grp, n, dh), lambda b, h: (b, h, 0, 0, 0)),
                  pl.BlockSpec((1, 1, n, dh), lambda b, h: (b, h, 0, 0)),
                  pl.BlockSpec((1, 1, n, dh), lambda b, h: (b, h, 0, 0))],
        out_specs=pl.BlockSpec((1, 1, grp, n, dh), lambda b, h: (b, h, 0, 0, 0)),
        out_shape=jax.ShapeDtypeStruct((bsz, hk, grp, n, dh), BF16),
        compiler_params=_cparams(("parallel", "parallel")),
        name="ctx_attention",
    )(sink.astype(F32), q, k, v)


CONV_PAD = 8
CONV_ROWS = 256
CONV_COLS = 256


def _conv_kernel(x_ref, w_ref, b_ref, o_ref, *, seq):
    rows = min(CONV_ROWS, seq)
    n = seq // rows

    def body(i, carry):
        base = pl.multiple_of(i * rows, rows)
        acc = jnp.zeros((rows, CONV_COLS), F32) + b_ref[...]
        prev_off = pl.multiple_of(jnp.maximum(base - CONV_PAD, 0), CONV_PAD)
        next_off = pl.multiple_of(jnp.minimum(base + rows, seq - CONV_PAD), CONV_PAD)
        prev = jnp.where(i > 0, x_ref[0, pl.ds(prev_off, CONV_PAD), :], 0.0)
        nxt = jnp.where(i < n - 1, x_ref[0, pl.ds(next_off, CONV_PAD), :], 0.0)
        halo = jnp.concatenate([prev, x_ref[0, pl.ds(base, rows), :], nxt], axis=0)
        for k in range(SSD_CONV_K):
            off = CONV_PAD + k - SSD_CONV_K // 2
            acc = acc + w_ref[k:k + 1, :] * halo[off:off + rows, :]
        o_ref[0, pl.ds(base, rows), :] = _silu(acc).astype(o_ref.dtype)
        return carry

    lax.fori_loop(0, seq // rows, body, 0)


def conv_silu(xbc, w, b):
    bsz, s, c = xbc.shape
    return pl.pallas_call(
        functools.partial(_conv_kernel, seq=s),
        grid=(bsz, c // CONV_COLS),
        in_specs=[pl.BlockSpec((1, s, CONV_COLS), lambda bb, j: (bb, 0, j)),
                  pl.BlockSpec((SSD_CONV_K, CONV_COLS), lambda bb, j: (0, j)),
                  pl.BlockSpec((1, CONV_COLS), lambda bb, j: (0, j))],
        out_specs=pl.BlockSpec((1, s, CONV_COLS), lambda bb, j: (bb, 0, j)),
        out_shape=jax.ShapeDtypeStruct((bsz, s, c), BF16),
        compiler_params=_cparams(("parallel", "parallel")),
        name="conv_silu",
    )(xbc, w.astype(F32), b.reshape(1, c).astype(F32))


def _scan_masks(length, d):
    row = lax.broadcasted_iota(jnp.int32, (length, length), 0)
    col = lax.broadcasted_iota(jnp.int32, (length, length), 1)
    return (row - col) * (1 - 2 * d) >= 0


def _ssd_kernel(xs_ref, bm_ref, cm_ref, bmt_ref, dt_ref, dtt_ref, alog_ref, alogt_ref, dtb_ref, dtbt_ref,
                h0_ref, y_ref, hout_ref, state_ref, *, n_chunks):
    d = pl.program_id(0)
    c = pl.program_id(2)

    @pl.when(c == 0)
    def _():
        state_ref[...] = h0_ref[0, 0]

    length = SSD_CHUNK
    incl = _scan_masks(length, d)
    tri = incl.astype(F32)
    dt = _softplus(dt_ref[0, 0] + dtb_ref[0])
    dtt = _softplus(dtt_ref[0, 0] + dtbt_ref[0])
    ad = dt * (-jnp.exp(alog_ref[0]))
    adt = dtt * (-jnp.exp(alogt_ref[0]))
    acs = _dot(tri, ad, HI)
    acst = _dot_nt(adt, tri, HI)
    tot = jnp.sum(ad, axis=0, keepdims=True)
    tott = jnp.sum(adt, axis=1, keepdims=True)
    rep = SSD_HEADS // SSD_GROUPS
    for gi in range(SSD_GROUPS):
        gsl = slice(gi * SSD_STATE, (gi + 1) * SSD_STATE)
        cg = cm_ref[0, :, gsl]
        bg = bm_ref[0, :, gsl]
        bgt = bmt_ref[0, gsl, :].astype(F32)
        cb = _dot_nt(cg, bg)
        for hh in range(rep):
            h = gi * rep + hh
            psl = slice(h * SSD_HEADDIM, (h + 1) * SSD_HEADDIM)
            a_col = acs[:, h:h + 1]
            a_row = acst[h:h + 1, :]
            lmat = jnp.exp(jnp.where(incl, a_col - a_row, NEG))
            xdt = (xs_ref[0, :, psl].astype(F32) * dt[:, h:h + 1]).astype(BF16)
            st = state_ref[h]
            y = _dot((cb * lmat).astype(BF16), xdt)
            y = y + jnp.exp(a_col) * _dot(cg, st.astype(BF16))
            dec = jnp.exp(tott[h:h + 1, :] - a_row)
            state_ref[h] = jnp.exp(tot[:, h:h + 1]) * st + _dot((bgt * dec).astype(BF16), xdt)
            y_ref[0, 0, :, psl] = y.astype(y_ref.dtype)

    @pl.when(c == n_chunks - 1)
    def _():
        hout_ref[0, 0] = state_ref[...]


def _chunk_index(d, c, n_chunks):
    return c + d * (n_chunks - 1 - 2 * c)


def ssd_scan(xbc_act, dt_raw, a_log, dt_bias, h0):
    bsz, s, _ = xbc_act.shape
    nc = s // SSD_CHUNK
    hh = SSD_HEADS
    ng = SSD_GROUPS * SSD_STATE
    bmt = jnp.swapaxes(xbc_act[:, :, SSD_D_INNER:SSD_D_INNER + ng], 1, 2)
    dt2 = dt_raw.reshape(bsz, s, 2, hh).transpose(2, 0, 1, 3)
    dtt = dt2.transpose(0, 1, 3, 2)
    a_log = a_log.astype(F32)
    dt_bias = dt_bias.astype(F32)
    cmap = functools.partial(_chunk_index, n_chunks=nc)
    nb_x = SSD_D_INNER // ng
    return pl.pallas_call(
        functools.partial(_ssd_kernel, n_chunks=nc),
        grid=(2, bsz, nc),
        in_specs=[pl.BlockSpec((1, SSD_CHUNK, SSD_D_INNER), lambda d, b, c: (b, cmap(d, c), 0)),
                  pl.BlockSpec((1, SSD_CHUNK, ng), lambda d, b, c: (b, cmap(d, c), nb_x)),
                  pl.BlockSpec((1, SSD_CHUNK, ng), lambda d, b, c: (b, cmap(d, c), nb_x + 1)),
                  pl.BlockSpec((1, ng, SSD_CHUNK), lambda d, b, c: (b, 0, cmap(d, c))),
                  pl.BlockSpec((1, 1, SSD_CHUNK, hh), lambda d, b, c: (d, b, cmap(d, c), 0)),
                  pl.BlockSpec((1, 1, hh, SSD_CHUNK), lambda d, b, c: (d, b, 0, cmap(d, c))),
                  pl.BlockSpec((1, 1, hh), lambda d, b, c: (d, 0, 0)),
                  pl.BlockSpec((1, hh, 1), lambda d, b, c: (d, 0, 0)),
                  pl.BlockSpec((1, 1, hh), lambda d, b, c: (d, 0, 0)),
                  pl.BlockSpec((1, hh, 1), lambda d, b, c: (d, 0, 0)),
                  pl.BlockSpec((1, 1, hh, SSD_STATE, SSD_HEADDIM), lambda d, b, c: (d, b, 0, 0, 0))],
        out_specs=[pl.BlockSpec((1, 1, SSD_CHUNK, SSD_D_INNER), lambda d, b, c: (d, b, cmap(d, c), 0)),
                   pl.BlockSpec((1, 1, hh, SSD_STATE, SSD_HEADDIM), lambda d, b, c: (d, b, 0, 0, 0))],
        out_shape=[jax.ShapeDtypeStruct((2, bsz, s, SSD_D_INNER), BF16),
                   jax.ShapeDtypeStruct((2, bsz, hh, SSD_STATE, SSD_HEADDIM), F32)],
        scratch_shapes=[pltpu.VMEM((hh, SSD_STATE, SSD_HEADDIM), F32)],
        compiler_params=_cparams(("parallel", "parallel", "arbitrary")),
        name="ssd_scan",
    )(xbc_act, xbc_act, xbc_act, bmt, dt2, dtt,
      a_log.reshape(2, 1, hh), a_log.reshape(2, hh, 1), dt_bias.reshape(2, 1, hh), dt_bias.reshape(2, hh, 1), h0)


def _ssd_finish_kernel(y_ref, xs_ref, z_ref, d_ref, g_ref, o_ref):
    y = y_ref[0, 0].astype(F32) + y_ref[1, 0].astype(F32) + d_ref[...] * xs_ref[0].astype(F32)
    z = z_ref[0].astype(F32)
    u = y * _silu(z)
    var = jnp.mean(u * u, axis=-1, keepdims=True)
    o_ref[0] = (u * lax.rsqrt(var + NORM_EPS) * g_ref[...]).astype(o_ref.dtype)


def ssd_finish(y, xbc_act, z, d_skip, norm_g, ts=512):
    _, bsz, s, di = y.shape
    ts = min(ts, s)
    dvec = jnp.repeat(d_skip.astype(F32), SSD_HEADDIM).reshape(1, di)
    return pl.pallas_call(
        _ssd_finish_kernel,
        grid=(bsz, s // ts),
        in_specs=[pl.BlockSpec((2, 1, ts, di), lambda b, i: (0, b, i, 0)),
                  pl.BlockSpec((1, ts, di), lambda b, i: (b, i, 0)),
                  pl.BlockSpec((1, ts, di), lambda b, i: (b, i, 0)),
                  pl.BlockSpec((1, di), lambda b, i: (0, 0)),
                  pl.BlockSpec((1, di), lambda b, i: (0, 0))],
        out_specs=pl.BlockSpec((1, ts, di), lambda b, i: (b, i, 0)),
        out_shape=jax.ShapeDtypeStruct((bsz, s, di), BF16),
        compiler_params=_cparams(("parallel", "parallel")),
        name="ssd_finish",
    )(y, xbc_act, z, dvec, norm_g.reshape(1, di).astype(F32))


GLA_STEP = 128


def _gla_kernel(q_ref, k_ref, kt_ref, v_ref, glr_ref, glrt_ref, wg_ref, wgt_ref, bg_ref, bgt_ref,
                s0_ref, o_ref, sout_ref, state_ref, *, n_steps, reverse):
    c = pl.program_id(1)

    @pl.when(c == 0)
    def _():
        state_ref[...] = s0_ref[0]

    length = GLA_CHUNK
    incl = _scan_masks(length, int(reverse))
    tri = incl.astype(F32)
    subs = range(GLA_STEP // length)
    for sub in (reversed(subs) if reverse else subs):
        tsl = slice(sub * length, (sub + 1) * length)
        g = _log_sigmoid(_dot(glr_ref[0, tsl, :], wg_ref[...], HI) + bg_ref[...]) / GLA_TAU
        gt = _log_sigmoid(_dot(wgt_ref[...], glrt_ref[0, :, tsl], HI) + bgt_ref[...]) / GLA_TAU
        gc = _dot(tri, g, HI)
        gct = _dot_nt(gt, tri, HI)
        tott = jnp.sum(gt, axis=1, keepdims=True)
        q_in = (q_ref[0, tsl, :].astype(F32) * (GLA_HEAD_K ** -0.5) * jnp.exp(gc)).astype(BF16)
        k_in = (k_ref[0, tsl, :].astype(F32) * jnp.exp(-gc)).astype(BF16)
        k_out_t = (kt_ref[0, :, tsl].astype(F32) * jnp.exp(tott - gct)).astype(BF16)
        dec_t = jnp.exp(tott)
        for h in range(GLA_HEADS):
            ksl = slice(h * GLA_HEAD_K, (h + 1) * GLA_HEAD_K)
            vsl = slice(h * GLA_HEAD_V, (h + 1) * GLA_HEAD_V)
            qh = q_in[:, ksl]
            vh = v_ref[0, tsl, vsl]
            att = jnp.where(incl, _dot_nt(qh, k_in[:, ksl]), 0.0)
            st = state_ref[h]
            o = _dot(att.astype(BF16), vh) + _dot(qh, st.astype(BF16))
            state_ref[h] = st * dec_t[ksl, :] + _dot(k_out_t[ksl, :], vh)
            o_ref[0, tsl, vsl] = o.astype(o_ref.dtype)

    @pl.when(c == n_steps - 1)
    def _():
        sout_ref[0] = state_ref[...]


def gla_scan(q, k, v, glr, w_gate, b_gate, s0):
    bsz, s, kd = q.shape
    vd = v.shape[-1]
    ns = s // GLA_STEP
    r = GLA_GATE_RANK
    kt = jnp.swapaxes(k, 1, 2)
    w_gate = w_gate.astype(F32)
    b_gate = b_gate.astype(F32)
    outs, states = [], []
    for d in range(2):
        cmap = (lambda c: ns - 1 - c) if d else (lambda c: c)
        glr_d = glr[:, :, d * r:(d + 1) * r]
        o, st = pl.pallas_call(
            functools.partial(_gla_kernel, n_steps=ns, reverse=bool(d)),
            grid=(bsz, ns),
            in_specs=[pl.BlockSpec((1, GLA_STEP, kd), lambda b, c, cmap=cmap: (b, cmap(c), 0)),
                      pl.BlockSpec((1, GLA_STEP, kd), lambda b, c, cmap=cmap: (b, cmap(c), 0)),
                      pl.BlockSpec((1, kd, GLA_STEP), lambda b, c, cmap=cmap: (b, 0, cmap(c))),
                      pl.BlockSpec((1, GLA_STEP, vd), lambda b, c, cmap=cmap: (b, cmap(c), 0)),
                      pl.BlockSpec((1, GLA_STEP, r), lambda b, c, cmap=cmap: (b, cmap(c), 0)),
                      pl.BlockSpec((1, r, GLA_STEP), lambda b, c, cmap=cmap: (b, 0, cmap(c))),
                      pl.BlockSpec((r, kd), lambda b, c: (0, 0)),
                      pl.BlockSpec((kd, r), lambda b, c: (0, 0)),
                      pl.BlockSpec((1, kd), lambda b, c: (0, 0)),
                      pl.BlockSpec((kd, 1), lambda b, c: (0, 0)),
                      pl.BlockSpec((1, GLA_HEADS, GLA_HEAD_K, GLA_HEAD_V), lambda b, c: (b, 0, 0, 0))],
            out_specs=[pl.BlockSpec((1, GLA_STEP, vd), lambda b, c, cmap=cmap: (b, cmap(c), 0)),
                       pl.BlockSpec((1, GLA_HEADS, GLA_HEAD_K, GLA_HEAD_V), lambda b, c: (b, 0, 0, 0))],
            out_shape=[jax.ShapeDtypeStruct((bsz, s, vd), BF16),
                       jax.ShapeDtypeStruct((bsz, GLA_HEADS, GLA_HEAD_K, GLA_HEAD_V), F32)],
            scratch_shapes=[pltpu.VMEM((GLA_HEADS, GLA_HEAD_K, GLA_HEAD_V), F32)],
            compiler_params=_cparams(("parallel", "arbitrary")),
            name="gla_scan_bwd" if d else "gla_scan_fwd",
        )(q, k, kt, v, glr_d, jnp.swapaxes(glr_d, 1, 2), w_gate[d], w_gate[d].T,
          b_gate[d].reshape(1, kd), b_gate[d].reshape(kd, 1), s0[d])
        outs.append(o)
        states.append(st)
    return outs, states


def _gla_finish_kernel(of_ref, ob_ref, r_ref, g_ref, out_ref):
    o = of_ref[0].astype(F32) + ob_ref[0].astype(F32)
    r = r_ref[0].astype(F32)
    for h in range(GLA_HEADS):
        vsl = slice(h * GLA_HEAD_V, (h + 1) * GLA_HEAD_V)
        oh = o[:, vsl]
        var = jnp.mean(oh * oh, axis=-1, keepdims=True)
        y = oh * lax.rsqrt(var + NORM_EPS) * g_ref[...]
        out_ref[0, :, vsl] = (y * _silu(r[:, vsl])).astype(out_ref.dtype)


def gla_finish(o, r, norm_g, ts=512):
    bsz, s, vd = o[0].shape
    ts = min(ts, s)
    return pl.pallas_call(
        _gla_finish_kernel,
        grid=(bsz, s // ts),
        in_specs=[pl.BlockSpec((1, ts, vd), lambda b, i: (b, i, 0)),
                  pl.BlockSpec((1, ts, vd), lambda b, i: (b, i, 0)),
                  pl.BlockSpec((1, ts, vd), lambda b, i: (b, i, 0)),
                  pl.BlockSpec((1, GLA_HEAD_V), lambda b, i: (0, 0))],
        out_specs=pl.BlockSpec((1, ts, vd), lambda b, i: (b, i, 0)),
        out_shape=jax.ShapeDtypeStruct((bsz, s, vd), BF16),
        compiler_params=_cparams(("parallel", "parallel")),
        name="gla_finish",
    )(o[0], o[1], r, norm_g.reshape(1, GLA_HEAD_V).astype(F32))


PEER_SCORE_TOKENS = 128
PEER_TOKENS = 128
PEER_SLOTS = 6
PEER_CAND_ROWS = -(-sum(PEER_TOPK // (a + 1) for a in range(PEER_TOPK)) // 8) * 8
PEER_SLAB = 2 * (D_MODEL // 2) // V7X_LANES
PEER_PITCH = PEER_SLAB + 4


def _extract_topk(s, rows, vals_ref, pos_ref):
    iota = lax.broadcasted_iota(jnp.int32, s.shape, 0).astype(F32)

    def body(j, cur):
        m = jnp.max(cur, axis=0, keepdims=True)
        pos = jnp.min(jnp.where(cur == m, iota, float(rows)), axis=0, keepdims=True)
        vals_ref[pl.ds(j, 1), :] = m
        pos_ref[pl.ds(j, 1), :] = pos
        return jnp.where(iota == pos, -jnp.inf, cur)

    lax.fori_loop(0, PEER_TOPK, body, s)


def _peer_score_kernel(q_ref, k1_ref, k2_ref, idx_ref, gate_ref, v12_ref, p12_ref, vt_ref, pt_ref, cand_ref):
    half = PEER_QDIM // 2
    tt = q_ref.shape[0]
    q = q_ref[...]
    s12 = jnp.concatenate([_dot_nt(k1_ref[...], q[:, :half], HI), _dot_nt(k2_ref[...], q[:, half:], HI)], axis=1)
    _extract_topk(s12, PEER_NKEYS, v12_ref, p12_ref)
    v1 = v12_ref[:, 0:tt]
    v2 = v12_ref[:, tt:2 * tt]
    p1 = p12_ref[:, 0:tt]
    p2 = p12_ref[:, tt:2 * tt]
    widths = [PEER_TOPK // (a + 1) for a in range(PEER_TOPK)]
    starts = [sum(widths[:a]) for a in range(PEER_TOPK)]
    n_cand = sum(widths)
    for a in range(PEER_TOPK):
        cand_ref[starts[a]:starts[a] + widths[a], :] = v1[a:a + 1, :] + v2[0:widths[a], :]
    cand_ref[n_cand:, :] = jnp.full((cand_ref.shape[0] - n_cand, tt), -jnp.inf, F32)
    _extract_topk(cand_ref[...], cand_ref.shape[0], vt_ref, pt_ref)
    pos = pt_ref[...]
    ia = jnp.zeros_like(pos)
    ib = pos
    for a in range(1, PEER_TOPK):
        past = pos >= float(starts[a])
        ia = ia + jnp.where(past, 1.0, 0.0)
        ib = ib - jnp.where(past, float(widths[a - 1]), 0.0)
    e1 = jnp.zeros_like(pos)
    e2 = jnp.zeros_like(pos)
    for a in range(PEER_TOPK):
        e1 = e1 + jnp.where(ia == float(a), p1[a:a + 1, :], 0.0)
        e2 = e2 + jnp.where(ib == float(a), p2[a:a + 1, :], 0.0)
    idx_ref[...] = (e1 * float(PEER_NKEYS) + e2).astype(jnp.int32)
    top = vt_ref[...]
    p = jnp.exp(top - top[0:1, :])
    gate_ref[...] = p / jnp.sum(p, axis=0, keepdims=True)


def peer_retrieve(q, k1, k2):
    t = q.shape[0]
    tt = min(PEER_SCORE_TOKENS, t)
    kk = PEER_TOPK
    return pl.pallas_call(
        _peer_score_kernel,
        grid=(t // tt, PEER_HEADS),
        in_specs=[pl.BlockSpec((tt, PEER_QDIM), lambda i, h: (i, h)),
                  pl.BlockSpec((PEER_NKEYS, PEER_QDIM // 2), lambda i, h: (0, 0)),
                  pl.BlockSpec((PEER_NKEYS, PEER_QDIM // 2), lambda i, h: (0, 0))],
        out_specs=[pl.BlockSpec((kk, tt), lambda i, h: (h, i)),
                   pl.BlockSpec((kk, tt), lambda i, h: (h, i))],
        out_shape=[jax.ShapeDtypeStruct((PEER_NSEL, t), jnp.int32),
                   jax.ShapeDtypeStruct((PEER_NSEL, t), F32)],
        scratch_shapes=[pltpu.VMEM((kk, 2 * tt), F32), pltpu.VMEM((kk, 2 * tt), F32),
                        pltpu.VMEM((kk, tt), F32), pltpu.VMEM((kk, tt), F32),
                        pltpu.VMEM((PEER_CAND_ROWS, tt), F32)],
        compiler_params=_cparams(("parallel", "parallel")),
        name="peer_retrieve",
    )(q, k1.astype(F32), k2.astype(F32))


def _gelu(x):
    return 0.5 * x * (1.0 + lax.erf(x * (1.0 / math.sqrt(2.0))))


def pack_expert_tables(u, v):
    def pack(tab):
        bits = lax.bitcast_convert_type(tab.astype(BF16), jnp.uint16).astype(jnp.uint32)
        half = tab.shape[1] // 2
        return bits[:, :half] | (bits[:, half:] << 16)

    n_exp = u.shape[0]
    slabs = jnp.concatenate([pack(u).reshape(n_exp, -1, V7X_LANES), pack(v).reshape(n_exp, -1, V7X_LANES)], axis=1)
    return slabs.reshape(n_exp * PEER_SLAB, V7X_LANES)


def _unpack_pair(words):
    lo = pltpu.bitcast(words << 16, F32)
    hi = pltpu.bitcast(words & jnp.uint32(0xFFFF0000), F32)
    return lo, hi


def _peer_expert_kernel(idx_ref, gate_ref, h_ref, x_ref, gt_ref, uv_hbm, o_ref, *scratch, tokens):
    bufs, sem = scratch[:PEER_SLOTS], scratch[PEER_SLOTS]
    half = D_MODEL // 2
    n_chunks = PEER_SLAB // 2
    per_step = PEER_NSEL // PEER_SLAB
    depth = PEER_SLOTS - 1

    def row_copy(e, slot, k):
        src = uv_hbm.at[pl.ds(pl.multiple_of(e * PEER_SLAB, PEER_SLAB), PEER_SLAB), :]
        return pltpu.make_async_copy(src, bufs[slot].at[pl.ds(k * PEER_PITCH, PEER_SLAB), :], sem.at[slot])

    def issue(tok, slot, ks):
        for k in ks:
            row_copy(idx_ref[tok, k], slot, k).start(priority=k % 2)

    def wait(slot):
        n = PEER_NSEL * PEER_SLAB
        pltpu.make_async_copy(uv_hbm.at[pl.ds(0, n), :], bufs[slot].at[pl.ds(0, n), :], sem.at[slot]).wait()

    lane = lax.broadcasted_iota(jnp.int32, (PEER_NSEL, tokens), 1)

    def chunk(slot, row):
        return _unpack_pair(bufs[slot][pl.ds(row, PEER_NSEL, stride=PEER_PITCH), :])

    def compute(j, slot, prefetch):
        def maybe_issue(step):
            if prefetch:
                issue(j + depth, (slot + depth) % PEER_SLOTS, range(step * per_step, (step + 1) * per_step))

        gcol = jnp.sum(jnp.where(lane == j, gate_ref[...], 0.0), axis=1, keepdims=True)
        p = jnp.zeros((PEER_NSEL, V7X_LANES), F32)
        t_lo = h_ref[pl.ds(j, 1), 0:half]
        t_hi = h_ref[pl.ds(j, 1), half:2 * half]
        for c in range(n_chunks):
            maybe_issue(c)
            ua, ub = chunk(slot, c)
            cols = slice(c * V7X_LANES, (c + 1) * V7X_LANES)
            p = p + ua * t_lo[:, cols] + ub * t_hi[:, cols]
        act = jnp.sum(p, axis=1, keepdims=True)
        a = jnp.broadcast_to(_gelu(act) * gcol, (PEER_NSEL, V7X_LANES))
        out_lo, out_hi = [], []
        for c in range(n_chunks):
            maybe_issue(n_chunks + c)
            va, vb = chunk(slot, n_chunks + c)
            out_lo.append(jnp.sum(a * va, axis=0, keepdims=True))
            out_hi.append(jnp.sum(a * vb, axis=0, keepdims=True))
        gt = gt_ref[0]
        for cols, parts in ((slice(0, half), out_lo), (slice(half, 2 * half), out_hi)):
            o_ref[0, pl.ds(j, 1), cols] = (x_ref[0, pl.ds(j, 1), cols]
                                           + gt[:, cols] * jnp.concatenate(parts, axis=1))

    for j0 in range(depth):
        issue(j0, j0 % PEER_SLOTS, range(PEER_NSEL))

    def body(it, carry):
        for s in range(PEER_SLOTS):
            wait(s)
            compute(it * PEER_SLOTS + s, s, True)
        return carry

    n_main = (tokens - depth) // PEER_SLOTS
    lax.fori_loop(0, n_main, body, 0)
    for j in range(n_main * PEER_SLOTS, tokens):
        wait(j % PEER_SLOTS)
        compute(j, j % PEER_SLOTS, j + depth < tokens)


def peer_experts(idx, gate, h, x, gt, uv):
    bsz, s, dm = x.shape
    t = bsz * s
    tt = min(PEER_TOKENS, s)
    nt = s // tt
    per_batch = gt.shape[0] == bsz and bsz > 1
    g_map = (lambda b, i: (b, 0, 0)) if per_batch else (lambda b, i: (0, 0, 0))
    out = pl.pallas_call(
        functools.partial(_peer_expert_kernel, tokens=tt),
        grid=(bsz, nt),
        in_specs=[pl.BlockSpec((tt, PEER_NSEL), lambda b, i: (b * nt + i, 0), memory_space=pltpu.SMEM),
                  pl.BlockSpec((PEER_NSEL, tt), lambda b, i: (0, b * nt + i)),
                  pl.BlockSpec((tt, dm), lambda b, i: (b * nt + i, 0)),
                  pl.BlockSpec((1, tt, dm), lambda b, i: (b, i, 0)),
                  pl.BlockSpec((1, 1, dm), g_map),
                  pl.BlockSpec(memory_space=pl.ANY)],
        out_specs=pl.BlockSpec((1, tt, dm), lambda b, i: (b, i, 0)),
        out_shape=jax.ShapeDtypeStruct((bsz, s, dm), F32),
        scratch_shapes=[pltpu.VMEM((PEER_NSEL * PEER_PITCH, V7X_LANES), jnp.uint32) for _ in range(PEER_SLOTS)]
        + [pltpu.SemaphoreType.DMA((PEER_SLOTS,))],
        compiler_params=pltpu.CompilerParams(dimension_semantics=("arbitrary", "arbitrary"),
                                             vmem_limit_bytes=VMEM_LIMIT),
        name="peer_experts",
    )(idx.T, gate, h, x.reshape(bsz, s, dm), gt, uv)
    return out


def peer_sublayer(x, g_norm, shift, scale, gt, wq, k1, k2, uv):
    bsz, s, dm = x.shape
    h_bf, h_f32 = norm_modulate(x, g_norm, shift, scale, (BF16, F32))
    q = matmul(h_bf.reshape(bsz * s, dm), wq, F32)
    idx, gate = peer_retrieve(q, k1, k2)
    return peer_experts(idx, gate, h_f32.reshape(bsz * s, dm), x, gt, uv)


def _head_major(t, n_heads):
    bsz, s, _ = t.shape
    return t.reshape(bsz, s, n_heads, -1).transpose(0, 2, 1, 3)


_PROJ_GROUPS = dict(z=(0, BF16), xbc=(1, F32), na=(3, BF16), gla_q=(4, BF16), gla_k=(5, BF16), gla_v=(6, BF16),
                    gla_r=(7, BF16), swa_q=(9, BF16), swa_k=(10, BF16), swa_v=(11, BF16), gates=(12, BF16))


def _split_w_in(w):
    o = IN_OFFS
    ws = {name: w[:, o[i]:o[i + 1]].astype(BF16) for name, (i, _) in _PROJ_GROUPS.items()}
    small = jnp.concatenate([w[:, o[2]:o[3]], w[:, o[8]:o[9]]], axis=1)
    ws['small'] = jnp.pad(small, ((0, 0), (0, V7X_LANES - small.shape[1]))).astype(BF16)
    return ws


def _project(h, ws):
    bsz, s, dm = h.shape
    hf = h.reshape(bsz * s, dm)
    out = {name: matmul(hf, ws[name], dt).reshape(bsz, s, -1) for name, (_, dt) in _PROJ_GROUPS.items()}
    sm = matmul(hf, ws['small'], F32).reshape(bsz, s, V7X_LANES)
    out['dt'] = sm[:, :, :2 * SSD_HEADS]
    out['glr'] = sm[:, :, 2 * SSD_HEADS:2 * SSD_HEADS + 2 * GLA_GATE_RANK]
    return out


def _flat_heads(t):
    return t.reshape((-1,) + t.shape[-2:])


def mixer_sublayer(hx, hc, p, rope, need_ctx):
    bsz, s, _ = hx.shape
    lc = hc.shape[1]
    ws = _split_w_in(p['w_in'])
    px = _project(hx, ws)
    pc = _project(hc, ws)
    zeros_h = jnp.zeros((2, bsz, SSD_HEADS, SSD_STATE, SSD_HEADDIM), F32)
    zero_s = jnp.zeros((bsz, GLA_HEADS, GLA_HEAD_K, GLA_HEAD_V), F32)
    zeros_s = (zero_s, zero_s)

    act_c = conv_silu(pc['xbc'], p['ssd_conv_w'], p['ssd_conv_b'])
    act_x = conv_silu(px['xbc'], p['ssd_conv_w'], p['ssd_conv_b'])
    y_c, h_c = ssd_scan(act_c, pc['dt'], p['ssd_a_log'], p['ssd_dt_bias'], zeros_h)
    y_x, _ = ssd_scan(act_x, px['dt'], p['ssd_a_log'], p['ssd_dt_bias'], h_c)
    a_x = ssd_finish(y_x, act_x, px['z'], p['ssd_d'], p['ssd_norm_g'])

    dh = NA_HEAD_DIM
    scale = dh ** -0.5

    def na_heads(t):
        t = _head_major(t, 3 * NA_HEADS)
        q = head_prep(_flat_heads(t[:, :NA_HEADS]), p['na_q_norm'], None, scale).reshape(t[:, :NA_HEADS].shape)
        k = head_prep(_flat_heads(t[:, NA_HEADS:2 * NA_HEADS]), p['na_k_norm'], None, 1.0).reshape(q.shape)
        return q, k, t[:, 2 * NA_HEADS:]

    b_x = na_attention(px['na'], pc['na'], p['na_q_norm'], p['na_k_norm'], na_bias_table(p['na_rpb']))

    o_c, s_c = gla_scan(pc['gla_q'], pc['gla_k'], pc['gla_v'], pc['glr'], p['gla_w_gate'], p['gla_b_gate'], zeros_s)
    o_x, _ = gla_scan(px['gla_q'], px['gla_k'], px['gla_v'], px['glr'], p['gla_w_gate'], p['gla_b_gate'], s_c)
    c_x = gla_finish(o_x, px['gla_r'], p['gla_norm_g'])

    def swa_heads(q, k, v, tabs):
        qh = _head_major(q, SWA_HEADS)
        kh = _head_major(k, SWA_KV_HEADS)
        qn = head_prep(_flat_heads(qh), p['swa_q_norm'], tabs, SWA_HEAD_DIM ** -0.5).reshape(qh.shape)
        kn = head_prep(_flat_heads(kh), p['swa_k_norm'], tabs, 1.0).reshape(kh.shape)
        qn = qn.reshape(qh.shape[0], SWA_KV_HEADS, SWA_GROUP, qh.shape[2], SWA_HEAD_DIM)
        return qn, kn, _head_major(v, SWA_KV_HEADS)

    sq, sk, sv = swa_heads(px['swa_q'], px['swa_k'], px['swa_v'], rope)
    sqc, skc, svc = swa_heads(pc['swa_q'], pc['swa_k'], pc['swa_v'], None)
    d_x = swa_attention(sq, sk, sv, skc, svc, p['swa_sink'])
    d_x = d_x.reshape(bsz, SWA_HEADS, s, SWA_HEAD_DIM).transpose(0, 2, 1, 3).reshape(bsz, s, BRANCH_WIDTH)

    wb = p['w_branch'].astype(BF16)
    flat = lambda t: t.reshape(-1, t.shape[-1])
    m_x = merge_branches([flat(a_x), flat(b_x), flat(c_x), flat(d_x)], flat(px['gates']), wb)
    if not need_ctx:
        return m_x, None

    a_c = ssd_finish(y_c, act_c, pc['z'], p['ssd_d'], p['ssd_norm_g'])
    no_sink = jnp.full((NA_HEADS,), NEG, F32)
    nqc, nkc, nvc = na_heads(pc['na'])
    b_c = ctx_attention(nqc[:, :, None], nkc, nvc, no_sink)[:, :, 0]
    b_c = b_c.transpose(0, 2, 1, 3).reshape(bsz, lc, BRANCH_WIDTH)
    c_c = gla_finish(o_c, pc['gla_r'], p['gla_norm_g'])
    d_c = ctx_attention(sqc, skc, svc, p['swa_sink'])
    d_c = d_c.reshape(bsz, SWA_HEADS, lc, SWA_HEAD_DIM).transpose(0, 2, 1, 3).reshape(bsz, lc, BRANCH_WIDTH)
    m_c = merge_branches([flat(a_c), flat(b_c), flat(c_c), flat(d_c)], flat(pc['gates']), wb)
    return m_x, m_c


def kernel(x, c, ctx, c_ctx, w_ada, b_ada, g_norm1, g_norm2, w_in, ssd_conv_w, ssd_conv_b, ssd_a_log, ssd_dt_bias, ssd_d, ssd_norm_g, na_q_norm, na_k_norm, na_rpb, gla_w_gate, gla_b_gate, gla_norm_g, swa_q_norm, swa_k_norm, swa_sink, w_branch, w_out, peer_wq, peer_k1, peer_k2, peer_u, peer_v):
    bsz, s, dm = x.shape
    rope = rope_tables(s)
    n_cond = 8
    cc = jnp.zeros((n_cond, dm), F32).at[:bsz].set(c).at[bsz].set(c_ctx)
    for l in range(DEPTH):
        need_ctx = l < DEPTH - 1
        mod = ada_mod(cc, w_ada[l], b_ada[l])
        mx = [mod[:bsz, i * dm:(i + 1) * dm].reshape(bsz, 1, dm) for i in range(6)]
        mc = [mod[bsz:bsz + 1, i * dm:(i + 1) * dm].reshape(1, 1, dm) for i in range(6)]
        p = dict(w_in=w_in[l], ssd_conv_w=ssd_conv_w[l], ssd_conv_b=ssd_conv_b[l], ssd_a_log=ssd_a_log[l],
                 ssd_dt_bias=ssd_dt_bias[l], ssd_d=ssd_d[l], ssd_norm_g=ssd_norm_g[l],
                 na_q_norm=na_q_norm[l], na_k_norm=na_k_norm[l], na_rpb=na_rpb[l],
                 gla_w_gate=gla_w_gate[l], gla_b_gate=gla_b_gate[l], gla_norm_g=gla_norm_g[l],
                 swa_q_norm=swa_q_norm[l], swa_k_norm=swa_k_norm[l], swa_sink=swa_sink[l],
                 w_branch=w_branch[l])
        (hx,) = norm_modulate(x, g_norm1[l], mx[0], mx[1], (BF16,))
        (hc,) = norm_modulate(ctx, g_norm1[l], mc[0], mc[1], (BF16,))
        m_x, m_c = mixer_sublayer(hx, hc, p, rope, need_ctx)
        wo = w_out[l].astype(BF16)
        wq = peer_wq[l].astype(BF16)
        uv = pack_expert_tables(peer_u[l], peer_v[l])
        x = matmul_gated_residual(m_x, wo, x, mx[2])
        x = peer_sublayer(x, g_norm2[l], mx[3], mx[4], mx[5], wq, peer_k1[l], peer_k2[l], uv)
        if need_ctx:
            ctx = matmul_gated_residual(m_c, wo, ctx, mc[2])
            ctx = peer_sublayer(ctx, g_norm2[l], mc[3], mc[4], mc[5], wq, peer_k1[l], peer_k2[l], uv)
    return x
```

```python
import functools
import math

import numpy as np
import jax
import jax.numpy as jnp
from jax import lax
from jax.experimental import pallas as pl
from jax.experimental.pallas import tpu as pltpu

D_MODEL = 2048
DEPTH = 2
GRID_W = 64
NORM_EPS = 1e-6
N_BRANCH = 4
BRANCH_WIDTH = D_MODEL // 2

SSD_HEADDIM = 64
SSD_D_INNER = BRANCH_WIDTH
SSD_HEADS = SSD_D_INNER // SSD_HEADDIM
SSD_GROUPS = 4
SSD_STATE = 128
SSD_CONV_DIM = SSD_D_INNER + 2 * SSD_GROUPS * SSD_STATE
SSD_CONV_K = 5
SSD_CHUNK = 128

NA_HEAD_DIM = 64
NA_HEADS = BRANCH_WIDTH // NA_HEAD_DIM
NA_WIN_R = 8
NA_WIN_C = 16

GLA_HEADS = 4
GLA_V_DIM = BRANCH_WIDTH
GLA_K_DIM = BRANCH_WIDTH // 2
GLA_HEAD_K = GLA_K_DIM // GLA_HEADS
GLA_HEAD_V = GLA_V_DIM // GLA_HEADS
GLA_GATE_RANK = 16
GLA_TAU = 16.0
GLA_CHUNK = 64

SWA_HEAD_DIM = 64
SWA_HEADS = BRANCH_WIDTH // SWA_HEAD_DIM
SWA_KV_HEADS = 4
SWA_GROUP = SWA_HEADS // SWA_KV_HEADS
SWA_WINDOW = 128
SWA_BLOCK = 128
ROPE_BASE = 10000.0
ROPE_AXIS_DIM = SWA_HEAD_DIM // 2

PEER_HEADS = 8
PEER_NKEYS = 128
PEER_QDIM = 256
PEER_TOPK = 16
PEER_NSEL = PEER_HEADS * PEER_TOPK

IN_SPLITS = (SSD_D_INNER, SSD_CONV_DIM, 2 * SSD_HEADS,
             3 * BRANCH_WIDTH,
             GLA_K_DIM, GLA_K_DIM, GLA_V_DIM, GLA_V_DIM, 2 * GLA_GATE_RANK,
             SWA_HEADS * SWA_HEAD_DIM, SWA_KV_HEADS * SWA_HEAD_DIM, SWA_KV_HEADS * SWA_HEAD_DIM,
             N_BRANCH * D_MODEL)
IN_OFFS = tuple(int(v) for v in np.cumsum((0,) + IN_SPLITS))

V7X_LANES = 128
V7X_VMEM_BYTES = 64 * 1024 * 1024
VMEM_LIMIT = 48 * 1024 * 1024

F32 = jnp.float32
BF16 = jnp.bfloat16
HI = lax.Precision.HIGHEST
NEG = -1e30


def _cparams(sem):
    return pltpu.CompilerParams(dimension_semantics=sem, vmem_limit_bytes=VMEM_LIMIT)


def _dot(a, b, precision=None):
    return jnp.dot(a, b, preferred_element_type=F32, precision=precision)


def _dot_nt(a, b, precision=None):
    return lax.dot_general(a, b, (((1,), (1,)), ((), ())), preferred_element_type=F32,
                           precision=precision)


def _silu(x):
    return x / (1.0 + jnp.exp(-x))


def _softplus(x):
    return jnp.maximum(x, 0.0) + jnp.log1p(jnp.exp(-jnp.abs(x)))


def _log_sigmoid(x):
    return jnp.minimum(x, 0.0) - jnp.log1p(jnp.exp(-jnp.abs(x)))


def _mm_kernel(a_ref, b_ref, o_ref):
    o_ref[...] = _dot(a_ref[...], b_ref[...]).astype(o_ref.dtype)


def matmul(a, b, out_dtype, tm=1024, tn=512):
    m, k = a.shape
    n = b.shape[1]
    tm = min(tm, m)
    if n % (2 * tn) == 0:
        tn = 2 * tn
    tn = min(tn, n)
    assert m % tm == 0 and n % tn == 0, (m, n, tm, tn)
    return pl.pallas_call(
        _mm_kernel,
        grid=(m // tm, n // tn),
        in_specs=[pl.BlockSpec((tm, k), lambda i, j: (i, 0)),
                  pl.BlockSpec((k, tn), lambda i, j: (0, j))],
        out_specs=pl.BlockSpec((tm, tn), lambda i, j: (i, j)),
        out_shape=jax.ShapeDtypeStruct((m, n), out_dtype),
        compiler_params=_cparams(("parallel", "parallel")),
        name="matmul",
    )(a, b)


def _mod_kernel(c_ref, w_ref, b_ref, o_ref):
    a = _silu(c_ref[...]).astype(BF16)
    o_ref[...] = _dot(a, w_ref[...].astype(BF16)) + b_ref[...]


def ada_mod(cc, w, b, tn=1024):
    m, k = cc.shape
    n = w.shape[1]
    return pl.pallas_call(
        _mod_kernel,
        grid=(n // tn,),
        in_specs=[pl.BlockSpec((m, k), lambda j: (0, 0)),
                  pl.BlockSpec((k, tn), lambda j: (0, j)),
                  pl.BlockSpec((1, tn), lambda j: (0, j))],
        out_specs=pl.BlockSpec((m, tn), lambda j: (0, j)),
        out_shape=jax.ShapeDtypeStruct((m, n), F32),
        compiler_params=_cparams(("parallel",)),
        name="ada_mod",
    )(cc, w, b.reshape(1, n))


def _normmod_kernel(x_ref, g_ref, sh_ref, sc_ref, *o_refs):
    x = x_ref[0]
    var = jnp.mean(x * x, axis=-1, keepdims=True)
    y = x * lax.rsqrt(var + NORM_EPS) * g_ref[...]
    y = y * (1.0 + sc_ref[0]) + sh_ref[0]
    for o_ref in o_refs:
        o_ref[0] = y.astype(o_ref.dtype)


def norm_modulate(x, g, shift, scale, out_dtypes, ts=512):
    bsz, s, d = x.shape
    ts = min(ts, s)
    per_batch = shift.shape[0] == bsz and bsz > 1
    mod_map = (lambda b, i: (b, 0, 0)) if per_batch else (lambda b, i: (0, 0, 0))
    outs = pl.pallas_call(
        _normmod_kernel,
        grid=(bsz, s // ts),
        in_specs=[pl.BlockSpec((1, ts, d), lambda b, i: (b, i, 0)),
                  pl.BlockSpec((1, d), lambda b, i: (0, 0)),
                  pl.BlockSpec((1, 1, d), mod_map),
                  pl.BlockSpec((1, 1, d), mod_map)],
        out_specs=[pl.BlockSpec((1, ts, d), lambda b, i: (b, i, 0)) for _ in out_dtypes],
        out_shape=[jax.ShapeDtypeStruct((bsz, s, d), dt) for dt in out_dtypes],
        compiler_params=_cparams(("parallel", "parallel")),
        name="norm_modulate",
    )(x, g.reshape(1, d), shift, scale)
    return outs


def _mm_resid_kernel(a_ref, w_ref, x_ref, gt_ref, o_ref):
    y = _dot(a_ref[...], w_ref[...])
    o_ref[0] = x_ref[0] + gt_ref[0] * y


def matmul_gated_residual(a, w, x, gate, tm=1024, tn=512):
    bsz, s, n = x.shape
    k = a.shape[1]
    tm = min(tm, s)
    nt = s // tm
    per_batch = gate.shape[0] == bsz and bsz > 1
    g_map = (lambda b, i, j: (b, 0, j)) if per_batch else (lambda b, i, j: (0, 0, j))
    return pl.pallas_call(
        _mm_resid_kernel,
        grid=(bsz, nt, n // tn),
        in_specs=[pl.BlockSpec((tm, k), lambda b, i, j: (b * nt + i, 0)),
                  pl.BlockSpec((k, tn), lambda b, i, j: (0, j)),
                  pl.BlockSpec((1, tm, tn), lambda b, i, j: (b, i, j)),
                  pl.BlockSpec((1, 1, tn), g_map)],
        out_specs=pl.BlockSpec((1, tm, tn), lambda b, i, j: (b, i, j)),
        out_shape=jax.ShapeDtypeStruct((bsz, s, n), F32),
        compiler_params=_cparams(("parallel", "parallel", "parallel")),
        name="out_proj_residual",
    )(a, w, x, gate)


def _merge_kernel(a_ref, b_ref, c_ref, d_ref, g0_ref, g1_ref, g2_ref, g3_ref, w_ref, o_ref):
    acc = None
    for i, (br, gr) in enumerate(((a_ref, g0_ref), (b_ref, g1_ref), (c_ref, g2_ref), (d_ref, g3_ref))):
        y = _dot(br[...], w_ref[i])
        gate = 1.0 / (1.0 + jnp.exp(-gr[...].astype(F32)))
        acc = gate * y if acc is None else acc + gate * y
    o_ref[...] = acc.astype(o_ref.dtype)


def merge_branches(outs, gate_logits, w_branch, tm=512, tn=512):
    t, kb = outs[0].shape
    d = w_branch.shape[-1]
    tm = min(tm, t)
    nj = d // tn
    in_specs = [pl.BlockSpec((tm, kb), lambda i, j: (i, 0)) for _ in range(N_BRANCH)]
    in_specs += [pl.BlockSpec((tm, tn), functools.partial(lambda i, j, br: (i, br * nj + j), br=br))
                 for br in range(N_BRANCH)]
    in_specs += [pl.BlockSpec((N_BRANCH, kb, tn), lambda i, j: (0, 0, j))]
    return pl.pallas_call(
        _merge_kernel,
        grid=(t // tm, nj),
        in_specs=in_specs,
        out_specs=pl.BlockSpec((tm, tn), lambda i, j: (i, j)),
        out_shape=jax.ShapeDtypeStruct((t, d), BF16),
        compiler_params=_cparams(("parallel", "parallel")),
        name="merge_branches",
    )(*outs, gate_logits, gate_logits, gate_logits, gate_logits, w_branch)


def _headprep_kernel(x_ref, g_ref, cos_ref, sin_ref, rot_ref, o_ref, *, rope, scale):
    x = x_ref[0].astype(F32)
    var = jnp.mean(x * x, axis=-1, keepdims=True)
    y = x * lax.rsqrt(var + NORM_EPS) * g_ref[...]
    if rope:
        y = y * cos_ref[...] + _dot(y, rot_ref[...], HI) * sin_ref[...]
    o_ref[0] = (y * scale).astype(o_ref.dtype)


def head_prep(x, g, rope_tabs, scale, ts=1024):
    n, s, dh = x.shape
    ts = min(ts, s)
    rope = rope_tabs is not None
    if rope:
        cos, sin, rot = rope_tabs
    else:
        cos = sin = jnp.zeros((s, dh), F32)
        rot = jnp.zeros((dh, dh), F32)
    return pl.pallas_call(
        functools.partial(_headprep_kernel, rope=rope, scale=scale),
        grid=(n, s // ts),
        in_specs=[pl.BlockSpec((1, ts, dh), lambda h, i: (h, i, 0)),
                  pl.BlockSpec((1, dh), lambda h, i: (0, 0)),
                  pl.BlockSpec((ts, dh), lambda h, i: (i, 0)),
                  pl.BlockSpec((ts, dh), lambda h, i: (i, 0)),
                  pl.BlockSpec((dh, dh), lambda h, i: (0, 0))],
        out_specs=pl.BlockSpec((1, ts, dh), lambda h, i: (h, i, 0)),
        out_shape=jax.ShapeDtypeStruct((n, s, dh), BF16),
        compiler_params=_cparams(("parallel", "parallel")),
        name="head_prep",
    )(x, g.reshape(1, dh), cos, sin, rot)


def rope_tables(s):
    t = np.arange(s)
    row = (t // GRID_W).astype(np.float32)
    col = (t % GRID_W).astype(np.float32)
    nf = ROPE_AXIS_DIM // 2
    inv = jnp.asarray(ROPE_BASE, F32) ** (-jnp.arange(nf, dtype=F32) / nf)
    ar = jnp.asarray(row)[:, None] * inv
    ac = jnp.asarray(col)[:, None] * inv
    cos = jnp.concatenate([jnp.cos(ar), jnp.cos(ar), jnp.cos(ac), jnp.cos(ac)], axis=-1)
    sin = jnp.concatenate([jnp.sin(ar), jnp.sin(ar), jnp.sin(ac), jnp.sin(ac)], axis=-1)
    rot = np.zeros((SWA_HEAD_DIM, SWA_HEAD_DIM), np.float32)
    for d in range(SWA_HEAD_DIM):
        if d % ROPE_AXIS_DIM < nf:
            rot[d + nf, d] = -1.0
        else:
            rot[d - nf, d] = 1.0
    return cos, sin, jnp.asarray(rot)


NA_ROWS_PER_STEP = 8


def _pair_head_norm(x, g, lane_lo):
    sq = x * x
    lo = jnp.sum(jnp.where(lane_lo, sq, 0.0), axis=-1, keepdims=True)
    hi = jnp.sum(sq, axis=-1, keepdims=True) - lo
    inv = jnp.where(lane_lo, lax.rsqrt(lo * (1.0 / NA_HEAD_DIM) + NORM_EPS),
                    lax.rsqrt(hi * (1.0 / NA_HEAD_DIM) + NORM_EPS))
    return x * inv * g


def _na_kernel(q_ref, k_ref, v_ref, kc_ref, vc_ref, qg_ref, kg_ref, bias_ref, o_ref, kn_ref, kcn_ref, *, n_rows, seq):
    i = pl.program_id(2)
    n_nb = NA_WIN_R * GRID_W
    width = 2 * NA_HEAD_DIM
    norm_rows = min(seq, 512)

    @pl.when(i == 0)
    def _():
        lane_lo = lax.broadcasted_iota(jnp.int32, (norm_rows, width), 1) < NA_HEAD_DIM

        def nbody(t, carry):
            off = pl.multiple_of(t * norm_rows, norm_rows)
            x = k_ref[0, pl.ds(off, norm_rows), :].astype(F32)
            kn_ref[pl.ds(off, norm_rows), :] = _pair_head_norm(x, kg_ref[...], lane_lo).astype(BF16)
            return carry

        lax.fori_loop(0, seq // norm_rows, nbody, 0)
        lane_lo_c = lax.broadcasted_iota(jnp.int32, kcn_ref.shape, 1) < NA_HEAD_DIM
        kcn_ref[...] = _pair_head_norm(kc_ref[0].astype(F32), kg_ref[...], lane_lo_c).astype(BF16)

    kc = kcn_ref[...]
    vc = vc_ref[0]
    lane_lo = lax.broadcasted_iota(jnp.int32, (GRID_W, width), 1) < NA_HEAD_DIM
    scale = NA_HEAD_DIM ** -0.5

    def body(rr, carry):
        r = i * NA_ROWS_PER_STEP + rr
        r0 = jnp.clip(r - NA_WIN_R // 2, 0, n_rows - NA_WIN_R)
        qoff = pl.multiple_of(rr * GRID_W, GRID_W)
        koff = pl.multiple_of(r0 * GRID_W, GRID_W)
        q = _pair_head_norm(q_ref[0, pl.ds(qoff, GRID_W), :].astype(F32), qg_ref[...], lane_lo) * scale
        kn = kn_ref[pl.ds(koff, n_nb), :]
        vn = v_ref[0, pl.ds(koff, n_nb), :]
        outs = []
        for e in range(2):
            qe = jnp.where(lane_lo if e == 0 else jnp.logical_not(lane_lo), q, 0.0).astype(BF16)
            s_nb = _dot_nt(qe, kn) + bias_ref[r - r0, e]
            s_cx = _dot_nt(qe, kc)
            m = jnp.maximum(jnp.max(s_nb, axis=-1, keepdims=True), jnp.max(s_cx, axis=-1, keepdims=True))
            p_nb = jnp.exp(s_nb - m)
            p_cx = jnp.exp(s_cx - m)
            den = jnp.sum(p_nb, axis=-1, keepdims=True) + jnp.sum(p_cx, axis=-1, keepdims=True)
            outs.append((_dot(p_nb.astype(BF16), vn) + _dot(p_cx.astype(BF16), vc)) / den)
        o_ref[0, pl.ds(qoff, GRID_W), :] = jnp.where(lane_lo, outs[0], outs[1]).astype(o_ref.dtype)
        return carry

    lax.fori_loop(0, NA_ROWS_PER_STEP, body, 0, unroll=True)


def na_bias_table(rpb):
    nh = rpb.shape[0]
    wr, wc, gw = NA_WIN_R, NA_WIN_C, GRID_W
    rpb = rpb.astype(F32)

    def toeplitz(vec, n, period):
        tiled = jnp.broadcast_to(vec[..., None, :], vec.shape[:-1] + (n, period))
        flat = tiled.reshape(vec.shape[:-1] + (n * period,))[..., :n * (period - 1)]
        return flat.reshape(vec.shape[:-1] + (n, period - 1))[..., :n]

    rows = jnp.concatenate([rpb[:, wr - 1:], jnp.zeros((nh, 1, 2 * wc - 1), F32), rpb[:, :wr - 1]], axis=1)
    tab = toeplitz(jnp.moveaxis(rows, 1, -1), wr, 2 * wr)
    tab = jnp.moveaxis(tab, 1, -1)
    fill = jnp.zeros(tab.shape[:-1] + (2 * gw - (2 * wc - 1),), F32)
    cols_ext = jnp.concatenate([tab[..., wc - 1:], fill, tab[..., :wc - 1]], axis=-1)
    tab = toeplitz(cols_ext, gw, 2 * gw)
    cols = np.arange(gw)
    c_start = np.clip(cols - wc // 2, 0, gw - wc)
    valid = (cols[None, :] >= c_start[:, None]) & (cols[None, :] < c_start[:, None] + wc)
    tab = jnp.where(jnp.asarray(valid)[None, None, None], tab, NEG)
    tab = tab.transpose(1, 0, 3, 2, 4)
    return tab.reshape(wr, nh, gw, wr * gw)


def na_attention(qkv, qkv_c, q_norm, k_norm, bias):
    bsz, s, _ = qkv.shape
    lc = qkv_c.shape[1]
    n_rows = s // GRID_W
    tq = NA_ROWS_PER_STEP * GRID_W
    n_nb = NA_WIN_R * GRID_W
    width = 2 * NA_HEAD_DIM
    n_pairs = NA_HEADS // 2
    gq = jnp.tile(q_norm.astype(F32), 2).reshape(1, width)
    gk = jnp.tile(k_norm.astype(F32), 2).reshape(1, width)
    return pl.pallas_call(
        functools.partial(_na_kernel, n_rows=n_rows, seq=s),
        grid=(bsz, n_pairs, s // tq),
        in_specs=[pl.BlockSpec((1, tq, width), lambda b, p, i: (b, i, p)),
                  pl.BlockSpec((1, s, width), lambda b, p, i: (b, 0, n_pairs + p)),
                  pl.BlockSpec((1, s, width), lambda b, p, i: (b, 0, 2 * n_pairs + p)),
                  pl.BlockSpec((1, lc, width), lambda b, p, i: (b, 0, n_pairs + p)),
                  pl.BlockSpec((1, lc, width), lambda b, p, i: (b, 0, 2 * n_pairs + p)),
                  pl.BlockSpec((1, width), lambda b, p, i: (0, 0)),
                  pl.BlockSpec((1, width), lambda b, p, i: (0, 0)),
                  pl.BlockSpec((NA_WIN_R, 2, GRID_W, n_nb), lambda b, p, i: (0, p, 0, 0))],
        out_specs=pl.BlockSpec((1, tq, width), lambda b, p, i: (b, i, p)),
        out_shape=jax.ShapeDtypeStruct((bsz, s, NA_HEADS * NA_HEAD_DIM), BF16),
        scratch_shapes=[pltpu.VMEM((s, width), BF16), pltpu.VMEM((lc, width), BF16)],
        compiler_params=_cparams(("parallel", "parallel", "arbitrary")),
        name="na_attention",
    )(qkv, qkv, qkv, qkv_c, qkv_c, gq, gk, bias)


def _swa_kernel(sink_ref, q_ref, k_ref, v_ref, kc_ref, vc_ref, o_ref, *, seq):
    kh = pl.program_id(1)
    n = pl.program_id(2)
    span = 3 * SWA_BLOCK
    start = pl.multiple_of(jnp.clip((n - 1) * SWA_BLOCK, 0, seq - span), SWA_BLOCK)
    kw = k_ref[0, 0, pl.ds(start, span), :]
    vw = v_ref[0, 0, pl.ds(start, span), :]
    kc = kc_ref[0, 0]
    vc = vc_ref[0, 0]
    qpos = n * SWA_BLOCK + lax.broadcasted_iota(jnp.int32, (SWA_BLOCK, span), 0)
    kpos = start + lax.broadcasted_iota(jnp.int32, (SWA_BLOCK, span), 1)
    valid = jnp.abs(qpos - kpos) <= SWA_WINDOW
    for g in range(SWA_GROUP):
        q = q_ref[0, 0, g]
        s_loc = jnp.where(valid, _dot_nt(q, kw), NEG)
        s_ctx = _dot_nt(q, kc)
        sink = sink_ref[kh * SWA_GROUP + g]
        m = jnp.maximum(jnp.max(s_loc, axis=-1, keepdims=True), jnp.max(s_ctx, axis=-1, keepdims=True))
        m = jnp.maximum(m, sink)
        p_loc = jnp.exp(s_loc - m)
        p_ctx = jnp.exp(s_ctx - m)
        den = (jnp.sum(p_loc, axis=-1, keepdims=True) + jnp.sum(p_ctx, axis=-1, keepdims=True)
               + jnp.exp(sink - m))
        o = _dot(p_loc.astype(BF16), vw) + _dot(p_ctx.astype(BF16), vc)
        o_ref[0, 0, g] = (o / den).astype(o_ref.dtype)


def swa_attention(q, k, v, kc, vc, sink):
    bsz, hk, grp, s, dh = q.shape
    lc = kc.shape[2]
    assert s >= 3 * SWA_BLOCK
    return pl.pallas_call(
        functools.partial(_swa_kernel, seq=s),
        grid=(bsz, hk, s // SWA_BLOCK),
        in_specs=[pl.BlockSpec(memory_space=pltpu.SMEM),
                  pl.BlockSpec((1, 1, grp, SWA_BLOCK, dh), lambda b, h, n: (b, h, 0, n, 0)),
                  pl.BlockSpec((1, 1, s, dh), lambda b, h, n: (b, h, 0, 0)),
                  pl.BlockSpec((1, 1, s, dh), lambda b, h, n: (b, h, 0, 0)),
                  pl.BlockSpec((1, 1, lc, dh), lambda b, h, n: (b, h, 0, 0)),
                  pl.BlockSpec((1, 1, lc, dh), lambda b, h, n: (b, h, 0, 0))],
        out_specs=pl.BlockSpec((1, 1, grp, SWA_BLOCK, dh), lambda b, h, n: (b, h, 0, n, 0)),
        out_shape=jax.ShapeDtypeStruct((bsz, hk, grp, s, dh), BF16),
        compiler_params=_cparams(("parallel", "parallel", "arbitrary")),
        name="swa_attention",
    )(sink.astype(F32), q, k, v, kc, vc)


def _ctx_attn_kernel(sink_ref, q_ref, k_ref, v_ref, o_ref, *, grp):
    kh = pl.program_id(1)
    k = k_ref[0, 0]
    v = v_ref[0, 0]
    for g in range(grp):
        q = q_ref[0, 0, g]
        s = _dot_nt(q, k)
        sink = sink_ref[kh * grp + g]
        m = jnp.maximum(jnp.max(s, axis=-1, keepdims=True), sink)
        p = jnp.exp(s - m)
        den = jnp.sum(p, axis=-1, keepdims=True) + jnp.exp(sink - m)
        o_ref[0, 0, g] = (_dot(p.astype(BF16), v) / den).astype(o_ref.dtype)


def ctx_attention(q, k, v, sink):
    bsz, hk, grp, n, dh = q.shape
    return pl.pallas_call(
        functools.partial(_ctx_attn_kernel, grp=grp),
        grid=(bsz, hk),
        in_specs=[pl.BlockSpec(memory_space=pltpu.SMEM),
                  pl.BlockSpec((1, 1, grp, n, dh), lambda b, h: (b, h, 0, 0, 0)),
                  pl.BlockSpec((1, 1, n, dh), lambda b, h: (b, h, 0, 0)),
                  pl.BlockSpec((1, 1, n, dh), lambda b, h: (b, h, 0, 0))],
        out_specs=pl.BlockSpec((1, 1, grp, n, dh), lambda b, h: (b, h, 0, 0, 0)),
        out_shape=jax.ShapeDtypeStruct((bsz, hk, grp, n, dh), BF16),
        compiler_params=_cparams(("parallel", "parallel")),
        name="ctx_attention",
    )(sink.astype(F32), q, k, v)


CONV_PAD = 8
CONV_ROWS = 256
CONV_COLS = 256


def _conv_kernel(x_ref, w_ref, b_ref, o_ref, *, seq):
    rows = min(CONV_ROWS, seq)
    n = seq // rows

    def body(i, carry):
        base = pl.multiple_of(i * rows, rows)
        acc = jnp.zeros((rows, CONV_COLS), F32) + b_ref[...]
        prev_off = pl.multiple_of(jnp.maximum(base - CONV_PAD, 0), CONV_PAD)
        next_off = pl.multiple_of(jnp.minimum(base + rows, seq - CONV_PAD), CONV_PAD)
        prev = jnp.where(i > 0, x_ref[0, pl.ds(prev_off, CONV_PAD), :], 0.0)
        nxt = jnp.where(i < n - 1, x_ref[0, pl.ds(next_off, CONV_PAD), :], 0.0)
        halo = jnp.concatenate([prev, x_ref[0, pl.ds(base, rows), :], nxt], axis=0)
        for k in range(SSD_CONV_K):
            off = CONV_PAD + k - SSD_CONV_K // 2
            acc = acc + w_ref[k:k + 1, :] * halo[off:off + rows, :]
        o_ref[0, pl.ds(base, rows), :] = _silu(acc).astype(o_ref.dtype)
        return carry

    lax.fori_loop(0, seq // rows, body, 0)


def conv_silu(xbc, w, b):
    bsz, s, c = xbc.shape
    return pl.pallas_call(
        functools.partial(_conv_kernel, seq=s),
        grid=(bsz, c // CONV_COLS),
        in_specs=[pl.BlockSpec((1, s, CONV_COLS), lambda bb, j: (bb, 0, j)),
                  pl.BlockSpec((SSD_CONV_K, CONV_COLS), lambda bb, j: (0, j)),
                  pl.BlockSpec((1, CONV_COLS), lambda bb, j: (0, j))],
        out_specs=pl.BlockSpec((1, s, CONV_COLS), lambda bb, j: (bb, 0, j)),
        out_shape=jax.ShapeDtypeStruct((bsz, s, c), BF16),
        compiler_params=_cparams(("parallel", "parallel")),
        name="conv_silu",
    )(xbc, w.astype(F32), b.reshape(1, c).astype(F32))


def _scan_masks(length, d):
    row = lax.broadcasted_iota(jnp.int32, (length, length), 0)
    col = lax.broadcasted_iota(jnp.int32, (length, length), 1)
    return (row - col) * (1 - 2 * d) >= 0


def _ssd_kernel(xs_ref, bm_ref, cm_ref, bmt_ref, dt_ref, dtt_ref, alog_ref, alogt_ref, dtb_ref, dtbt_ref,
                h0_ref, y_ref, hout_ref, state_ref, *, n_chunks):
    d = pl.program_id(0)
    c = pl.program_id(2)

    @pl.when(c == 0)
    def _():
        state_ref[...] = h0_ref[0, 0]

    length = SSD_CHUNK
    incl = _scan_masks(length, d)
    tri = incl.astype(F32)
    dt = _softplus(dt_ref[0, 0] + dtb_ref[0])
    dtt = _softplus(dtt_ref[0, 0] + dtbt_ref[0])
    ad = dt * (-jnp.exp(alog_ref[0]))
    adt = dtt * (-jnp.exp(alogt_ref[0]))
    acs = _dot(tri, ad, HI)
    acst = _dot_nt(adt, tri, HI)
    tot = jnp.sum(ad, axis=0, keepdims=True)
    tott = jnp.sum(adt, axis=1, keepdims=True)
    rep = SSD_HEADS // SSD_GROUPS
    for gi in range(SSD_GROUPS):
        gsl = slice(gi * SSD_STATE, (gi + 1) * SSD_STATE)
        cg = cm_ref[0, :, gsl]
        bg = bm_ref[0, :, gsl]
        bgt = bmt_ref[0, gsl, :].astype(F32)
        cb = _dot_nt(cg, bg)
        for hh in range(rep):
            h = gi * rep + hh
            psl = slice(h * SSD_HEADDIM, (h + 1) * SSD_HEADDIM)
            a_col = acs[:, h:h + 1]
            a_row = acst[h:h + 1, :]
            lmat = jnp.exp(jnp.where(incl, a_col - a_row, NEG))
            xdt = (xs_ref[0, :, psl].astype(F32) * dt[:, h:h + 1]).astype(BF16)
            st = state_ref[h]
            y = _dot((cb * lmat).astype(BF16), xdt)
            y = y + jnp.exp(a_col) * _dot(cg, st.astype(BF16))
            dec = jnp.exp(tott[h:h + 1, :] - a_row)
            state_ref[h] = jnp.exp(tot[:, h:h + 1]) * st + _dot((bgt * dec).astype(BF16), xdt)
            y_ref[0, 0, :, psl] = y.astype(y_ref.dtype)

    @pl.when(c == n_chunks - 1)
    def _():
        hout_ref[0, 0] = state_ref[...]


def _chunk_index(d, c, n_chunks):
    return c + d * (n_chunks - 1 - 2 * c)


def ssd_scan(xbc_act, dt_raw, a_log, dt_bias, h0):
    bsz, s, _ = xbc_act.shape
    nc = s // SSD_CHUNK
    hh = SSD_HEADS
    ng = SSD_GROUPS * SSD_STATE
    bmt = jnp.swapaxes(xbc_act[:, :, SSD_D_INNER:SSD_D_INNER + ng], 1, 2)
    dt2 = dt_raw.reshape(bsz, s, 2, hh).transpose(2, 0, 1, 3)
    dtt = dt2.transpose(0, 1, 3, 2)
    a_log = a_log.astype(F32)
    dt_bias = dt_bias.astype(F32)
    cmap = functools.partial(_chunk_index, n_chunks=nc)
    nb_x = SSD_D_INNER // ng
    return pl.pallas_call(
        functools.partial(_ssd_kernel, n_chunks=nc),
        grid=(2, bsz, nc),
        in_specs=[pl.BlockSpec((1, SSD_CHUNK, SSD_D_INNER), lambda d, b, c: (b, cmap(d, c), 0)),
                  pl.BlockSpec((1, SSD_CHUNK, ng), lambda d, b, c: (b, cmap(d, c), nb_x)),
                  pl.BlockSpec((1, SSD_CHUNK, ng), lambda d, b, c: (b, cmap(d, c), nb_x + 1)),
                  pl.BlockSpec((1, ng, SSD_CHUNK), lambda d, b, c: (b, 0, cmap(d, c))),
                  pl.BlockSpec((1, 1, SSD_CHUNK, hh), lambda d, b, c: (d, b, cmap(d, c), 0)),
                  pl.BlockSpec((1, 1, hh, SSD_CHUNK), lambda d, b, c: (d, b, 0, cmap(d, c))),
                  pl.BlockSpec((1, 1, hh), lambda d, b, c: (d, 0, 0)),
                  pl.BlockSpec((1, hh, 1), lambda d, b, c: (d, 0, 0)),
                  pl.BlockSpec((1, 1, hh), lambda d, b, c: (d, 0, 0)),
                  pl.BlockSpec((1, hh, 1), lambda d, b, c: (d, 0, 0)),
                  pl.BlockSpec((1, 1, hh, SSD_STATE, SSD_HEADDIM), lambda d, b, c: (d, b, 0, 0, 0))],
        out_specs=[pl.BlockSpec((1, 1, SSD_CHUNK, SSD_D_INNER), lambda d, b, c: (d, b, cmap(d, c), 0)),
                   pl.BlockSpec((1, 1, hh, SSD_STATE, SSD_HEADDIM), lambda d, b, c: (d, b, 0, 0, 0))],
        out_shape=[jax.ShapeDtypeStruct((2, bsz, s, SSD_D_INNER), BF16),
                   jax.ShapeDtypeStruct((2, bsz, hh, SSD_STATE, SSD_HEADDIM), F32)],
        scratch_shapes=[pltpu.VMEM((hh, SSD_STATE, SSD_HEADDIM), F32)],
        compiler_params=_cparams(("parallel", "parallel", "arbitrary")),
        name="ssd_scan",
    )(xbc_act, xbc_act, xbc_act, bmt, dt2, dtt,
      a_log.reshape(2, 1, hh), a_log.reshape(2, hh, 1), dt_bias.reshape(2, 1, hh), dt_bias.reshape(2, hh, 1), h0)


def _ssd_finish_kernel(y_ref, xs_ref, z_ref, d_ref, g_ref, o_ref):
    y = y_ref[0, 0].astype(F32) + y_ref[1, 0].astype(F32) + d_ref[...] * xs_ref[0].astype(F32)
    z = z_ref[0].astype(F32)
    u = y * _silu(z)
    var = jnp.mean(u * u, axis=-1, keepdims=True)
    o_ref[0] = (u * lax.rsqrt(var + NORM_EPS) * g_ref[...]).astype(o_ref.dtype)


def ssd_finish(y, xbc_act, z, d_skip, norm_g, ts=512):
    _, bsz, s, di = y.shape
    ts = min(ts, s)
    dvec = jnp.repeat(d_skip.astype(F32), SSD_HEADDIM).reshape(1, di)
    return pl.pallas_call(
        _ssd_finish_kernel,
        grid=(bsz, s // ts),
        in_specs=[pl.BlockSpec((2, 1, ts, di), lambda b, i: (0, b, i, 0)),
                  pl.BlockSpec((1, ts, di), lambda b, i: (b, i, 0)),
                  pl.BlockSpec((1, ts, di), lambda b, i: (b, i, 0)),
                  pl.BlockSpec((1, di), lambda b, i: (0, 0)),
                  pl.BlockSpec((1, di), lambda b, i: (0, 0))],
        out_specs=pl.BlockSpec((1, ts, di), lambda b, i: (b, i, 0)),
        out_shape=jax.ShapeDtypeStruct((bsz, s, di), BF16),
        compiler_params=_cparams(("parallel", "parallel")),
        name="ssd_finish",
    )(y, xbc_act, z, dvec, norm_g.reshape(1, di).astype(F32))


GLA_STEP = 128


def _gla_kernel(q_ref, k_ref, kt_ref, v_ref, glr_ref, glrt_ref, wg_ref, wgt_ref, bg_ref, bgt_ref,
                s0_ref, o_ref, sout_ref, state_ref, *, n_steps, reverse):
    c = pl.program_id(1)

    @pl.when(c == 0)
    def _():
        state_ref[...] = s0_ref[0]

    length = GLA_CHUNK
    incl = _scan_masks(length, int(reverse))
    tri = incl.astype(F32)
    subs = range(GLA_STEP // length)
    for sub in (reversed(subs) if reverse else subs):
        tsl = slice(sub * length, (sub + 1) * length)
        g = _log_sigmoid(_dot(glr_ref[0, tsl, :], wg_ref[...], HI) + bg_ref[...]) / GLA_TAU
        gt = _log_sigmoid(_dot(wgt_ref[...], glrt_ref[0, :, tsl], HI) + bgt_ref[...]) / GLA_TAU
        gc = _dot(tri, g, HI)
        gct = _dot_nt(gt, tri, HI)
        tott = jnp.sum(gt, axis=1, keepdims=True)
        q_in = (q_ref[0, tsl, :].astype(F32) * (GLA_HEAD_K ** -0.5) * jnp.exp(gc)).astype(BF16)
        k_in = (k_ref[0, tsl, :].astype(F32) * jnp.exp(-gc)).astype(BF16)
        k_out_t = (kt_ref[0, :, tsl].astype(F32) * jnp.exp(tott - gct)).astype(BF16)
        dec_t = jnp.exp(tott)
        for h in range(GLA_HEADS):
            ksl = slice(h * GLA_HEAD_K, (h + 1) * GLA_HEAD_K)
            vsl = slice(h * GLA_HEAD_V, (h + 1) * GLA_HEAD_V)
            qh = q_in[:, ksl]
            vh = v_ref[0, tsl, vsl]
            att = jnp.where(incl, _dot_nt(qh, k_in[:, ksl]), 0.0)
            st = state_ref[h]
            o = _dot(att.astype(BF16), vh) + _dot(qh, st.astype(BF16))
            state_ref[h] = st * dec_t[ksl, :] + _dot(k_out_t[ksl, :], vh)
            o_ref[0, tsl, vsl] = o.astype(o_ref.dtype)

    @pl.when(c == n_steps - 1)
    def _():
        sout_ref[0] = state_ref[...]


def gla_scan(q, k, v, glr, w_gate, b_gate, s0):
    bsz, s, kd = q.shape
    vd = v.shape[-1]
    ns = s // GLA_STEP
    r = GLA_GATE_RANK
    kt = jnp.swapaxes(k, 1, 2)
    w_gate = w_gate.astype(F32)
    b_gate = b_gate.astype(F32)
    outs, states = [], []
    for d in range(2):
        cmap = (lambda c: ns - 1 - c) if d else (lambda c: c)
        glr_d = glr[:, :, d * r:(d + 1) * r]
        o, st = pl.pallas_call(
            functools.partial(_gla_kernel, n_steps=ns, reverse=bool(d)),
            grid=(bsz, ns),
            in_specs=[pl.BlockSpec((1, GLA_STEP, kd), lambda b, c, cmap=cmap: (b, cmap(c), 0)),
                      pl.BlockSpec((1, GLA_STEP, kd), lambda b, c, cmap=cmap: (b, cmap(c), 0)),
                      pl.BlockSpec((1, kd, GLA_STEP), lambda b, c, cmap=cmap: (b, 0, cmap(c))),
                      pl.BlockSpec((1, GLA_STEP, vd), lambda b, c, cmap=cmap: (b, cmap(c), 0)),
                      pl.BlockSpec((1, GLA_STEP, r), lambda b, c, cmap=cmap: (b, cmap(c), 0)),
                      pl.BlockSpec((1, r, GLA_STEP), lambda b, c, cmap=cmap: (b, 0, cmap(c))),
                      pl.BlockSpec((r, kd), lambda b, c: (0, 0)),
                      pl.BlockSpec((kd, r), lambda b, c: (0, 0)),
                      pl.BlockSpec((1, kd), lambda b, c: (0, 0)),
                      pl.BlockSpec((kd, 1), lambda b, c: (0, 0)),
                      pl.BlockSpec((1, GLA_HEADS, GLA_HEAD_K, GLA_HEAD_V), lambda b, c: (b, 0, 0, 0))],
            out_specs=[pl.BlockSpec((1, GLA_STEP, vd), lambda b, c, cmap=cmap: (b, cmap(c), 0)),
                       pl.BlockSpec((1, GLA_HEADS, GLA_HEAD_K, GLA_HEAD_V), lambda b, c: (b, 0, 0, 0))],
            out_shape=[jax.ShapeDtypeStruct((bsz, s, vd), BF16),
                       jax.ShapeDtypeStruct((bsz, GLA_HEADS, GLA_HEAD_K, GLA_HEAD_V), F32)],
            scratch_shapes=[pltpu.VMEM((GLA_HEADS, GLA_HEAD_K, GLA_HEAD_V), F32)],
            compiler_params=_cparams(("parallel", "arbitrary")),
            name="gla_scan_bwd" if d else "gla_scan_fwd",
        )(q, k, kt, v, glr_d, jnp.swapaxes(glr_d, 1, 2), w_gate[d], w_gate[d].T,
          b_gate[d].reshape(1, kd), b_gate[d].reshape(kd, 1), s0[d])
        outs.append(o)
        states.append(st)
    return outs, states


def _gla_finish_kernel(of_ref, ob_ref, r_ref, g_ref, out_ref):
    o = of_ref[0].astype(F32) + ob_ref[0].astype(F32)
    r = r_ref[0].astype(F32)
    for h in range(GLA_HEADS):
        vsl = slice(h * GLA_HEAD_V, (h + 1) * GLA_HEAD_V)
        oh = o[:, vsl]
        var = jnp.mean(oh * oh, axis=-1, keepdims=True)
        y = oh * lax.rsqrt(var + NORM_EPS) * g_ref[...]
        out_ref[0, :, vsl] = (y * _silu(r[:, vsl])).astype(out_ref.dtype)


def gla_finish(o, r, norm_g, ts=512):
    bsz, s, vd = o[0].shape
    ts = min(ts, s)
    return pl.pallas_call(
        _gla_finish_kernel,
        grid=(bsz, s // ts),
        in_specs=[pl.BlockSpec((1, ts, vd), lambda b, i: (b, i, 0)),
                  pl.BlockSpec((1, ts, vd), lambda b, i: (b, i, 0)),
                  pl.BlockSpec((1, ts, vd), lambda b, i: (b, i, 0)),
                  pl.BlockSpec((1, GLA_HEAD_V), lambda b, i: (0, 0))],
        out_specs=pl.BlockSpec((1, ts, vd), lambda b, i: (b, i, 0)),
        out_shape=jax.ShapeDtypeStruct((bsz, s, vd), BF16),
        compiler_params=_cparams(("parallel", "parallel")),
        name="gla_finish",
    )(o[0], o[1], r, norm_g.reshape(1, GLA_HEAD_V).astype(F32))


PEER_SCORE_TOKENS = 128
PEER_TOKENS = 128
PEER_SLOTS = 8
PEER_CAND_ROWS = -(-sum(PEER_TOPK // (a + 1) for a in range(PEER_TOPK)) // 8) * 8
PEER_SLAB = 2 * (D_MODEL // 2) // V7X_LANES
PEER_PITCH = PEER_SLAB + 4


def _extract_topk(s, rows, vals_ref, pos_ref):
    iota = lax.broadcasted_iota(jnp.int32, s.shape, 0).astype(F32)

    def body(j, cur):
        m = jnp.max(cur, axis=0, keepdims=True)
        pos = jnp.min(jnp.where(cur == m, iota, float(rows)), axis=0, keepdims=True)
        vals_ref[pl.ds(j, 1), :] = m
        pos_ref[pl.ds(j, 1), :] = pos
        return jnp.where(iota == pos, -jnp.inf, cur)

    lax.fori_loop(0, PEER_TOPK, body, s)


def _peer_score_kernel(q_ref, k1_ref, k2_ref, idx_ref, gate_ref, v12_ref, p12_ref, vt_ref, pt_ref, cand_ref):
    half = PEER_QDIM // 2
    tt = q_ref.shape[0]
    q = q_ref[...]
    s12 = jnp.concatenate([_dot_nt(k1_ref[...], q[:, :half], HI), _dot_nt(k2_ref[...], q[:, half:], HI)], axis=1)
    _extract_topk(s12, PEER_NKEYS, v12_ref, p12_ref)
    v1 = v12_ref[:, 0:tt]
    v2 = v12_ref[:, tt:2 * tt]
    p1 = p12_ref[:, 0:tt]
    p2 = p12_ref[:, tt:2 * tt]
    widths = [PEER_TOPK // (a + 1) for a in range(PEER_TOPK)]
    starts = [sum(widths[:a]) for a in range(PEER_TOPK)]
    n_cand = sum(widths)
    for a in range(PEER_TOPK):
        cand_ref[starts[a]:starts[a] + widths[a], :] = v1[a:a + 1, :] + v2[0:widths[a], :]
    cand_ref[n_cand:, :] = jnp.full((cand_ref.shape[0] - n_cand, tt), -jnp.inf, F32)
    _extract_topk(cand_ref[...], cand_ref.shape[0], vt_ref, pt_ref)
    pos = pt_ref[...]
    ia = jnp.zeros_like(pos)
    ib = pos
    for a in range(1, PEER_TOPK):
        past = pos >= float(starts[a])
        ia = ia + jnp.where(past, 1.0, 0.0)
        ib = ib - jnp.where(past, float(widths[a - 1]), 0.0)
    e1 = jnp.zeros_like(pos)
    e2 = jnp.zeros_like(pos)
    for a in range(PEER_TOPK):
        e1 = e1 + jnp.where(ia == float(a), p1[a:a + 1, :], 0.0)
        e2 = e2 + jnp.where(ib == float(a), p2[a:a + 1, :], 0.0)
    idx_ref[...] = (e1 * float(PEER_NKEYS) + e2).astype(jnp.int32)
    top = vt_ref[...]
    p = jnp.exp(top - top[0:1, :])
    gate_ref[...] = p / jnp.sum(p, axis=0, keepdims=True)


def peer_retrieve(q, k1, k2):
    t = q.shape[0]
    tt = min(PEER_SCORE_TOKENS, t)
    kk = PEER_TOPK
    return pl.pallas_call(
        _peer_score_kernel,
        grid=(t // tt, PEER_HEADS),
        in_specs=[pl.BlockSpec((tt, PEER_QDIM), lambda i, h: (i, h)),
                  pl.BlockSpec((PEER_NKEYS, PEER_QDIM // 2), lambda i, h: (0, 0)),
                  pl.BlockSpec((PEER_NKEYS, PEER_QDIM // 2), lambda i, h: (0, 0))],
        out_specs=[pl.BlockSpec((kk, tt), lambda i, h: (h, i)),
                   pl.BlockSpec((kk, tt), lambda i, h: (h, i))],
        out_shape=[jax.ShapeDtypeStruct((PEER_NSEL, t), jnp.int32),
                   jax.ShapeDtypeStruct((PEER_NSEL, t), F32)],
        scratch_shapes=[pltpu.VMEM((kk, 2 * tt), F32), pltpu.VMEM((kk, 2 * tt), F32),
                        pltpu.VMEM((kk, tt), F32), pltpu.VMEM((kk, tt), F32),
                        pltpu.VMEM((PEER_CAND_ROWS, tt), F32)],
        compiler_params=_cparams(("parallel", "parallel")),
        name="peer_retrieve",
    )(q, k1.astype(F32), k2.astype(F32))


def _gelu(x):
    return 0.5 * x * (1.0 + lax.erf(x * (1.0 / math.sqrt(2.0))))


def pack_expert_tables(u, v):
    def pack(tab):
        bits = lax.bitcast_convert_type(tab.astype(BF16), jnp.uint16).astype(jnp.uint32)
        half = tab.shape[1] // 2
        return bits[:, :half] | (bits[:, half:] << 16)

    n_exp = u.shape[0]
    slabs = jnp.concatenate([pack(u).reshape(n_exp, -1, V7X_LANES), pack(v).reshape(n_exp, -1, V7X_LANES)], axis=1)
    return slabs.reshape(n_exp * PEER_SLAB, V7X_LANES)


def _unpack_pair(words):
    lo = pltpu.bitcast(words << 16, F32)
    hi = pltpu.bitcast(words & jnp.uint32(0xFFFF0000), F32)
    return lo, hi


def _peer_expert_kernel(idx_ref, gate_ref, h_ref, x_ref, gt_ref, uv_hbm, o_ref, *scratch, tokens):
    bufs, sem = scratch[:PEER_SLOTS], scratch[PEER_SLOTS]
    half = D_MODEL // 2
    n_chunks = PEER_SLAB // 2
    per_step = PEER_NSEL // PEER_SLAB
    depth = PEER_SLOTS - 1

    def row_copy(e, slot, k):
        src = uv_hbm.at[pl.ds(pl.multiple_of(e * PEER_SLAB, PEER_SLAB), PEER_SLAB), :]
        return pltpu.make_async_copy(src, bufs[slot].at[pl.ds(k * PEER_PITCH, PEER_SLAB), :], sem.at[slot])

    def issue(tok, slot, ks):
        for k in ks:
            row_copy(idx_ref[tok, k], slot, k).start(priority=k % 2)

    def wait(slot):
        n = PEER_NSEL * PEER_SLAB
        pltpu.make_async_copy(uv_hbm.at[pl.ds(0, n), :], bufs[slot].at[pl.ds(0, n), :], sem.at[slot]).wait()

    lane = lax.broadcasted_iota(jnp.int32, (PEER_NSEL, tokens), 1)

    def chunk(slot, row):
        return _unpack_pair(bufs[slot][pl.ds(row, PEER_NSEL, stride=PEER_PITCH), :])

    def compute(j, slot, prefetch):
        def maybe_issue(step):
            if prefetch:
                issue(j + depth, (slot + depth) % PEER_SLOTS, range(step * per_step, (step + 1) * per_step))

        gcol = jnp.sum(jnp.where(lane == j, gate_ref[...], 0.0), axis=1, keepdims=True)
        p = jnp.zeros((PEER_NSEL, V7X_LANES), F32)
        t_lo = h_ref[pl.ds(j, 1), 0:half]
        t_hi = h_ref[pl.ds(j, 1), half:2 * half]
        for c in range(n_chunks):
            maybe_issue(c)
            ua, ub = chunk(slot, c)
            cols = slice(c * V7X_LANES, (c + 1) * V7X_LANES)
            p = p + ua * t_lo[:, cols] + ub * t_hi[:, cols]
        act = jnp.sum(p, axis=1, keepdims=True)
        a = jnp.broadcast_to(_gelu(act) * gcol, (PEER_NSEL, V7X_LANES))
        out_lo, out_hi = [], []
        for c in range(n_chunks):
            maybe_issue(n_chunks + c)
            va, vb = chunk(slot, n_chunks + c)
            out_lo.append(jnp.sum(a * va, axis=0, keepdims=True))
            out_hi.append(jnp.sum(a * vb, axis=0, keepdims=True))
        gt = gt_ref[0]
        for cols, parts in ((slice(0, half), out_lo), (slice(half, 2 * half), out_hi)):
            o_ref[0, pl.ds(j, 1), cols] = (x_ref[0, pl.ds(j, 1), cols]
                                           + gt[:, cols] * jnp.concatenate(parts, axis=1))

    for j0 in range(depth):
        issue(j0, j0 % PEER_SLOTS, range(PEER_NSEL))

    def body(it, carry):
        for s in range(PEER_SLOTS):
            wait(s)
            compute(it * PEER_SLOTS + s, s, True)
        return carry

    n_main = (tokens - depth) // PEER_SLOTS
    lax.fori_loop(0, n_main, body, 0)
    for j in range(n_main * PEER_SLOTS, tokens):
        wait(j % PEER_SLOTS)
        compute(j, j % PEER_SLOTS, j + depth < tokens)


def peer_experts(idx, gate, h, x, gt, uv):
    bsz, s, dm = x.shape
    t = bsz * s
    tt = min(PEER_TOKENS, s)
    nt = s // tt
    per_batch = gt.shape[0] == bsz and bsz > 1
    g_map = (lambda b, i: (b, 0, 0)) if per_batch else (lambda b, i: (0, 0, 0))
    out = pl.pallas_call(
        functools.partial(_peer_expert_kernel, tokens=tt),
        grid=(bsz, nt),
        in_specs=[pl.BlockSpec((tt, PEER_NSEL), lambda b, i: (b * nt + i, 0), memory_space=pltpu.SMEM),
                  pl.BlockSpec((PEER_NSEL, tt), lambda b, i: (0, b * nt + i)),
                  pl.BlockSpec((tt, dm), lambda b, i: (b * nt + i, 0)),
                  pl.BlockSpec((1, tt, dm), lambda b, i: (b, i, 0)),
                  pl.BlockSpec((1, 1, dm), g_map),
                  pl.BlockSpec(memory_space=pl.ANY)],
        out_specs=pl.BlockSpec((1, tt, dm), lambda b, i: (b, i, 0)),
        out_shape=jax.ShapeDtypeStruct((bsz, s, dm), F32),
        scratch_shapes=[pltpu.VMEM((PEER_NSEL * PEER_PITCH, V7X_LANES), jnp.uint32) for _ in range(PEER_SLOTS)]
        + [pltpu.SemaphoreType.DMA((PEER_SLOTS,))],
        compiler_params=pltpu.CompilerParams(dimension_semantics=("arbitrary", "arbitrary"),
                                             vmem_limit_bytes=VMEM_LIMIT),
        name="peer_experts",
    )(idx.T, gate, h, x.reshape(bsz, s, dm), gt, uv)
    return out


def peer_sublayer(x, g_norm, shift, scale, gt, wq, k1, k2, uv):
    bsz, s, dm = x.shape
    h_bf, h_f32 = norm_modulate(x, g_norm, shift, scale, (BF16, F32))
    q = matmul(h_bf.reshape(bsz * s, dm), wq, F32)
    idx, gate = peer_retrieve(q, k1, k2)
    return peer_experts(idx, gate, h_f32.reshape(bsz * s, dm), x, gt, uv)


def _head_major(t, n_heads):
    bsz, s, _ = t.shape
    return t.reshape(bsz, s, n_heads, -1).transpose(0, 2, 1, 3)


_PROJ_GROUPS = dict(z=(0, BF16), xbc=(1, F32), na=(3, BF16), gla_q=(4, BF16), gla_k=(5, BF16), gla_v=(6, BF16),
                    gla_r=(7, BF16), swa_q=(9, BF16), swa_k=(10, BF16), swa_v=(11, BF16), gates=(12, BF16))


def _split_w_in(w):
    o = IN_OFFS
    ws = {name: w[:, o[i]:o[i + 1]].astype(BF16) for name, (i, _) in _PROJ_GROUPS.items()}
    small = jnp.concatenate([w[:, o[2]:o[3]], w[:, o[8]:o[9]]], axis=1)
    ws['small'] = jnp.pad(small, ((0, 0), (0, V7X_LANES - small.shape[1]))).astype(BF16)
    return ws


def _project(h, ws):
    bsz, s, dm = h.shape
    hf = h.reshape(bsz * s, dm)
    out = {name: matmul(hf, ws[name], dt).reshape(bsz, s, -1) for name, (_, dt) in _PROJ_GROUPS.items()}
    sm = matmul(hf, ws['small'], F32).reshape(bsz, s, V7X_LANES)
    out['dt'] = sm[:, :, :2 * SSD_HEADS]
    out['glr'] = sm[:, :, 2 * SSD_HEADS:2 * SSD_HEADS + 2 * GLA_GATE_RANK]
    return out


def _flat_heads(t):
    return t.reshape((-1,) + t.shape[-2:])


def mixer_sublayer(hx, hc, p, rope, need_ctx):
    bsz, s, _ = hx.shape
    lc = hc.shape[1]
    ws = _split_w_in(p['w_in'])
    px = _project(hx, ws)
    pc = _project(hc, ws)
    zeros_h = jnp.zeros((2, bsz, SSD_HEADS, SSD_STATE, SSD_HEADDIM), F32)
    zero_s = jnp.zeros((bsz, GLA_HEADS, GLA_HEAD_K, GLA_HEAD_V), F32)
    zeros_s = (zero_s, zero_s)

    act_c = conv_silu(pc['xbc'], p['ssd_conv_w'], p['ssd_conv_b'])
    act_x = conv_silu(px['xbc'], p['ssd_conv_w'], p['ssd_conv_b'])
    y_c, h_c = ssd_scan(act_c, pc['dt'], p['ssd_a_log'], p['ssd_dt_bias'], zeros_h)
    y_x, _ = ssd_scan(act_x, px['dt'], p['ssd_a_log'], p['ssd_dt_bias'], h_c)
    a_x = ssd_finish(y_x, act_x, px['z'], p['ssd_d'], p['ssd_norm_g'])

    dh = NA_HEAD_DIM
    scale = dh ** -0.5

    def na_heads(t):
        t = _head_major(t, 3 * NA_HEADS)
        q = head_prep(_flat_heads(t[:, :NA_HEADS]), p['na_q_norm'], None, scale).reshape(t[:, :NA_HEADS].shape)
        k = head_prep(_flat_heads(t[:, NA_HEADS:2 * NA_HEADS]), p['na_k_norm'], None, 1.0).reshape(q.shape)
        return q, k, t[:, 2 * NA_HEADS:]

    b_x = na_attention(px['na'], pc['na'], p['na_q_norm'], p['na_k_norm'], na_bias_table(p['na_rpb']))

    o_c, s_c = gla_scan(pc['gla_q'], pc['gla_k'], pc['gla_v'], pc['glr'], p['gla_w_gate'], p['gla_b_gate'], zeros_s)
    o_x, _ = gla_scan(px['gla_q'], px['gla_k'], px['gla_v'], px['glr'], p['gla_w_gate'], p['gla_b_gate'], s_c)
    c_x = gla_finish(o_x, px['gla_r'], p['gla_norm_g'])

    def swa_heads(q, k, v, tabs):
        qh = _head_major(q, SWA_HEADS)
        kh = _head_major(k, SWA_KV_HEADS)
        qn = head_prep(_flat_heads(qh), p['swa_q_norm'], tabs, SWA_HEAD_DIM ** -0.5).reshape(qh.shape)
        kn = head_prep(_flat_heads(kh), p['swa_k_norm'], tabs, 1.0).reshape(kh.shape)
        qn = qn.reshape(qh.shape[0], SWA_KV_HEADS, SWA_GROUP, qh.shape[2], SWA_HEAD_DIM)
        return qn, kn, _head_major(v, SWA_KV_HEADS)

    sq, sk, sv = swa_heads(px['swa_q'], px['swa_k'], px['swa_v'], rope)
    sqc, skc, svc = swa_heads(pc['swa_q'], pc['swa_k'], pc['swa_v'], None)
    d_x = swa_attention(sq, sk, sv, skc, svc, p['swa_sink'])
    d_x = d_x.reshape(bsz, SWA_HEADS, s, SWA_HEAD_DIM).transpose(0, 2, 1, 3).reshape(bsz, s, BRANCH_WIDTH)

    wb = p['w_branch'].astype(BF16)
    flat = lambda t: t.reshape(-1, t.shape[-1])
    m_x = merge_branches([flat(a_x), flat(b_x), flat(c_x), flat(d_x)], flat(px['gates']), wb)
    if not need_ctx:
        return m_x, None

    a_c = ssd_finish(y_c, act_c, pc['z'], p['ssd_d'], p['ssd_norm_g'])
    no_sink = jnp.full((NA_HEADS,), NEG, F32)
    nqc, nkc, nvc = na_heads(pc['na'])
    b_c = ctx_attention(nqc[:, :, None], nkc, nvc, no_sink)[:, :, 0]
    b_c = b_c.transpose(0, 2, 1, 3).reshape(bsz, lc, BRANCH_WIDTH)
    c_c = gla_finish(o_c, pc['gla_r'], p['gla_norm_g'])
    d_c = ctx_attention(sqc, skc, svc, p['swa_sink'])
    d_c = d_c.reshape(bsz, SWA_HEADS, lc, SWA_HEAD_DIM).transpose(0, 2, 1, 3).reshape(bsz, lc, BRANCH_WIDTH)
    m_c = merge_branches([flat(a_c), flat(b_c), flat(c_c), flat(d_c)], flat(pc['gates']), wb)
    return m_x, m_c


def kernel(x, c, ctx, c_ctx, w_ada, b_ada, g_norm1, g_norm2, w_in, ssd_conv_w, ssd_conv_b, ssd_a_log, ssd_dt_bias, ssd_d, ssd_norm_g, na_q_norm, na_k_norm, na_rpb, gla_w_gate, gla_b_gate, gla_norm_g, swa_q_norm, swa_k_norm, swa_sink, w_branch, w_out, peer_wq, peer_k1, peer_k2, peer_u, peer_v):
    bsz, s, dm = x.shape
    rope = rope_tables(s)
    n_cond = 8
    cc = jnp.zeros((n_cond, dm), F32).at[:bsz].set(c).at[bsz].set(c_ctx)
    for l in range(DEPTH):
        need_ctx = l < DEPTH - 1
        mod = ada_mod(cc, w_ada[l], b_ada[l])
        mx = [mod[:bsz, i * dm:(i + 1) * dm].reshape(bsz, 1, dm) for i in range(6)]
        mc = [mod[bsz:bsz + 1, i * dm:(i + 1) * dm].reshape(1, 1, dm) for i in range(6)]
        p = dict(w_in=w_in[l], ssd_conv_w=ssd_conv_w[l], ssd_conv_b=ssd_conv_b[l], ssd_a_log=ssd_a_log[l],
                 ssd_dt_bias=ssd_dt_bias[l], ssd_d=ssd_d[l], ssd_norm_g=ssd_norm_g[l],
                 na_q_norm=na_q_norm[l], na_k_norm=na_k_norm[l], na_rpb=na_rpb[l],
                 gla_w_gate=gla_w_gate[l], gla_b_gate=gla_b_gate[l], gla_norm_g=gla_norm_g[l],
                 swa_q_norm=swa_q_norm[l], swa_k_norm=swa_k_norm[l], swa_sink=swa_sink[l],
                 w_branch=w_branch[l])
        (hx,) = norm_modulate(x, g_norm1[l], mx[0], mx[1], (BF16,))
        (hc,) = norm_modulate(ctx, g_norm1[l], mc[0], mc[1], (BF16,))
        m_x, m_c = mixer_sublayer(hx, hc, p, rope, need_ctx)
        wo = w_out[l].astype(BF16)
        wq = peer_wq[l].astype(BF16)
        uv = pack_expert_tables(peer_u[l], peer_v[l])
        x = matmul_gated_residual(m_x, wo, x, mx[2])
        x = peer_sublayer(x, g_norm2[l], mx[3], mx[4], mx[5], wq, peer_k1[l], peer_k2[l], uv)
        if need_ctx:
            ctx = matmul_gated_residual(m_c, wo, ctx, mc[2])
            ctx = peer_sublayer(ctx, g_norm2[l], mc[3], mc[4], mc[5], wq, peer_k1[l], peer_k2[l], uv)
    return x
```
